```python
import math
import jax, jax.numpy as jnp
from jax import lax
import numpy as np

D_MODEL = 1024
BATCH = 8
SEQ = 2048
DEPTH = 2
DEC_BATCH = 128
DEC_SEQ = 4
PAST_LEN = 16384
PAGE_SIZE = 128

N_MEM = 256
N_EVEN = (DEPTH + 1) // 2
N_ODD = DEPTH // 2
A_WIDTH = D_MODEL // 2
A_DK = 128
A_HEADS = A_WIDTH // A_DK
A_DV = A_WIDTH // A_HEADS
B_WIDTH = D_MODEL - A_WIDTH
B_GROUP = 16
B_GROUPS = B_WIDTH // B_GROUP
B_STATE = 64
S5_MAX_RE = -1e-4
C_HEADS = 4
C_KEY = D_MODEL // 2
C_VAL = D_MODEL
C_DK = C_KEY // C_HEADS
C_DV = C_VAL // C_HEADS
C_GATE_RANK = 16
C_GATE_TAU = 16.0
CHUNK = 64
X_HEADS = 4
X_HD = D_MODEL // X_HEADS
FFN_DIM = 2816
CONV_W = 3
EPS = 1e-6

AB_COLS = 4 * A_WIDTH + B_WIDTH
C_COLS = 2 * C_KEY + 2 * C_VAL + C_GATE_RANK

kernel_name = "hgrn2_s5_gla_memxattn_convffn_step"


def rms_norm(x, g):
    xf = x.astype(jnp.float32)
    y = xf * lax.rsqrt(jnp.mean(xf * xf, axis=-1, keepdims=True) + EPS)
    return (y * g.astype(jnp.float32)).astype(x.dtype)


def gated_linear_attention(q, k, v, log_g, s0):
    bsz, L, H, _ = q.shape
    dv = v.shape[-1]
    c = math.gcd(CHUNK, L)
    n = L // c

    def to_chunks(t):
        return jnp.moveaxis(t.reshape(bsz, n, c, H, t.shape[-1]), 1, 0)

    qs, ks, vs, gs = to_chunks(q), to_chunks(k), to_chunks(v), to_chunks(log_g.astype(jnp.float32))
    mask = jnp.tril(jnp.ones((c, c), bool))[None, :, :, None, None]

    def step(S, inp):
        qc, kc, vc, gc = inp
        b = jnp.cumsum(gc, axis=1)
        b_last = b[:, -1:]
        diff = b[:, :, None] - b[:, None]
        decay = jnp.where(mask, jnp.exp(jnp.where(mask, diff, 0.0)), 0.0)
        scores = jnp.sum(qc[:, :, None] * kc[:, None] * decay, axis=-1)
        o = (jnp.einsum('bijh,bjhv->bihv', scores, vc)
             + jnp.einsum('bihk,bhkv->bihv', qc * jnp.exp(b), S))
        S_new = (jnp.exp(b_last[:, 0])[..., None] * S
                 + jnp.einsum('bjhk,bjhv->bhkv', kc * jnp.exp(b_last - b), vc))
        return S_new, o

    S, o = lax.scan(step, s0.astype(jnp.float32), (qs, ks, vs, gs))
    o = jnp.moveaxis(o, 0, 1).reshape(bsz, L, H, dv)
    return o, S


def hgrn2_mixer(q, f, i, g, lb, gnorm, s0):
    bsz, L, _ = q.shape
    forget = lb + (1.0 - lb) * jax.nn.sigmoid(f.astype(jnp.float32))
    key = 1.0 - forget
    log_g = jnp.log(forget)
    qh = jax.nn.silu(q).reshape(bsz, L, A_HEADS, A_DK)
    o, S = gated_linear_attention(qh, key.reshape(bsz, L, A_HEADS, A_DK),
                                  i.reshape(bsz, L, A_HEADS, A_DV),
                                  log_g.reshape(bsz, L, A_HEADS, A_DK), s0)
    o = rms_norm(o, gnorm) * jax.nn.silu(g.reshape(bsz, L, A_HEADS, A_DV).astype(jnp.float32))
    return o.reshape(bsz, L, A_WIDTH), S


def s5_mixer(u, lam_re, lam_im, log_step, b_re, b_im, c_re, c_im, d, w_glu, b_glu, x0_re, x0_im):
    bsz, L, _ = u.shape
    ug = u.reshape(bsz, L, B_GROUPS, B_GROUP).astype(jnp.float32)
    lr = jnp.minimum(lam_re.astype(jnp.float32), S5_MAX_RE)
    li = lam_im.astype(jnp.float32)
    dt = jnp.exp(log_step.astype(jnp.float32))[:, None]
    mag = jnp.exp(lr * dt)
    a_re = mag * jnp.cos(li * dt)
    a_im = mag * jnp.sin(li * dt)
    den = lr * lr + li * li
    z_re = ((a_re - 1.0) * lr + a_im * li) / den
    z_im = (a_im * lr - (a_re - 1.0) * li) / den
    bb_re = z_re[..., None] * b_re - z_im[..., None] * b_im
    bb_im = z_re[..., None] * b_im + z_im[..., None] * b_re
    bu_re = jnp.einsum('blgp,gnp->blgn', ug, bb_re)
    bu_im = jnp.einsum('blgp,gnp->blgn', ug, bb_im)
    a_re_t = jnp.broadcast_to(a_re, (1, L, B_GROUPS, B_STATE))
    a_im_t = jnp.broadcast_to(a_im, (1, L, B_GROUPS, B_STATE))

    def combine(e1, e2):
        a1r, a1i, h1r, h1i = e1
        a2r, a2i, h2r, h2i = e2
        return (a2r * a1r - a2i * a1i, a2r * a1i + a2i * a1r,
                a2r * h1r - a2i * h1i + h2r, a2r * h1i + a2i * h1r + h2i)

    pr, pi, hr, hi = lax.associative_scan(combine, (a_re_t, a_im_t, bu_re, bu_im), axis=1)
    x0r = x0_re.astype(jnp.float32)[:, None]
    x0i = x0_im.astype(jnp.float32)[:, None]
    xr = hr + pr * x0r - pi * x0i
    xi = hi + pr * x0i + pi * x0r
    y = (jnp.einsum('blgn,gpn->blgp', xr, c_re) - jnp.einsum('blgn,gpn->blgp', xi, c_im)
         + d * ug)
    y = jax.nn.gelu(y.reshape(bsz, L, B_WIDTH))
    y = y * jax.nn.sigmoid(y @ w_glu + b_glu)
    return y, xr[:, -1], xi[:, -1]


def gla_mixer(proj, w_gate_up, b_gate, gnorm, s0):
    bsz, L, _ = proj.shape
    q, k, v, r, gd = jnp.split(proj, [C_KEY, 2 * C_KEY, 2 * C_KEY + C_VAL, 2 * C_KEY + 2 * C_VAL], axis=-1)
    log_a = jax.nn.log_sigmoid((gd @ w_gate_up + b_gate).astype(jnp.float32)) / C_GATE_TAU
    o, S = gated_linear_attention((q * (C_DK ** -0.5)).reshape(bsz, L, C_HEADS, C_DK),
                                  k.reshape(bsz, L, C_HEADS, C_DK),
                                  v.reshape(bsz, L, C_HEADS, C_DV),
                                  log_a.reshape(bsz, L, C_HEADS, C_DK), s0)
    o = rms_norm(o, gnorm) * jax.nn.silu(r.reshape(bsz, L, C_HEADS, C_DV).astype(jnp.float32))
    return o.reshape(bsz, L, C_VAL), S


def memory_kv(mem, g, w_kv):
    bsz, m, _ = mem.shape
    k, v = jnp.split(rms_norm(mem, g) @ w_kv, 2, axis=-1)
    return k.reshape(bsz, m, X_HEADS, X_HD), v.reshape(bsz, m, X_HEADS, X_HD)


def cross_attend(h, mk, mv, w_q, w_o):
    bsz, L, _ = h.shape
    q = (h @ w_q).reshape(bsz, L, X_HEADS, X_HD)
    s = jnp.einsum('blhd,bmhd->bhlm', q, mk).astype(jnp.float32) * (X_HD ** -0.5)
    p = jax.nn.softmax(s, axis=-1).astype(mv.dtype)
    o = jnp.einsum('bhlm,bmhd->blhd', p, mv).reshape(bsz, L, D_MODEL)
    return o.astype(h.dtype) @ w_o


def conv_ffn(h, w_up, conv_w, conv_b, w_down, prev):
    L = h.shape[1]
    up = h @ w_up
    ext = jnp.concatenate([prev.astype(up.dtype), up], axis=1)
    c = conv_b + conv_w[0] * ext[:, 0:L]
    for j in range(1, CONV_W):
        c = c + conv_w[j] * ext[:, j:j + L]
    a, b = jnp.split(c, 2, axis=-1)
    y = (jax.nn.silu(a) * b).astype(h.dtype)
    return y @ w_down, ext[:, L:]


def trunk(x, mem_k, mem_v, s_hgrn, s5_re, s5_im, s_gla, s_conv, p):
    lb_all = jnp.cumsum(jax.nn.softmax(p['hgrn_lb'].astype(jnp.float32), axis=0), axis=0)
    new_hgrn, new_s5r, new_s5i, new_gla, new_conv = [], [], [], [], []
    for l in range(DEPTH):
        h = rms_norm(x, p['norm_mix'][l])
        if l % 2 == 0:
            e = l // 2
            proj = h @ p['w_in_ab'][e]
            q, f, i, g, u = jnp.split(proj, [A_WIDTH, 2 * A_WIDTH, 3 * A_WIDTH, 4 * A_WIDTH], axis=-1)
            o_a, s_a = hgrn2_mixer(q, f, i, g, lb_all[l], p['hgrn_gnorm'][e], s_hgrn[e])
            o_b, sr, si = s5_mixer(u, p['s5_lam_re'][e], p['s5_lam_im'][e], p['s5_log_step'][e],
                                   p['s5_b_re'][e], p['s5_b_im'][e], p['s5_c_re'][e], p['s5_c_im'][e],
                                   p['s5_d'][e], p['s5_w_glu'][e], p['s5_b_glu'][e], s5_re[e], s5_im[e])
            mixed = jnp.concatenate([o_a.astype(x.dtype), o_b.astype(x.dtype)], axis=-1) @ p['w_out_ab'][e]
            new_hgrn.append(s_a)
            new_s5r.append(sr)
            new_s5i.append(si)
        else:
            o_idx = l // 2
            o_c, s_c = gla_mixer(h @ p['w_in_c'][o_idx], p['gla_w_gate_up'][o_idx], p['gla_b_gate'][o_idx],
                                 p['gla_gnorm'][o_idx], s_gla[o_idx])
            mixed = o_c.astype(x.dtype) @ p['w_out_c'][o_idx]
            new_gla.append(s_c)
        x = x + mixed
        h = rms_norm(x, p['norm_cross'][l])
        x = x + cross_attend(h, mem_k[l], mem_v[l], p['xa_w_q'][l], p['xa_w_o'][l])
        h = rms_norm(x, p['norm_ffn'][l])
        ff, conv_state = conv_ffn(h, p['ffn_w_up'][l], p['ffn_conv_w'][l], p['ffn_conv_b'][l],
                                  p['ffn_w_down'][l], s_conv[l])
        x = x + ff
        new_conv.append(conv_state)
    y = rms_norm(x, p['norm_final'])
    return (y, jnp.stack(new_hgrn), jnp.stack(new_s5r), jnp.stack(new_s5i),
            jnp.stack(new_gla), jnp.stack(new_conv))


def setup_inputs(seed: int = 0) -> dict:
    key = jax.random.key(seed)
    ks = iter(jax.random.split(key, 64))

    def nrm(shape, scale):
        return scale * jax.random.normal(next(ks), shape, jnp.float32)

    D = D_MODEL
    F2 = 2 * FFN_DIM
    lam_im_base = jnp.pi * jnp.arange(B_STATE, dtype=jnp.float32)
    return {
        'x_prompt': nrm((BATCH, SEQ, D), 1.0),
        'x_sample': nrm((DEC_BATCH, DEC_SEQ, D), 1.0),
        'mem_prompt': nrm((BATCH, N_MEM, D), 1.0),
        'cache_mem_k': nrm((DEPTH, DEC_BATCH, N_MEM, X_HEADS, X_HD), 1.0),
        'cache_mem_v': nrm((DEPTH, DEC_BATCH, N_MEM, X_HEADS, X_HD), 1.0),
        'state_hgrn': nrm((N_EVEN, DEC_BATCH, A_HEADS, A_DK, A_DV), 0.5),
        'state_s5_re': nrm((N_EVEN, DEC_BATCH, B_GROUPS, B_STATE), 0.1),
        'state_s5_im': nrm((N_EVEN, DEC_BATCH, B_GROUPS, B_STATE), 0.1),
        'state_gla': nrm((N_ODD, DEC_BATCH, C_HEADS, C_DK, C_DV), 1.0),
        'state_ffn_conv': nrm((DEPTH, DEC_BATCH, CONV_W - 1, F2), 1.0),
        'norm_mix': 1.0 + nrm((DEPTH, D), 0.01),
        'norm_cross': 1.0 + nrm((DEPTH, D), 0.01),
        'norm_mem': 1.0 + nrm((DEPTH, D), 0.01),
        'norm_ffn': 1.0 + nrm((DEPTH, D), 0.01),
        'norm_final': 1.0 + nrm((D,), 0.01),
        'w_in_ab': nrm((N_EVEN, D, AB_COLS), D ** -0.5),
        'hgrn_lb': nrm((DEPTH + 1, A_WIDTH), 0.1),
        'hgrn_gnorm': 1.0 + nrm((N_EVEN, A_DV), 0.01),
        's5_lam_re': -0.5 + nrm((N_EVEN, B_GROUPS, B_STATE), 0.01),
        's5_lam_im': lam_im_base + nrm((N_EVEN, B_GROUPS, B_STATE), 0.01),
        's5_log_step': jax.random.uniform(next(ks), (N_EVEN, B_GROUPS), jnp.float32,
                                          math.log(1e-3), math.log(1e-1)),
        's5_b_re': nrm((N_EVEN, B_GROUPS, B_STATE, B_GROUP), (2 * B_GROUP) ** -0.5),
        's5_b_im': nrm((N_EVEN, B_GROUPS, B_STATE, B_GROUP), (2 * B_GROUP) ** -0.5),
        's5_c_re': nrm((N_EVEN, B_GROUPS, B_GROUP, B_STATE), (2 * B_STATE) ** -0.5),
        's5_c_im': nrm((N_EVEN, B_GROUPS, B_GROUP, B_STATE), (2 * B_STATE) ** -0.5),
        's5_d': nrm((N_EVEN, B_GROUPS, B_GROUP), 1.0),
        's5_w_glu': nrm((N_EVEN, B_WIDTH, B_WIDTH), B_WIDTH ** -0.5),
        's5_b_glu': nrm((N_EVEN, B_WIDTH), 0.01),
        'w_out_ab': nrm((N_EVEN, A_WIDTH + B_WIDTH, D), (A_WIDTH + B_WIDTH) ** -0.5),
        'w_in_c': nrm((N_ODD, D, C_COLS), D ** -0.5),
        'gla_w_gate_up': nrm((N_ODD, C_GATE_RANK, C_KEY), C_GATE_RANK ** -0.5),
        'gla_b_gate': nrm((N_ODD, C_KEY), 0.1),
        'gla_gnorm': 1.0 + nrm((N_ODD, C_DV), 0.01),
        'w_out_c': nrm((N_ODD, C_VAL, D), C_VAL ** -0.5),
        'xa_w_q': nrm((DEPTH, D, D), D ** -0.5),
        'xa_w_kv': nrm((DEPTH, D, 2 * D), D ** -0.5),
        'xa_w_o': nrm((DEPTH, D, D), D ** -0.5),
        'ffn_w_up': nrm((DEPTH, D, F2), D ** -0.5),
        'ffn_conv_w': nrm((DEPTH, CONV_W, F2), CONV_W ** -0.5),
        'ffn_conv_b': nrm((DEPTH, F2), 0.01),
        'ffn_w_down': nrm((DEPTH, FFN_DIM, D), FFN_DIM ** -0.5),
    }


def reference(x_prompt, x_sample, mem_prompt, cache_mem_k, cache_mem_v, state_hgrn, state_s5_re,
              state_s5_im, state_gla, state_ffn_conv, norm_mix, norm_cross, norm_mem, norm_ffn,
              norm_final, w_in_ab, hgrn_lb, hgrn_gnorm, s5_lam_re, s5_lam_im, s5_log_step, s5_b_re,
              s5_b_im, s5_c_re, s5_c_im, s5_d, s5_w_glu, s5_b_glu, w_out_ab, w_in_c, gla_w_gate_up,
              gla_b_gate, gla_gnorm, w_out_c, xa_w_q, xa_w_kv, xa_w_o, ffn_w_up, ffn_conv_w,
              ffn_conv_b, ffn_w_down):
    p = dict(norm_mix=norm_mix, norm_cross=norm_cross, norm_ffn=norm_ffn, norm_final=norm_final,
             w_in_ab=w_in_ab, hgrn_lb=hgrn_lb, hgrn_gnorm=hgrn_gnorm, s5_lam_re=s5_lam_re,
             s5_lam_im=s5_lam_im, s5_log_step=s5_log_step, s5_b_re=s5_b_re, s5_b_im=s5_b_im,
             s5_c_re=s5_c_re, s5_c_im=s5_c_im, s5_d=s5_d, s5_w_glu=s5_w_glu, s5_b_glu=s5_b_glu,
             w_out_ab=w_out_ab, w_in_c=w_in_c, gla_w_gate_up=gla_w_gate_up, gla_b_gate=gla_b_gate,
             gla_gnorm=gla_gnorm, w_out_c=w_out_c, xa_w_q=xa_w_q, xa_w_o=xa_w_o,
             ffn_w_up=ffn_w_up, ffn_conv_w=ffn_conv_w, ffn_conv_b=ffn_conv_b, ffn_w_down=ffn_w_down)

    mks, mvs = [], []
    for l in range(DEPTH):
        mk, mv = memory_kv(mem_prompt, norm_mem[l], xa_w_kv[l])
        mks.append(mk)
        mvs.append(mv)
    mem_k_p = jnp.stack(mks)
    mem_v_p = jnp.stack(mvs)
    f32 = jnp.float32
    y_prompt, hgrn_p, s5_re_p, s5_im_p, gla_p, conv_p = trunk(
        x_prompt, mem_k_p, mem_v_p,
        jnp.zeros((N_EVEN, BATCH, A_HEADS, A_DK, A_DV), f32),
        jnp.zeros((N_EVEN, BATCH, B_GROUPS, B_STATE), f32),
        jnp.zeros((N_EVEN, BATCH, B_GROUPS, B_STATE), f32),
        jnp.zeros((N_ODD, BATCH, C_HEADS, C_DK, C_DV), f32),
        jnp.zeros((DEPTH, BATCH, CONV_W - 1, 2 * FFN_DIM), f32), p)

    y_sample, hgrn_s, s5_re_s, s5_im_s, gla_s, conv_s = trunk(
        x_sample, cache_mem_k, cache_mem_v, state_hgrn, state_s5_re, state_s5_im, state_gla,
        state_ffn_conv, p)

    return (y_prompt, y_sample, hgrn_p, s5_re_p, s5_im_p, gla_p, mem_k_p, mem_v_p, conv_p,
            hgrn_s, s5_re_s, s5_im_s, gla_s, conv_s)
```

```python
import functools
import math

import jax
import jax.numpy as jnp
from jax import lax
from jax.experimental import pallas as pl
from jax.experimental.pallas import tpu as pltpu

F32 = jnp.float32
BF16 = jnp.bfloat16

EPS = 1e-6
S5_MAX_RE = -1e-4
GLA_GATE_TAU = 16.0
N_HEADS = 4
HEAD_DK = 128
S5_GROUP = 16
S5_STATE = 64
SUB_BLOCK = 16
VMEM_LIMIT = 56 * 1024 * 1024

_NT = (((1,), (1,)), ((), ()))
_TN = (((0,), (0,)), ((), ()))


def _cparams(*sem):
    return pltpu.CompilerParams(dimension_semantics=sem, vmem_limit_bytes=VMEM_LIMIT)


def _rms(x, g):
    return x * lax.rsqrt(jnp.mean(x * x, axis=-1, keepdims=True) + EPS) * g


def _sigmoid(x):
    return 1.0 / (1.0 + jnp.exp(-x))


def _silu(x):
    return x * _sigmoid(x)


def _row_tile(rows, want):
    t = min(rows, want)
    assert rows % t == 0, (rows, t)
    return t


def _norm_proj_kernel(x_ref, g_ref, w_ref, *o_refs, widths, has_norm):
    x = x_ref[...]
    h = (_rms(x, g_ref[...]) if has_norm else x).astype(BF16)
    col = 0
    for o_ref, wd in zip(o_refs, widths):
        for c0 in range(0, wd, 512):
            cw = min(512, wd - c0)
            o_ref[:, c0:c0 + cw] = jnp.dot(
                h, w_ref[:, col + c0:col + c0 + cw],
                preferred_element_type=F32).astype(o_ref.dtype)
        col += wd


def _norm_proj(x, g, w, widths, out_dtypes, *, has_norm=True, tm=512):
    m, k = x.shape
    nl, _, n = w.shape
    assert n == sum(widths)
    tm = _row_tile(m, tm)
    return pl.pallas_call(
        functools.partial(_norm_proj_kernel, widths=tuple(widths), has_norm=has_norm),
        grid=(nl, m // tm),
        in_specs=[
            pl.BlockSpec((tm, k), lambda l, i: (i, 0)),
            pl.BlockSpec((None, 1, k), lambda l, i: (l, 0, 0)),
            pl.BlockSpec((None, k, n), lambda l, i: (l, 0, 0)),
        ],
        out_specs=[pl.BlockSpec((None, tm, wd), lambda l, i: (l, i, 0)) for wd in widths],
        out_shape=[jax.ShapeDtypeStruct((nl, m, wd), dt) for wd, dt in zip(widths, out_dtypes)],
        compiler_params=_cparams("arbitrary", "arbitrary"),
        name="norm_proj",
    )(x, g, w)


def _proj_res_kernel(res_ref, *refs, n_in):
    acc = res_ref[...]
    for a_ref, w_ref in zip(refs[:n_in], refs[n_in:2 * n_in]):
        acc = acc + jnp.dot(a_ref[...], w_ref[...], preferred_element_type=F32)
    refs[2 * n_in][...] = acc


def _proj_res(res, a_list, w_list, *, tm=512):
    m, n = res.shape
    tm = _row_tile(m, tm)
    n_in = len(a_list)
    in_specs = [pl.BlockSpec((tm, n), lambda i: (i, 0))]
    in_specs += [pl.BlockSpec((tm, a.shape[1]), lambda i: (i, 0)) for a in a_list]
    in_specs += [pl.BlockSpec(w.shape, lambda i: (0, 0)) for w in w_list]
    return pl.pallas_call(
        functools.partial(_proj_res_kernel, n_in=n_in),
        grid=(m // tm,),
        in_specs=in_specs,
        out_specs=pl.BlockSpec((tm, n), lambda i: (i, 0)),
        out_shape=jax.ShapeDtypeStruct((m, n), F32),
        compiler_params=_cparams("arbitrary"),
        name="proj_res",
    )(res, *a_list, *w_list)


def _cumsum_rows(x):
    c = x.shape[0]
    rows = lax.broadcasted_iota(jnp.int32, x.shape, 0)
    s = 1
    while s < c:
        x = x + jnp.where(rows >= s, pltpu.roll(x, s, axis=0), 0.0)
        s *= 2
    return x


def _gla_head(qh, kh, bh, vh, st_ref):
    c = qh.shape[0]
    sb = min(SUB_BLOCK, c)
    st = st_ref[...]
    b_last = bh[c - 1:c, :]
    o_inter = lax.dot_general((qh * jnp.exp(bh)).astype(BF16), st.astype(BF16), _NT,
                              preferred_element_type=F32)
    vb = vh.astype(BF16)
    rows = lax.broadcasted_iota(jnp.int32, (sb, 1), 0)
    parts = []
    for s in range(c // sb):
        r0 = s * sb
        qs, ks, bs, vs = qh[r0:r0 + sb], kh[r0:r0 + sb], bh[r0:r0 + sb], vh[r0:r0 + sb]
        acc = o_inter[r0:r0 + sb]
        if s > 0:
            ref_b = bh[r0 - 1:r0, :]
            qf = (qs * jnp.exp(bs - ref_b)).astype(BF16)
            kf = (kh[0:r0] * jnp.exp(ref_b - bh[0:r0])).astype(BF16)
            a_off = lax.dot_general(qf, kf, _NT, preferred_element_type=F32)
            acc = acc + jnp.dot(a_off.astype(BF16), vb[0:r0], preferred_element_type=F32)
        for j in range(sb):
            w = jnp.exp(jnp.minimum(bs - bs[j:j + 1], 0.0)) * qs * ks[j:j + 1]
            col = jnp.where(rows >= j, jnp.sum(w, axis=-1, keepdims=True), 0.0)
            acc = acc + col * vs[j:j + 1]
        parts.append(acc)
    kd = (kh * jnp.exp(b_last - bh)).astype(BF16)
    st_ref[...] = st * jnp.exp(b_last) + lax.dot_general(vb, kd, _TN, preferred_element_type=F32)
    return parts[0] if len(parts) == 1 else jnp.concatenate(parts, axis=0)


def _gla_kernel(*refs, mode, layer, dv, rows_in, chunk, n_chunks, has_s0):
    n_in = (6 if mode == "hgrn" else 8) + (1 if has_s0 else 0)
    ins, (o_ref, sout_ref), scr = refs[:n_in], refs[n_in:n_in + 2], refs[n_in + 2:]
    st_ref = scr[0]
    pad_refs = scr[1:]
    n_act = 4 if mode == "hgrn" else 5
    t_idx = pl.program_id(1)
    padded = rows_in < chunk

    @pl.when(t_idx == 0)
    def _():
        for h in range(N_HEADS):
            if has_s0:
                st_ref[h] = ins[-1][h].T
            else:
                st_ref[h] = jnp.zeros(st_ref.shape[1:], F32)

    if padded:
        for p_ref, a_ref in zip(pad_refs, ins[:n_act]):
            p_ref[...] = jnp.zeros(p_ref.shape, F32)
            p_ref[0:rows_in, :] = a_ref[...]
        acts = pad_refs
    else:
        acts = ins[:n_act]

    def one_chunk(c, carry):
        r0 = c * chunk if isinstance(c, int) else pl.multiple_of(c * chunk, chunk)
        ld = [a[pl.ds(r0, chunk), :] for a in acts]
        if mode == "hgrn":
            q_raw, f, v, gate = ld
            lb_ref, gn_ref = ins[4], ins[5]
            lbv = lb_ref[...]
            e = jnp.exp(lbv - jnp.max(lbv, axis=0, keepdims=True))
            lb = jnp.sum(e[0:layer + 1], axis=0, keepdims=True) / jnp.sum(e, axis=0, keepdims=True)
            forget = lb + (1.0 - lb) * _sigmoid(f)
            k = 1.0 - forget
            lg = jnp.log(forget)
            q = _silu(q_raw)
        else:
            q_raw, k, v, gate, gd = ld
            wg_ref, bg_ref, gn_ref = ins[5], ins[6], ins[7]
            z = jnp.dot(gd.astype(BF16), wg_ref[...], preferred_element_type=F32) + bg_ref[...]
            lg = (jnp.minimum(z, 0.0) - jnp.log(1.0 + jnp.exp(-jnp.abs(z)))) / GLA_GATE_TAU
            q = q_raw * (HEAD_DK ** -0.5)
        if padded:
            live = lax.broadcasted_iota(jnp.int32, (chunk, 1), 0) < rows_in
            lg = jnp.where(live, lg, 0.0)
            k = jnp.where(live, k, 0.0)
        b = _cumsum_rows(lg)
        outs = []
        for h in range(N_HEADS):
            ks = slice(h * HEAD_DK, (h + 1) * HEAD_DK)
            vs = slice(h * dv, (h + 1) * dv)
            o = _gla_head(q[:, ks], k[:, ks], b[:, ks], v[:, vs], st_ref.at[h])
            outs.append(_rms(o, gn_ref[...]) * _silu(gate[:, vs]))
        o_all = jnp.concatenate(outs, axis=1).astype(o_ref.dtype)
        if padded:
            o_ref[...] = o_all[0:rows_in]
        else:
            o_ref[pl.ds(r0, chunk), :] = o_all
        return carry

    if n_chunks == 1:
        one_chunk(0, 0)
    else:
        lax.fori_loop(0, n_chunks, one_chunk, 0)

    @pl.when(t_idx == pl.num_programs(1) - 1)
    def _():
        for h in range(N_HEADS):
            sout_ref[h] = st_ref[h].T


def _gla_call(mode, proj, params, s0, *, layer, dv, seq_len, tb, chunk):
    bsz, seq, _ = proj.shape
    assert seq == seq_len
    rows_in = min(tb, seq)
    if rows_in < chunk:
        assert seq == rows_in
        nt, n_chunks = 1, 1
    else:
        assert seq % tb == 0 and tb % chunk == 0
        nt, n_chunks = seq // tb, tb // chunk
    kw, vw = N_HEADS * HEAD_DK, N_HEADS * dv

    def act(width, col_block):
        return pl.BlockSpec((None, rows_in, width), lambda b, t: (b, t, col_block))

    def whole(a):
        return pl.BlockSpec(a.shape, lambda b, t: (0,) * a.ndim)

    if mode == "hgrn":
        act_specs = [act(kw, 0), act(kw, 1), act(vw, 2), act(vw, 3)]
        act_widths = [kw, kw, vw, vw]
    else:
        act_specs = [act(kw, 0), act(kw, 1), act(vw, kw * 2 // vw), act(vw, kw * 2 // vw + 1),
                     act(128, (2 * kw + 2 * vw) // 128)]
        act_widths = [kw, kw, vw, vw, 128]
    in_specs = act_specs + [whole(p) for p in params]
    args = [proj] * len(act_specs) + list(params)
    state_spec = pl.BlockSpec((None, N_HEADS, HEAD_DK, dv), lambda b, t: (b, 0, 0, 0))
    if s0 is not None:
        in_specs.append(state_spec)
        args.append(s0)
    scratch = [pltpu.VMEM((N_HEADS, dv, HEAD_DK), F32)]
    if rows_in < chunk:
        scratch += [pltpu.VMEM((chunk, w), F32) for w in act_widths]
    return pl.pallas_call(
        functools.partial(_gla_kernel, mode=mode, layer=layer, dv=dv, rows_in=rows_in,
                          chunk=chunk, n_chunks=n_chunks, has_s0=s0 is not None),
        grid=(bsz, nt),
        in_specs=in_specs,
        out_specs=[pl.BlockSpec((None, rows_in, vw), lambda b, t: (b, t, 0)), state_spec],
        out_shape=[jax.ShapeDtypeStruct((bsz, seq, vw), BF16),
                   jax.ShapeDtypeStruct((bsz, N_HEADS, HEAD_DK, dv), F32)],
        scratch_shapes=scratch,
        compiler_params=_cparams("arbitrary", "arbitrary"),
        name="gla_" + mode,
    )(*args)


def _s5_prep_kernel(lre_ref, lim_ref, ls_ref, lre_x_ref, lim_x_ref, ls_x_ref, bre_ref, bim_ref,
                    are_ref, aim_ref, bbre_ref, bbim_ref):
    def disc(lre, lim, ls):
        lr = jnp.minimum(lre, S5_MAX_RE)
        dt = jnp.exp(ls)
        mag = jnp.exp(lr * dt)
        a_re = mag * jnp.cos(lim * dt)
        a_im = mag * jnp.sin(lim * dt)
        den = lr * lr + lim * lim
        z_re = ((a_re - 1.0) * lr + a_im * lim) / den
        z_im = (a_im * lr - (a_re - 1.0) * lim) / den
        return a_re, a_im, z_re, z_im

    a_re, a_im, _, _ = disc(lre_ref[...], lim_ref[...], ls_ref[...])
    are_ref[...] = a_re
    aim_ref[...] = a_im
    _, _, z_re, z_im = disc(lre_x_ref[...], lim_x_ref[...], ls_x_ref[...])
    bbre_ref[...] = z_re * bre_ref[...] - z_im * bim_ref[...]
    bbim_ref[...] = z_re * bim_ref[...] + z_im * bre_ref[...]


def _s5_prep(lam_re, lam_im, log_step, b_re, b_im):
    g, n = lam_re.shape
    p = b_re.shape[-1]
    ls = jnp.broadcast_to(log_step[:, None], (g, n))
    rep = lambda a: jnp.repeat(a, p, axis=1)
    outs = pl.pallas_call(
        _s5_prep_kernel,
        out_shape=[jax.ShapeDtypeStruct((g, n), F32)] * 2 + [jax.ShapeDtypeStruct((g, n * p), F32)] * 2,
        name="s5_prep",
    )(lam_re, lam_im, ls, rep(lam_re), rep(lam_im), rep(ls),
      b_re.reshape(g, n * p), b_im.reshape(g, n * p))
    a_re, a_im, bb_re, bb_im = outs
    return a_re, a_im, bb_re.reshape(g, n, p), bb_im.reshape(g, n, p)


def _s5_kernel(*refs, bg, tc, nt, cw, t_live, has_x0):
    n_in = 9 if has_x0 else 7
    u_ref, bm_ref, cm_ref, a_ref, d_ref, wg_ref, bgl_ref = refs[:7]
    o_ref, sre_ref, sim_ref = refs[n_in:n_in + 3]
    xs_ref, st_ref = refs[n_in + 3:]
    t_idx = pl.program_id(1)
    rows = bg * tc
    half = xs_ref.shape[0] * 128 // 2
    uw = u_ref.shape[-1] // 2

    @pl.when(t_idx == 0)
    def _():
        for hf in range(2):
            if has_x0:
                st_ref[:, hf * 2 * half:hf * 2 * half + half] = refs[7][:, hf * half:(hf + 1) * half]
                st_ref[:, hf * 2 * half + half:(hf + 1) * 2 * half] = refs[8][:, hf * half:(hf + 1) * half]
            else:
                st_ref[...] = jnp.zeros(st_ref.shape, F32)

    u = u_ref[...].reshape(rows, 2 * uw)
    nsl = half // 128
    ys = []
    for hf in range(2):
        bu = jnp.dot(u[:, hf * uw:(hf + 1) * uw].astype(BF16), bm_ref[hf], preferred_element_type=F32)
        for j in range(2 * nsl):
            xs_ref[j] = bu[:, j * 128:(j + 1) * 128]
        for j0 in range(0, nsl, cw // 128):
            slabs = range(j0, j0 + cw // 128)
            base = hf * 2 * half
            ar = [jnp.broadcast_to(a_ref[0:1, base + j * 128:base + (j + 1) * 128], (bg, 128)) for j in slabs]
            ai = [jnp.broadcast_to(a_ref[0:1, base + half + j * 128:base + half + (j + 1) * 128], (bg, 128))
                  for j in slabs]
            xr = [st_ref[:, base + j * 128:base + (j + 1) * 128] for j in slabs]
            xi = [st_ref[:, base + half + j * 128:base + half + (j + 1) * 128] for j in slabs]
            for t in range(t_live):
                rsel = pl.ds(t, bg, stride=tc) if tc > 1 else pl.ds(0, bg)
                for n, j in enumerate(slabs):
                    re_slab, im_slab = xs_ref.at[j], xs_ref.at[nsl + j]
                    nr = ar[n] * xr[n] - ai[n] * xi[n] + re_slab[rsel, :]
                    ni = ar[n] * xi[n] + ai[n] * xr[n] + im_slab[rsel, :]
                    re_slab[rsel, :] = nr
                    im_slab[rsel, :] = ni
                    xr[n], xi[n] = nr, ni
            for n, j in enumerate(slabs):
                st_ref[:, base + j * 128:base + (j + 1) * 128] = xr[n]
                st_ref[:, base + half + j * 128:base + half + (j + 1) * 128] = xi[n]
        xs = jnp.concatenate([xs_ref[j] for j in range(2 * nsl)], axis=1)
        ys.append(jnp.dot(xs.astype(BF16), cm_ref[hf], preferred_element_type=F32))
    y = jnp.concatenate(ys, axis=1) + d_ref[...] * u
    y = 0.5 * y * (1.0 + jnp.tanh(math.sqrt(2.0 / math.pi) * (y + 0.044715 * (y * y * y))))
    gate = jnp.dot(y.astype(BF16), wg_ref[...], preferred_element_type=F32) + bgl_ref[...]
    o_ref[...] = (y * _sigmoid(gate)).astype(o_ref.dtype).reshape(o_ref.shape)

    @pl.when(t_idx == nt - 1)
    def _():
        for hf in range(2):
            sre_ref[:, hf * half:(hf + 1) * half] = st_ref[:, hf * 2 * half:hf * 2 * half + half]
            sim_ref[:, hf * half:(hf + 1) * half] = st_ref[:, hf * 2 * half + half:(hf + 1) * 2 * half]


def _s5_call(u_src, col_block, mats, x0, *, blk, bg, tc, cw):
    na, ns, _ = u_src.shape
    assert na == blk[0] and ns % blk[1] == 0 and blk[0] * blk[1] == bg * tc
    nt = ns // blk[1]
    bmat, cmat, a_flat, d_row, w_glu, b_glu = mats
    width = d_row.shape[1]
    nstate = a_flat.shape[1] // 2
    whole = lambda a: pl.BlockSpec(a.shape, lambda g, t: (0,) * a.ndim)
    in_specs = [pl.BlockSpec((blk[0], blk[1], width), lambda g, t: (0, t, col_block))]
    in_specs += [whole(m) for m in mats]
    args = [u_src] + list(mats)
    st_spec = pl.BlockSpec((bg, nstate), lambda g, t: (0, 0))
    if x0 is not None:
        in_specs += [st_spec, st_spec]
        args += list(x0)
    return pl.pallas_call(
        functools.partial(_s5_kernel, bg=bg, tc=tc, nt=nt, cw=cw, t_live=tc, has_x0=x0 is not None),
        grid=(1, nt),
        in_specs=in_specs,
        out_specs=[pl.BlockSpec((blk[0], blk[1], width), lambda g, t: (0, t, 0)), st_spec, st_spec],
        out_shape=[jax.ShapeDtypeStruct((na, ns, width), BF16),
                   jax.ShapeDtypeStruct((bg, nstate), F32),
                   jax.ShapeDtypeStruct((bg, nstate), F32)],
        scratch_shapes=[pltpu.VMEM((nstate // 128, bg * tc, 128), F32), pltpu.VMEM((bg, 2 * nstate), F32)],
        compiler_params=_cparams("arbitrary", "arbitrary"),
        name="s5",
    )(*args)


def _s5_matrices(a_re, a_im, bb_re, bb_im, c_re, c_im, d, w_glu, b_glu):
    g, n, p = bb_re.shape
    gh = g // 2
    eye = jnp.eye(gh, dtype=F32)

    def block_diag(t):
        return (eye[:, None, :, None] * t[:, :, None, :]).reshape(gh * t.shape[1], gh * t.shape[2])

    def in_mat(bb):
        return block_diag(bb.transpose(0, 2, 1))

    def out_mat(cc):
        return block_diag(cc.transpose(0, 2, 1))

    bmat = jnp.stack([jnp.concatenate([in_mat(bb_re[h * gh:(h + 1) * gh]),
                                       in_mat(bb_im[h * gh:(h + 1) * gh])], axis=1)
                      for h in range(2)]).astype(BF16)
    cmat = jnp.stack([jnp.concatenate([out_mat(c_re[h * gh:(h + 1) * gh]),
                                       out_mat(-c_im[h * gh:(h + 1) * gh])], axis=0)
                      for h in range(2)]).astype(BF16)
    a_flat = jnp.concatenate([jnp.concatenate([a_re[h * gh:(h + 1) * gh].reshape(1, gh * n),
                                               a_im[h * gh:(h + 1) * gh].reshape(1, gh * n)], axis=1)
                              for h in range(2)], axis=1)
    return (bmat, cmat, a_flat, d.reshape(1, g * p), w_glu.astype(BF16), b_glu.reshape(1, -1))


def _attn_kernel(q_ref, k_ref, v_ref, o_ref, *scr, rows_in, rows):
    if rows_in != rows:
        pad_ref = scr[0]
        pad_ref[...] = jnp.zeros(pad_ref.shape, F32)
        pad_ref[0:rows_in, :] = q_ref[...]
        q = pad_ref[...]
    else:
        q = q_ref[...]
    hd = q.shape[1] // N_HEADS
    outs = []
    for h in range(N_HEADS):
        cols = slice(h * hd, (h + 1) * hd)
        s = lax.dot_general(q[:, cols].astype(BF16), k_ref[:, cols].astype(BF16), _NT,
                            preferred_element_type=F32) * (hd ** -0.5)
        p = jnp.exp(s - jnp.max(s, axis=-1, keepdims=True))
        p = p / jnp.sum(p, axis=-1, keepdims=True)
        outs.append(jnp.dot(p.astype(BF16), v_ref[:, cols].astype(BF16), preferred_element_type=F32))
    o = jnp.concatenate(outs, axis=1).astype(o_ref.dtype)
    o_ref[...] = o[0:rows_in] if rows_in != rows else o


def _attn_call(q, mem_k, mem_v, layer, *, tq):
    bsz, seq, d = q.shape
    n_mem = mem_k.shape[2]
    rows_in = min(tq, seq)
    rows = max(rows_in, 16)
    kv_spec = pl.BlockSpec((None, None, n_mem, d), lambda b, t: (layer, b, 0, 0))
    return pl.pallas_call(
        functools.partial(_attn_kernel, rows_in=rows_in, rows=rows),
        grid=(bsz, seq // rows_in),
        in_specs=[pl.BlockSpec((None, rows_in, d), lambda b, t: (b, t, 0)), kv_spec, kv_spec],
        out_specs=pl.BlockSpec((None, rows_in, d), lambda b, t: (b, t, 0)),
        out_shape=jax.ShapeDtypeStruct((bsz, seq, d), BF16),
        scratch_shapes=[pltpu.VMEM((rows, d), F32)] if rows != rows_in else [],
        compiler_params=_cparams("arbitrary", "arbitrary"),
        name="mem_attn",
    )(q, mem_k, mem_v)


FFN_COLS = 256


def _ffn_kernel(*refs, tm, ts, hs, f_dim, has_hist, has_final):
    n_in = 6 + int(has_hist) + int(has_final)
    x_ref, g_ref, wup_ref, cw_ref, cb_ref, wdn_ref = refs[:6]
    y_ref, state_ref = refs[n_in:n_in + 2]
    e_refs = refs[n_in + 2:n_in + 4]
    hist_ref = refs[n_in + 4]
    t_idx = pl.program_id(1)

    @pl.when(t_idx == 0)
    def _():
        hist_ref[...] = refs[6][...] if has_hist else jnp.zeros(hist_ref.shape, F32)

    x = x_ref[...]
    h = _rms(x, g_ref[...]).astype(BF16)
    acc = x
    for c in range(f_dim // FFN_COLS):
        conv = []
        for part, e_ref in enumerate(e_refs):
            cols = slice(part * f_dim + c * FFN_COLS, part * f_dim + (c + 1) * FFN_COLS)
            u = jnp.dot(h, wup_ref[:, cols], preferred_element_type=F32)
            e_ref[hs - 2 * ts:hs, :] = hist_ref[:, cols]
            e_ref[hs:hs + tm, :] = u
            conv.append(cb_ref[:, cols]
                        + cw_ref[0:1, cols] * e_ref[hs - 2 * ts:hs - 2 * ts + tm, :]
                        + cw_ref[1:2, cols] * e_ref[hs - ts:hs - ts + tm, :]
                        + cw_ref[2:3, cols] * u)
            hist_ref[:, cols] = e_ref[hs + tm - 2 * ts:hs + tm, :]
        y = (_silu(conv[0]) * conv[1]).astype(BF16)
        acc = acc + jnp.dot(y, wdn_ref[c * FFN_COLS:(c + 1) * FFN_COLS, :], preferred_element_type=F32)
    if has_final:
        acc = _rms(acc, refs[n_in - 1][...])
    y_ref[...] = acc

    @pl.when(t_idx == pl.num_programs(1) - 1)
    def _():
        state_ref[...] = hist_ref[...]


def _ffn_call(x, g, w_up, conv_w, conv_b, w_down, hist0, g_final, *, tm, ts):
    ngrp, rows, d = x.shape
    f2 = w_up.shape[1]
    f_dim = f2 // 2
    assert rows % tm == 0 and tm >= 2 * ts and f_dim % FFN_COLS == 0
    hs = max(8, 2 * ts)
    whole = lambda a: pl.BlockSpec(a.shape, lambda s, t: (0,) * a.ndim)
    args = [x, g, w_up, conv_w, conv_b, w_down]
    in_specs = [pl.BlockSpec((None, tm, d), lambda s, t: (s, t, 0))] + [whole(a) for a in args[1:]]
    st_spec = pl.BlockSpec((None, 2 * ts, f2), lambda s, t: (s, 0, 0))
    if hist0 is not None:
        in_specs.append(st_spec)
        args.append(hist0)
    if g_final is not None:
        in_specs.append(whole(g_final))
        args.append(g_final)
    return pl.pallas_call(
        functools.partial(_ffn_kernel, tm=tm, ts=ts, hs=hs, f_dim=f_dim,
                          has_hist=hist0 is not None, has_final=g_final is not None),
        grid=(ngrp, rows // tm),
        in_specs=in_specs,
        out_specs=[pl.BlockSpec((None, tm, d), lambda s, t: (s, t, 0)), st_spec],
        out_shape=[jax.ShapeDtypeStruct((ngrp, rows, d), F32),
                   jax.ShapeDtypeStruct((ngrp, 2 * ts, f2), F32)],
        scratch_shapes=[pltpu.VMEM((hs + tm, FFN_COLS), F32), pltpu.VMEM((hs + tm, FFN_COLS), F32),
                        pltpu.VMEM((2 * ts, f2), F32)],
        compiler_params=_cparams("arbitrary", "arbitrary"),
        name="conv_ffn",
    )(*args)


def _trunk(x, mem_k, mem_v, states, p, *, prompt):
    bsz, seq, d = x.shape
    depth = p["norm_mix"].shape[0]
    m = bsz * seq
    x2 = x.reshape(m, d)
    new = {"hgrn": [], "s5_re": [], "s5_im": [], "gla": [], "conv": []}
    if prompt:
        gla_tiles = dict(tb=256, chunk=64)
        attn_tq = 512
    else:
        gla_tiles = dict(tb=seq, chunk=16)
        attn_tq = seq
    for l in range(depth):
        g_mix = p["norm_mix"][l].reshape(1, 1, d)
        if l % 2 == 0:
            e = l // 2
            w_in = p["w_in_ab"][e]
            proj = _norm_proj(x2, g_mix, w_in[None], [w_in.shape[1]], [F32])[0][0]
            kw = N_HEADS * HEAD_DK
            o_a, s_a = _gla_call(
                "hgrn", proj.reshape(bsz, seq, -1),
                [p["hgrn_lb"], p["hgrn_gnorm"][e].reshape(1, -1)],
                None if states is None else states["hgrn"][e],
                layer=l, dv=kw // N_HEADS, seq_len=seq, **gla_tiles)
            mats = p["s5_mats"][e]
            if prompt:
                o_b, sr, si = _s5_call(proj.reshape(bsz, seq, -1), 4, mats, None,
                                       blk=(bsz, 64), bg=bsz, tc=64, cw=256)
            else:
                x0 = (states["s5_re"][e].reshape(bsz, -1), states["s5_im"][e].reshape(bsz, -1))
                o_b, sr, si = _s5_call(proj.reshape(1, m, -1), 4, mats, x0,
                                       blk=(1, m), bg=bsz, tc=seq, cw=128)
            w_out = p["w_out_ab"][e]
            x2 = _proj_res(x2, [o_a.reshape(m, -1), o_b.reshape(m, -1)], [w_out[:kw], w_out[kw:]])
            new["hgrn"].append(s_a)
            new["s5_re"].append(sr.reshape(bsz, -1, S5_STATE))
            new["s5_im"].append(si.reshape(bsz, -1, S5_STATE))
        else:
            o_idx = l // 2
            w_in = p["w_in_c"][o_idx]
            proj = _norm_proj(x2, g_mix, w_in[None], [w_in.shape[1]], [F32])[0][0]
            o_c, s_c = _gla_call(
                "gla", proj.reshape(bsz, seq, -1),
                [p["gla_w_gate"][o_idx], p["gla_b_gate"][o_idx].reshape(1, -1),
                 p["gla_gnorm"][o_idx].reshape(1, -1)],
                None if states is None else states["gla"][o_idx],
                layer=l, dv=d // N_HEADS, seq_len=seq, **gla_tiles)
            x2 = _proj_res(x2, [o_c.reshape(m, -1)], [p["w_out_c"][o_idx]])
            new["gla"].append(s_c)
        q = _norm_proj(x2, p["norm_cross"][l].reshape(1, 1, d), p["xa_w_q"][l][None], [d], [F32])[0][0]
        o_x = _attn_call(q.reshape(bsz, seq, d), mem_k, mem_v, l, tq=attn_tq)
        x2 = _proj_res(x2, [o_x.reshape(m, d)], [p["xa_w_o"][l]])
        g_final = p["norm_final"].reshape(1, d) if l == depth - 1 else None
        ffn_w = (p["norm_ffn"][l].reshape(1, d), p["ffn_w_up"][l], p["ffn_conv_w"][l],
                 p["ffn_conv_b"][l].reshape(1, -1), p["ffn_w_down"][l])
        if prompt:
            y, cst = _ffn_call(x2.reshape(bsz, seq, d), *ffn_w, None, g_final, tm=256, ts=1)
            x2 = y.reshape(m, d)
        else:
            xt = x2.reshape(bsz, seq, d).transpose(1, 0, 2).reshape(1, m, d)
            hist0 = states["conv"][l].transpose(1, 0, 2).reshape(1, 2 * bsz, -1)
            y, cst = _ffn_call(xt, *ffn_w, hist0, g_final, tm=m, ts=bsz)
            x2 = y.reshape(seq, bsz, d).transpose(1, 0, 2).reshape(m, d)
            cst = cst.reshape(2, bsz, -1).transpose(1, 0, 2)
        new["conv"].append(cst)
    return x2.reshape(bsz, seq, d), new


def kernel(x_prompt, x_sample, mem_prompt, cache_mem_k, cache_mem_v, state_hgrn, state_s5_re, state_s5_im, state_gla, state_ffn_conv, norm_mix, norm_cross, norm_mem, norm_ffn, norm_final, w_in_ab, hgrn_lb, hgrn_gnorm, s5_lam_re, s5_lam_im, s5_log_step, s5_b_re, s5_b_im, s5_c_re, s5_c_im, s5_d, s5_w_glu, s5_b_glu, w_out_ab, w_in_c, gla_w_gate_up, gla_b_gate, gla_gnorm, w_out_c, xa_w_q, xa_w_kv, xa_w_o, ffn_w_up, ffn_conv_w, ffn_conv_b, ffn_w_down):
    depth, d = norm_mix.shape
    n_mem = mem_prompt.shape[1]
    bsz = x_prompt.shape[0]
    dec_bsz = x_sample.shape[0]

    gla_cols = w_in_c.shape[2]
    gate_rank = gla_w_gate_up.shape[1]
    pad_c = (-gla_cols) % 128
    w_in_c_p = jnp.pad(w_in_c, ((0, 0), (0, 0), (0, pad_c))).astype(BF16)
    gla_w_gate = jnp.pad(gla_w_gate_up, ((0, 0), (0, 128 - gate_rank), (0, 0))).astype(BF16)

    s5_mats = []
    for e in range(s5_lam_re.shape[0]):
        a_re, a_im, bb_re, bb_im = _s5_prep(s5_lam_re[e], s5_lam_im[e], s5_log_step[e],
                                            s5_b_re[e], s5_b_im[e])
        s5_mats.append(_s5_matrices(a_re, a_im, bb_re, bb_im, s5_c_re[e], s5_c_im[e], s5_d[e],
                                    s5_w_glu[e], s5_b_glu[e]))

    p = dict(norm_mix=norm_mix, norm_cross=norm_cross, norm_ffn=norm_ffn, norm_final=norm_final,
             w_in_ab=w_in_ab.astype(BF16), hgrn_lb=hgrn_lb, hgrn_gnorm=hgrn_gnorm, s5_mats=s5_mats,
             w_out_ab=w_out_ab.astype(BF16), w_in_c=w_in_c_p, gla_w_gate=gla_w_gate,
             gla_b_gate=gla_b_gate, gla_gnorm=gla_gnorm, w_out_c=w_out_c.astype(BF16),
             xa_w_q=xa_w_q.astype(BF16), xa_w_o=xa_w_o.astype(BF16),
             ffn_w_up=ffn_w_up.astype(BF16), ffn_conv_w=ffn_conv_w, ffn_conv_b=ffn_conv_b,
             ffn_w_down=ffn_w_down.astype(BF16))

    mem_k_p, mem_v_p = _norm_proj(mem_prompt.reshape(bsz * n_mem, d), norm_mem.reshape(depth, 1, d),
                                  xa_w_kv.astype(BF16), [d, d], [F32, F32])
    mem_k_p = mem_k_p.reshape(depth, bsz, n_mem, d)
    mem_v_p = mem_v_p.reshape(depth, bsz, n_mem, d)
    y_prompt, st_p = _trunk(x_prompt, mem_k_p, mem_v_p, None, p, prompt=True)

    states = dict(hgrn=state_hgrn, s5_re=state_s5_re, s5_im=state_s5_im, gla=state_gla,
                  conv=state_ffn_conv)
    y_sample, st_s = _trunk(x_sample, cache_mem_k.reshape(depth, dec_bsz, n_mem, d),
                            cache_mem_v.reshape(depth, dec_bsz, n_mem, d), states, p, prompt=False)

    hd = d // N_HEADS
    kv_shape = (depth, bsz, n_mem, N_HEADS, hd)
    stack = lambda xs: jnp.stack(xs)
    return (y_prompt, y_sample,
            stack(st_p["hgrn"]), stack(st_p["s5_re"]), stack(st_p["s5_im"]), stack(st_p["gla"]),
            mem_k_p.reshape(kv_shape), mem_v_p.reshape(kv_shape), stack(st_p["conv"]),
            stack(st_s["hgrn"]), stack(st_s["s5_re"]), stack(st_s["s5_im"]), stack(st_s["gla"]),
            stack(st_s["conv"]))
```

```python
import functools
import math

import jax
import jax.numpy as jnp
from jax import lax
from jax.experimental import pallas as pl
from jax.experimental.pallas import tpu as pltpu

F32 = jnp.float32
BF16 = jnp.bfloat16

EPS = 1e-6
S5_MAX_RE = -1e-4
GLA_GATE_TAU = 16.0
N_HEADS = 4
HEAD_DK = 128
S5_GROUP = 16
S5_STATE = 64
SUB_BLOCK = 16
GLA_SAFE_DECAY = 64.0
VMEM_LIMIT = 56 * 1024 * 1024

_NT = (((1,), (1,)), ((), ()))
_TN = (((0,), (0,)), ((), ()))


def _cparams(*sem):
    return pltpu.CompilerParams(dimension_semantics=sem, vmem_limit_bytes=VMEM_LIMIT)


def _rms(x, g):
    return x * lax.rsqrt(jnp.mean(x * x, axis=-1, keepdims=True) + EPS) * g


def _sigmoid(x):
    return 1.0 / (1.0 + jnp.exp(-x))


def _silu(x):
    return x * _sigmoid(x)


def _row_tile(rows, want):
    t = min(rows, want)
    assert rows % t == 0, (rows, t)
    return t


def _norm_proj_kernel(x_ref, g_ref, w_ref, *o_refs, widths, has_norm):
    x = x_ref[...]
    h = (_rms(x, g_ref[...]) if has_norm else x).astype(BF16)
    col = 0
    for o_ref, wd in zip(o_refs, widths):
        for c0 in range(0, wd, 512):
            cw = min(512, wd - c0)
            o_ref[:, c0:c0 + cw] = jnp.dot(
                h, w_ref[:, col + c0:col + c0 + cw],
                preferred_element_type=F32).astype(o_ref.dtype)
        col += wd


def _norm_proj(x, g, w, widths, out_dtypes, *, has_norm=True, tm=512):
    m, k = x.shape
    nl, _, n = w.shape
    assert n == sum(widths)
    tm = _row_tile(m, tm)
    return pl.pallas_call(
        functools.partial(_norm_proj_kernel, widths=tuple(widths), has_norm=has_norm),
        grid=(nl, m // tm),
        in_specs=[
            pl.BlockSpec((tm, k), lambda l, i: (i, 0)),
            pl.BlockSpec((None, 1, k), lambda l, i: (l, 0, 0)),
            pl.BlockSpec((None, k, n), lambda l, i: (l, 0, 0)),
        ],
        out_specs=[pl.BlockSpec((None, tm, wd), lambda l, i: (l, i, 0)) for wd in widths],
        out_shape=[jax.ShapeDtypeStruct((nl, m, wd), dt) for wd, dt in zip(widths, out_dtypes)],
        compiler_params=_cparams("arbitrary", "arbitrary"),
        name="norm_proj",
    )(x, g, w)


def _mem_kv_kernel(x_ref, g_ref, w_ref, k_ref, v_ref):
    h = _rms(x_ref[...], g_ref[...]).astype(BF16)
    d = x_ref.shape[-1]
    hd = d // N_HEADS
    for o_ref, base in ((k_ref, 0), (v_ref, d)):
        for hh in range(N_HEADS):
            o_ref[:, hh, :] = jnp.dot(h, w_ref[:, base + hh * hd:base + (hh + 1) * hd],
                                      preferred_element_type=F32)


def _mem_kv(mem, g, w):
    bsz, n_mem, d = mem.shape
    depth = w.shape[0]
    out_spec = pl.BlockSpec((None, None, n_mem, N_HEADS, d // N_HEADS), lambda l, b: (l, b, 0, 0, 0))
    out_shape = jax.ShapeDtypeStruct((depth, bsz, n_mem, N_HEADS, d // N_HEADS), F32)
    return pl.pallas_call(
        _mem_kv_kernel,
        grid=(depth, bsz),
        in_specs=[pl.BlockSpec((None, n_mem, d), lambda l, b: (b, 0, 0)),
                  pl.BlockSpec((None, 1, d), lambda l, b: (l, 0, 0)),
                  pl.BlockSpec((None, d, 2 * d), lambda l, b: (l, 0, 0))],
        out_specs=[out_spec, out_spec],
        out_shape=[out_shape, out_shape],
        compiler_params=_cparams("arbitrary", "arbitrary"),
        name="mem_kv",
    )(mem, g, w)


def _proj_res_kernel(res_ref, *refs, n_in):
    acc = res_ref[...]
    for a_ref, w_ref in zip(refs[:n_in], refs[n_in:2 * n_in]):
        acc = acc + jnp.dot(a_ref[...].astype(BF16), w_ref[...], preferred_element_type=F32)
    refs[2 * n_in][...] = acc


def _proj_res(res, a_list, w_list, *, tm=512):
    m, n = res.shape
    tm = _row_tile(m, tm)
    n_in = len(a_list)
    in_specs = [pl.BlockSpec((tm, n), lambda i: (i, 0))]
    in_specs += [pl.BlockSpec((tm, a.shape[1]), lambda i: (i, 0)) for a in a_list]
    in_specs += [pl.BlockSpec(w.shape, lambda i: (0, 0)) for w in w_list]
    return pl.pallas_call(
        functools.partial(_proj_res_kernel, n_in=n_in),
        grid=(m // tm,),
        in_specs=in_specs,
        out_specs=pl.BlockSpec((tm, n), lambda i: (i, 0)),
        out_shape=jax.ShapeDtypeStruct((m, n), F32),
        compiler_params=_cparams("arbitrary"),
        name="proj_res",
    )(res, *a_list, *w_list)


def _cumsum_rows(x):
    c = x.shape[0]
    rows = lax.broadcasted_iota(jnp.int32, x.shape, 0)
    s = 1
    while s < c:
        x = x + jnp.where(rows >= s, pltpu.roll(x, s, axis=0), 0.0)
        s *= 2
    return x


def _gla_head(qh, kh, bh, vh, st_ref, small_decay):
    c = qh.shape[0]
    sb = min(SUB_BLOCK, c)
    vb = vh.astype(BF16)
    q_in = (qh * jnp.exp(bh)).astype(BF16)
    o_inter = lax.dot_general(q_in, st_ref[...].astype(BF16), _NT, preferred_element_type=F32)
    if small_decay:
        kf = (kh * jnp.exp(-bh)).astype(BF16)
        a = lax.dot_general(q_in, kf, _NT, preferred_element_type=F32)
        causal = (lax.broadcasted_iota(jnp.int32, (c, c), 0) >= lax.broadcasted_iota(jnp.int32, (c, c), 1))
        return o_inter + jnp.dot(jnp.where(causal, a, 0.0).astype(BF16), vb, preferred_element_type=F32)
    rows = lax.broadcasted_iota(jnp.int32, (sb, 1), 0)
    parts = []
    for s in range(c // sb):
        r0 = s * sb
        qs, ks, bs, vs = qh[r0:r0 + sb], kh[r0:r0 + sb], bh[r0:r0 + sb], vh[r0:r0 + sb]
        acc = o_inter[r0:r0 + sb]
        if s > 0:
            ref_b = bh[r0 - 1:r0, :]
            qf = (qs * jnp.exp(bs - ref_b)).astype(BF16)
            kf = (kh[0:r0] * jnp.exp(ref_b - bh[0:r0])).astype(BF16)
            a_off = lax.dot_general(qf, kf, _NT, preferred_element_type=F32)
            acc = acc + jnp.dot(a_off.astype(BF16), vb[0:r0], preferred_element_type=F32)
        for j in range(sb):
            w = jnp.exp(jnp.minimum(bs - bs[j:j + 1], 0.0)) * qs * ks[j:j + 1]
            col = jnp.where(rows >= j, jnp.sum(w, axis=-1, keepdims=True), 0.0)
            acc = acc + col * vs[j:j + 1]
        parts.append(acc)
    return parts[0] if len(parts) == 1 else jnp.concatenate(parts, axis=0)


def _gla_state_update(kh, bh, vh, st_ref):
    c = kh.shape[0]
    b_last = bh[c - 1:c, :]
    kd = (kh * jnp.exp(b_last - bh)).astype(BF16)
    st_ref[...] = st_ref[...] * jnp.exp(b_last) + lax.dot_general(vh.astype(BF16), kd, _TN,
                                                                   preferred_element_type=F32)


def _gla_kernel(*refs, mode, layer, dv, rows_in, chunk, n_chunks, has_s0):
    n_in = (6 if mode == "hgrn" else 8) + (1 if has_s0 else 0)
    ins, (o_ref, sout_ref), scr = refs[:n_in], refs[n_in:n_in + 2], refs[n_in + 2:]
    st_ref = scr[0]
    pad_refs = scr[1:]
    n_act = 4 if mode == "hgrn" else 5
    t_idx = pl.program_id(1)
    padded = rows_in < chunk

    @pl.when(t_idx == 0)
    def _():
        for h in range(N_HEADS):
            if has_s0:
                st_ref[h] = ins[-1][h].T
            else:
                st_ref[h] = jnp.zeros(st_ref.shape[1:], F32)

    if padded:
        for p_ref, a_ref in zip(pad_refs, ins[:n_act]):
            p_ref[...] = jnp.zeros(p_ref.shape, F32)
            p_ref[0:rows_in, :] = a_ref[...]
        acts = pad_refs
    else:
        acts = ins[:n_act]

    def one_chunk(c, carry):
        r0 = c * chunk if isinstance(c, int) else pl.multiple_of(c * chunk, chunk)
        ld = [a[pl.ds(r0, chunk), :] for a in acts]
        if mode == "hgrn":
            q_raw, f, v, gate = ld
            lb_ref, gn_ref = ins[4], ins[5]
            lbv = lb_ref[...]
            e = jnp.exp(lbv - jnp.max(lbv, axis=0, keepdims=True))
            lb = jnp.sum(e[0:layer + 1], axis=0, keepdims=True) / jnp.sum(e, axis=0, keepdims=True)
            forget = lb + (1.0 - lb) * _sigmoid(f)
            k = 1.0 - forget
            lg = jnp.log(forget)
            q = _silu(q_raw)
        else:
            q_raw, k, v, gate, gd = ld
            wg_ref, bg_ref, gn_ref = ins[5], ins[6], ins[7]
            z = jnp.dot(gd.astype(BF16), wg_ref[...], preferred_element_type=F32) + bg_ref[...]
            lg = (jnp.minimum(z, 0.0) - jnp.log(1.0 + jnp.exp(-jnp.abs(z)))) / GLA_GATE_TAU
            q = q_raw * (HEAD_DK ** -0.5)
        if padded:
            live = lax.broadcasted_iota(jnp.int32, (chunk, 1), 0) < rows_in
            lg = jnp.where(live, lg, 0.0)
            k = jnp.where(live, k, 0.0)
        b = _cumsum_rows(lg)

        def heads(small_decay):
            return jnp.concatenate(
                [_gla_head(q[:, h * HEAD_DK:(h + 1) * HEAD_DK], k[:, h * HEAD_DK:(h + 1) * HEAD_DK],
                           b[:, h * HEAD_DK:(h + 1) * HEAD_DK], v[:, h * dv:(h + 1) * dv],
                           st_ref.at[h], small_decay) for h in range(N_HEADS)], axis=1)

        o_raw = lax.cond(jnp.min(b[chunk - 1:chunk, :]) >= -GLA_SAFE_DECAY,
                         functools.partial(heads, True), functools.partial(heads, False))
        outs = []
        for h in range(N_HEADS):
            ks = slice(h * HEAD_DK, (h + 1) * HEAD_DK)
            vs = slice(h * dv, (h + 1) * dv)
            _gla_state_update(k[:, ks], b[:, ks], v[:, vs], st_ref.at[h])
            outs.append(_rms(o_raw[:, vs], gn_ref[...]) * _silu(gate[:, vs]))
        o_all = jnp.concatenate(outs, axis=1).astype(o_ref.dtype)
        if padded:
            o_ref[...] = o_all[0:rows_in]
        else:
            o_ref[pl.ds(r0, chunk), :] = o_all
        return carry

    if n_chunks == 1:
        one_chunk(0, 0)
    else:
        lax.fori_loop(0, n_chunks, one_chunk, 0)

    @pl.when(t_idx == pl.num_programs(1) - 1)
    def _():
        for h in range(N_HEADS):
            sout_ref[h] = st_ref[h].T


def _gla_call(mode, proj, params, s0, *, layer, dv, seq_len, tb, chunk):
    bsz, seq, _ = proj.shape
    assert seq == seq_len
    rows_in = min(tb, seq)
    if rows_in < chunk:
        assert seq == rows_in
        nt, n_chunks = 1, 1
    else:
        assert seq % tb == 0 and tb % chunk == 0
        nt, n_chunks = seq // tb, tb // chunk
    kw, vw = N_HEADS * HEAD_DK, N_HEADS * dv

    def act(width, col_block):
        return pl.BlockSpec((None, rows_in, width), lambda b, t: (b, t, col_block))

    def whole(a):
        return pl.BlockSpec(a.shape, lambda b, t: (0,) * a.ndim)

    if mode == "hgrn":
        act_specs = [act(kw, 0), act(kw, 1), act(vw, 2), act(vw, 3)]
        act_widths = [kw, kw, vw, vw]
    else:
        act_specs = [act(kw, 0), act(kw, 1), act(vw, kw * 2 // vw), act(vw, kw * 2 // vw + 1),
                     act(128, (2 * kw + 2 * vw) // 128)]
        act_widths = [kw, kw, vw, vw, 128]
    in_specs = act_specs + [whole(p) for p in params]
    args = [proj] * len(act_specs) + list(params)
    state_spec = pl.BlockSpec((None, N_HEADS, HEAD_DK, dv), lambda b, t: (b, 0, 0, 0))
    if s0 is not None:
        in_specs.append(state_spec)
        args.append(s0)
    scratch = [pltpu.VMEM((N_HEADS, dv, HEAD_DK), F32)]
    if rows_in < chunk:
        scratch += [pltpu.VMEM((chunk, w), F32) for w in act_widths]
    return pl.pallas_call(
        functools.partial(_gla_kernel, mode=mode, layer=layer, dv=dv, rows_in=rows_in,
                          chunk=chunk, n_chunks=n_chunks, has_s0=s0 is not None),
        grid=(bsz, nt),
        in_specs=in_specs,
        out_specs=[pl.BlockSpec((None, rows_in, vw), lambda b, t: (b, t, 0)), state_spec],
        out_shape=[jax.ShapeDtypeStruct((bsz, seq, vw), BF16),
                   jax.ShapeDtypeStruct((bsz, N_HEADS, HEAD_DK, dv), F32)],
        scratch_shapes=scratch,
        compiler_params=_cparams("arbitrary", "arbitrary"),
        name="gla_" + mode,
    )(*args)


def _s5_prep_kernel(lre_ref, lim_ref, ls_ref, lre_x_ref, lim_x_ref, ls_x_ref, bre_ref, bim_ref,
                    are_ref, aim_ref, bbre_ref, bbim_ref):
    def disc(lre, lim, ls):
        lr = jnp.minimum(lre, S5_MAX_RE)
        dt = jnp.exp(ls)
        mag = jnp.exp(lr * dt)
        a_re = mag * jnp.cos(lim * dt)
        a_im = mag * jnp.sin(lim * dt)
        den = lr * lr + lim * lim
        z_re = ((a_re - 1.0) * lr + a_im * lim) / den
        z_im = (a_im * lr - (a_re - 1.0) * lim) / den
        return a_re, a_im, z_re, z_im

    a_re, a_im, _, _ = disc(lre_ref[...], lim_ref[...], ls_ref[...])
    are_ref[...] = a_re
    aim_ref[...] = a_im
    _, _, z_re, z_im = disc(lre_x_ref[...], lim_x_ref[...], ls_x_ref[...])
    bbre_ref[...] = z_re * bre_ref[...] - z_im * bim_ref[...]
    bbim_ref[...] = z_re * bim_ref[...] + z_im * bre_ref[...]


def _s5_prep(lam_re, lam_im, log_step, b_re, b_im):
    g, n = lam_re.shape
    p = b_re.shape[-1]
    ls = jnp.broadcast_to(log_step[:, None], (g, n))
    rep = lambda a: jnp.repeat(a, p, axis=1)
    outs = pl.pallas_call(
        _s5_prep_kernel,
        out_shape=[jax.ShapeDtypeStruct((g, n), F32)] * 2 + [jax.ShapeDtypeStruct((g, n * p), F32)] * 2,
        name="s5_prep",
    )(lam_re, lam_im, ls, rep(lam_re), rep(lam_im), rep(ls),
      b_re.reshape(g, n * p), b_im.reshape(g, n * p))
    a_re, a_im, bb_re, bb_im = outs
    return a_re, a_im, bb_re.reshape(g, n, p), bb_im.reshape(g, n, p)


def _s5_kernel(*refs, bg, tc, nt, cw, has_x0):
    n_in = 9 if has_x0 else 7
    u_ref, bm_ref, cm_ref, a_ref, d_ref, wg_ref, bgl_ref = refs[:7]
    o_ref, sre_ref, sim_ref = refs[n_in:n_in + 3]
    utm_ref, xs_ref, st_ref = refs[n_in + 3:]
    t_idx = pl.program_id(1)
    half = xs_ref.shape[1] // 2
    uw = u_ref.shape[-1] // 2

    @pl.when(t_idx == 0)
    def _():
        for hf in range(2):
            if has_x0:
                st_ref[:, hf * 2 * half:hf * 2 * half + half] = refs[7][:, hf * half:(hf + 1) * half]
                st_ref[:, hf * 2 * half + half:(hf + 1) * 2 * half] = refs[8][:, hf * half:(hf + 1) * half]
            else:
                st_ref[...] = jnp.zeros(st_ref.shape, F32)

    for t in range(tc):
        utm_ref[t * bg:(t + 1) * bg, :] = u_ref[:, t, :]
    u = utm_ref[...]
    ys = []
    for hf in range(2):
        xs_ref[...] = jnp.dot(u[:, hf * uw:(hf + 1) * uw].astype(BF16), bm_ref[hf],
                              preferred_element_type=F32)
        base = hf * 2 * half
        for c0 in range(0, half, cw):
            ar = a_ref[0:1, base + c0:base + c0 + cw]
            ai = a_ref[0:1, base + half + c0:base + half + c0 + cw]
            xr = st_ref[:, base + c0:base + c0 + cw]
            xi = st_ref[:, base + half + c0:base + half + c0 + cw]
            for t in range(tc):
                rows = slice(t * bg, (t + 1) * bg)
                nr = ar * xr - ai * xi + xs_ref[rows, c0:c0 + cw]
                ni = ar * xi + ai * xr + xs_ref[rows, half + c0:half + c0 + cw]
                xs_ref[rows, c0:c0 + cw] = nr
                xs_ref[rows, half + c0:half + c0 + cw] = ni
                xr, xi = nr, ni
            st_ref[:, base + c0:base + c0 + cw] = xr
            st_ref[:, base + half + c0:base + half + c0 + cw] = xi
        ys.append(jnp.dot(xs_ref[...].astype(BF16), cm_ref[hf], preferred_element_type=F32))
    y = jnp.concatenate(ys, axis=1) + d_ref[...] * u
    y = 0.5 * y * (1.0 + jnp.tanh(math.sqrt(2.0 / math.pi) * (y + 0.044715 * (y * y * y))))
    gate = jnp.dot(y.astype(BF16), wg_ref[...], preferred_element_type=F32) + bgl_ref[...]
    utm_ref[...] = y * _sigmoid(gate)
    for t in range(tc):
        o_ref[:, t, :] = utm_ref[t * bg:(t + 1) * bg, :]

    @pl.when(t_idx == nt - 1)
    def _():
        for hf in range(2):
            sre_ref[:, hf * half:(hf + 1) * half] = st_ref[:, hf * 2 * half:hf * 2 * half + half]
            sim_ref[:, hf * half:(hf + 1) * half] = st_ref[:, hf * 2 * half + half:(hf + 1) * 2 * half]


def _s5_call(u_src, col_block, mats, x0, *, tc, cw):
    bg, seq, _ = u_src.shape
    assert seq % tc == 0
    nt = seq // tc
    bmat, cmat, a_flat, d_row, w_glu, b_glu = mats
    width = d_row.shape[1]
    nstate = a_flat.shape[1] // 2
    whole = lambda a: pl.BlockSpec(a.shape, lambda g, t: (0,) * a.ndim)
    in_specs = [pl.BlockSpec((bg, tc, width), lambda g, t: (0, t, col_block))]
    in_specs += [whole(m) for m in mats]
    args = [u_src] + list(mats)
    st_spec = pl.BlockSpec((bg, nstate), lambda g, t: (0, 0))
    if x0 is not None:
        in_specs += [st_spec, st_spec]
        args += list(x0)
    return pl.pallas_call(
        functools.partial(_s5_kernel, bg=bg, tc=tc, nt=nt, cw=cw, has_x0=x0 is not None),
        grid=(1, nt),
        in_specs=in_specs,
        out_specs=[pl.BlockSpec((bg, tc, width), lambda g, t: (0, t, 0)), st_spec, st_spec],
        out_shape=[jax.ShapeDtypeStruct((bg, seq, width), F32),
                   jax.ShapeDtypeStruct((bg, nstate), F32),
                   jax.ShapeDtypeStruct((bg, nstate), F32)],
        scratch_shapes=[pltpu.VMEM((bg * tc, width), F32), pltpu.VMEM((bg * tc, nstate), F32),
                        pltpu.VMEM((bg, 2 * nstate), F32)],
        compiler_params=_cparams("arbitrary", "arbitrary"),
        name="s5",
    )(*args)


def _s5_matrices(a_re, a_im, bb_re, bb_im, c_re, c_im, d, w_glu, b_glu):
    g, n, p = bb_re.shape
    gh = g // 2
    eye = jnp.eye(gh, dtype=F32)

    def block_diag(t):
        return (eye[:, None, :, None] * t[:, :, None, :]).reshape(gh * t.shape[1], gh * t.shape[2])

    def in_mat(bb):
        return block_diag(bb.transpose(0, 2, 1))

    def out_mat(cc):
        return block_diag(cc.transpose(0, 2, 1))

    bmat = jnp.stack([jnp.concatenate([in_mat(bb_re[h * gh:(h + 1) * gh]),
                                       in_mat(bb_im[h * gh:(h + 1) * gh])], axis=1)
                      for h in range(2)]).astype(BF16)
    cmat = jnp.stack([jnp.concatenate([out_mat(c_re[h * gh:(h + 1) * gh]),
                                       out_mat(-c_im[h * gh:(h + 1) * gh])], axis=0)
                      for h in range(2)]).astype(BF16)
    a_flat = jnp.concatenate([jnp.concatenate([a_re[h * gh:(h + 1) * gh].reshape(1, gh * n),
                                               a_im[h * gh:(h + 1) * gh].reshape(1, gh * n)], axis=1)
                              for h in range(2)], axis=1)
    return (bmat, cmat, a_flat, d.reshape(1, g * p), w_glu.astype(BF16), b_glu.reshape(1, -1))


def _attn_kernel(q_ref, k_ref, v_ref, o_ref, *scr, rows_in, rows):
    hd = q_ref.shape[-1] // N_HEADS
    for i in range(q_ref.shape[0]):
        if rows_in != rows:
            pad_ref = scr[0]
            pad_ref[i] = jnp.zeros(pad_ref.shape[1:], F32)
            pad_ref[i, 0:rows_in, :] = q_ref[i]
            q = pad_ref[i]
        else:
            q = q_ref[i]
        k_all = jnp.transpose(k_ref[i], (1, 0, 2))
        v_all = jnp.transpose(v_ref[i], (1, 0, 2))
        outs = []
        for h in range(N_HEADS):
            cols = slice(h * hd, (h + 1) * hd)
            s = lax.dot_general(q[:, cols].astype(BF16), k_all[h].astype(BF16), _NT,
                                preferred_element_type=F32) * (hd ** -0.5)
            p = jnp.exp(s - jnp.max(s, axis=-1, keepdims=True))
            p = p / jnp.sum(p, axis=-1, keepdims=True)
            outs.append(jnp.dot(p.astype(BF16), v_all[h].astype(BF16), preferred_element_type=F32))
        o = jnp.concatenate(outs, axis=1).astype(o_ref.dtype)
        o_ref[i] = o[0:rows_in] if rows_in != rows else o


def _attn_call(q, mem_k, mem_v, layer, *, tq, nb):
    bsz, seq, d = q.shape
    rows_in = min(tq, seq)
    rows = max(rows_in, 16)
    assert bsz % nb == 0 and seq % rows_in == 0
    kv_spec = pl.BlockSpec((None, nb) + mem_k.shape[2:], lambda b, t: (layer, b, 0, 0, 0))
    return pl.pallas_call(
        functools.partial(_attn_kernel, rows_in=rows_in, rows=rows),
        grid=(bsz // nb, seq // rows_in),
        in_specs=[pl.BlockSpec((nb, rows_in, d), lambda b, t: (b, t, 0)), kv_spec, kv_spec],
        out_specs=pl.BlockSpec((nb, rows_in, d), lambda b, t: (b, t, 0)),
        out_shape=jax.ShapeDtypeStruct((bsz, seq, d), BF16),
        scratch_shapes=[pltpu.VMEM((nb, rows, d), F32)] if rows != rows_in else [],
        compiler_params=_cparams("arbitrary", "arbitrary"),
        name="mem_attn",
    )(q, mem_k, mem_v)


FFN_COLS = 256


def _ffn_kernel(*refs, tm, ts, hs, f_dim, has_hist, has_final):
    n_in = 6 + int(has_hist) + int(has_final)
    x_ref, g_ref, wup_ref, cw_ref, cb_ref, wdn_ref = refs[:6]
    y_ref, state_ref = refs[n_in:n_in + 2]
    gated_ref, hist_ref = refs[n_in + 2:n_in + 4]
    t_idx = pl.program_id(1)

    @pl.when(t_idx == 0)
    def _():
        hist_ref[...] = jnp.zeros(hist_ref.shape, F32)
        if has_hist:
            hist_ref[hs - 2 * ts:hs, :] = refs[6][...]

    x = x_ref[...]
    h = _rms(x, g_ref[...]).astype(BF16)
    row = lax.broadcasted_iota(jnp.int32, (tm, 1), 0)
    for c in range(f_dim // FFN_COLS):
        conv = []
        for part in range(2):
            cols = slice(part * f_dim + c * FFN_COLS, part * f_dim + (c + 1) * FFN_COLS)
            u = jnp.dot(h, wup_ref[:, cols], preferred_element_type=F32)
            if ts == 1:
                prev2, prev1 = hist_ref[hs - 2:hs - 1, cols], hist_ref[hs - 1:hs, cols]
                m1 = jnp.where(row == 0, prev1, pltpu.roll(u, 1, axis=0))
                m2 = jnp.where(row == 0, prev2, jnp.where(row == 1, prev1, pltpu.roll(u, 2, axis=0)))
            else:
                ext = jnp.concatenate([hist_ref[hs - 2 * ts:hs, cols], u], axis=0)
                m2, m1 = ext[0:tm], ext[ts:ts + tm]
            conv.append(cb_ref[:, cols] + cw_ref[0:1, cols] * m2 + cw_ref[1:2, cols] * m1
                        + cw_ref[2:3, cols] * u)
            hist_ref[:, cols] = u[tm - hs:tm]
        gated_ref[:, c * FFN_COLS:(c + 1) * FFN_COLS] = (_silu(conv[0]) * conv[1]).astype(BF16)
    out = x + jnp.dot(gated_ref[...], wdn_ref[...], preferred_element_type=F32)
    if has_final:
        out = _rms(out, refs[n_in - 1][...])
    y_ref[...] = out

    @pl.when(t_idx == pl.num_programs(1) - 1)
    def _():
        state_ref[...] = hist_ref[hs - 2 * ts:hs, :]


def _ffn_call(x, g, w_up, conv_w, conv_b, w_down, hist0, g_final, *, tm, ts):
    ngrp, rows, d = x.shape
    f2 = w_up.shape[1]
    f_dim = f2 // 2
    hs = max(8, 2 * ts)
    assert rows % tm == 0 and tm >= hs and f_dim % FFN_COLS == 0 and (ts == 1 or ts % 8 == 0)
    whole = lambda a: pl.BlockSpec(a.shape, lambda s, t: (0,) * a.ndim)
    args = [x, g, w_up, conv_w, conv_b, w_down]
    in_specs = [pl.BlockSpec((None, tm, d), lambda s, t: (s, t, 0))] + [whole(a) for a in args[1:]]
    st_spec = pl.BlockSpec((None, 2 * ts, f2), lambda s, t: (s, 0, 0))
    if hist0 is not None:
        in_specs.append(st_spec)
        args.append(hist0)
    if g_final is not None:
        in_specs.append(whole(g_final))
        args.append(g_final)
    return pl.pallas_call(
        functools.partial(_ffn_kernel, tm=tm, ts=ts, hs=hs, f_dim=f_dim,
                          has_hist=hist0 is not None, has_final=g_final is not None),
        grid=(ngrp, rows // tm),
        in_specs=in_specs,
        out_specs=[pl.BlockSpec((None, tm, d), lambda s, t: (s, t, 0)), st_spec],
        out_shape=[jax.ShapeDtypeStruct((ngrp, rows, d), F32),
                   jax.ShapeDtypeStruct((ngrp, 2 * ts, f2), F32)],
        scratch_shapes=[pltpu.VMEM((tm, f_dim), BF16), pltpu.VMEM((hs, f2), F32)],
        compiler_params=_cparams("arbitrary", "arbitrary"),
        name="conv_ffn",
    )(*args)


def _trunk(x, mem_k, mem_v, states, p, *, prompt):
    bsz, seq, d = x.shape
    depth = p["norm_mix"].shape[0]
    m = bsz * seq
    x2 = x.reshape(m, d)
    new = {"hgrn": [], "s5_re": [], "s5_im": [], "gla": [], "conv": []}
    if prompt:
        gla_tiles = dict(tb=256, chunk=64)
        attn_tq = 512
    else:
        gla_tiles = dict(tb=seq, chunk=16)
        attn_tq = seq
    for l in range(depth):
        g_mix = p["norm_mix"][l].reshape(1, 1, d)
        if l % 2 == 0:
            e = l // 2
            w_in = p["w_in_ab"][e]
            proj = _norm_proj(x2, g_mix, w_in[None], [w_in.shape[1]], [F32])[0][0]
            kw = N_HEADS * HEAD_DK
            o_a, s_a = _gla_call(
                "hgrn", proj.reshape(bsz, seq, -1),
                [p["hgrn_lb"], p["hgrn_gnorm"][e].reshape(1, -1)],
                None if states is None else states["hgrn"][e],
                layer=l, dv=kw // N_HEADS, seq_len=seq, **gla_tiles)
            mats = p["s5_mats"][e]
            if prompt:
                o_b, sr, si = _s5_call(proj.reshape(bsz, seq, -1), 4, mats, None, tc=64, cw=512)
            else:
                x0 = (states["s5_re"][e].reshape(bsz, -1), states["s5_im"][e].reshape(bsz, -1))
                o_b, sr, si = _s5_call(proj.reshape(bsz, seq, -1), 4, mats, x0, tc=seq, cw=128)
            w_out = p["w_out_ab"][e]
            x2 = _proj_res(x2, [o_a.reshape(m, -1), o_b.reshape(m, -1)], [w_out[:kw], w_out[kw:]])
            new["hgrn"].append(s_a)
            new["s5_re"].append(sr.reshape(bsz, -1, S5_STATE))
            new["s5_im"].append(si.reshape(bsz, -1, S5_STATE))
        else:
            o_idx = l // 2
            w_in = p["w_in_c"][o_idx]
            proj = _norm_proj(x2, g_mix, w_in[None], [w_in.shape[1]], [F32])[0][0]
            o_c, s_c = _gla_call(
                "gla", proj.reshape(bsz, seq, -1),
                [p["gla_w_gate"][o_idx], p["gla_b_gate"][o_idx].reshape(1, -1),
                 p["gla_gnorm"][o_idx].reshape(1, -1)],
                None if states is None else states["gla"][o_idx],
                layer=l, dv=d // N_HEADS, seq_len=seq, **gla_tiles)
            x2 = _proj_res(x2, [o_c.reshape(m, -1)], [p["w_out_c"][o_idx]])
            new["gla"].append(s_c)
        q = _norm_proj(x2, p["norm_cross"][l].reshape(1, 1, d), p["xa_w_q"][l][None], [d], [F32])[0][0]
        o_x = _attn_call(q.reshape(bsz, seq, d), mem_k, mem_v, l, tq=attn_tq, nb=1 if prompt else 4)
        x2 = _proj_res(x2, [o_x.reshape(m, d)], [p["xa_w_o"][l]])
        g_final = p["norm_final"].reshape(1, d) if l == depth - 1 else None
        ffn_w = (p["norm_ffn"][l].reshape(1, d), p["ffn_w_up"][l], p["ffn_conv_w"][l],
                 p["ffn_conv_b"][l].reshape(1, -1), p["ffn_w_down"][l])
        if prompt:
            y, cst = _ffn_call(x2.reshape(bsz, seq, d), *ffn_w, None, g_final, tm=512, ts=1)
            x2 = y.reshape(m, d)
        else:
            xt = x2.reshape(bsz, seq, d).transpose(1, 0, 2).reshape(1, m, d)
            hist0 = states["conv"][l].transpose(1, 0, 2).reshape(1, 2 * bsz, -1)
            y, cst = _ffn_call(xt, *ffn_w, hist0, g_final, tm=m, ts=bsz)
            x2 = y.reshape(seq, bsz, d).transpose(1, 0, 2).reshape(m, d)
            cst = cst.reshape(2, bsz, -1).transpose(1, 0, 2)
        new["conv"].append(cst)
    return x2.reshape(bsz, seq, d), new


def kernel(x_prompt, x_sample, mem_prompt, cache_mem_k, cache_mem_v, state_hgrn, state_s5_re, state_s5_im, state_gla, state_ffn_conv, norm_mix, norm_cross, norm_mem, norm_ffn, norm_final, w_in_ab, hgrn_lb, hgrn_gnorm, s5_lam_re, s5_lam_im, s5_log_step, s5_b_re, s5_b_im, s5_c_re, s5_c_im, s5_d, s5_w_glu, s5_b_glu, w_out_ab, w_in_c, gla_w_gate_up, gla_b_gate, gla_gnorm, w_out_c, xa_w_q, xa_w_kv, xa_w_o, ffn_w_up, ffn_conv_w, ffn_conv_b, ffn_w_down):
    depth, d = norm_mix.shape
    n_mem = mem_prompt.shape[1]
    bsz = x_prompt.shape[0]
    dec_bsz = x_sample.shape[0]

    gla_cols = w_in_c.shape[2]
    gate_rank = gla_w_gate_up.shape[1]
    pad_c = (-gla_cols) % 128
    w_in_c_p = jnp.pad(w_in_c, ((0, 0), (0, 0), (0, pad_c))).astype(BF16)
    gla_w_gate = jnp.pad(gla_w_gate_up, ((0, 0), (0, 128 - gate_rank), (0, 0))).astype(BF16)

    s5_mats = []
    for e in range(s5_lam_re.shape[0]):
        a_re, a_im, bb_re, bb_im = _s5_prep(s5_lam_re[e], s5_lam_im[e], s5_log_step[e],
                                            s5_b_re[e], s5_b_im[e])
        s5_mats.append(_s5_matrices(a_re, a_im, bb_re, bb_im, s5_c_re[e], s5_c_im[e], s5_d[e],
                                    s5_w_glu[e], s5_b_glu[e]))

    p = dict(norm_mix=norm_mix, norm_cross=norm_cross, norm_ffn=norm_ffn, norm_final=norm_final,
             w_in_ab=w_in_ab.astype(BF16), hgrn_lb=hgrn_lb, hgrn_gnorm=hgrn_gnorm, s5_mats=s5_mats,
             w_out_ab=w_out_ab.astype(BF16), w_in_c=w_in_c_p, gla_w_gate=gla_w_gate,
             gla_b_gate=gla_b_gate, gla_gnorm=gla_gnorm, w_out_c=w_out_c.astype(BF16),
             xa_w_q=xa_w_q.astype(BF16), xa_w_o=xa_w_o.astype(BF16),
             ffn_w_up=ffn_w_up.astype(BF16), ffn_conv_w=ffn_conv_w, ffn_conv_b=ffn_conv_b,
             ffn_w_down=ffn_w_down.astype(BF16))

    mem_k_p, mem_v_p = _mem_kv(mem_prompt, norm_mem.reshape(depth, 1, d), xa_w_kv.astype(BF16))
    y_prompt, st_p = _trunk(x_prompt, mem_k_p, mem_v_p, None, p, prompt=True)

    states = dict(hgrn=state_hgrn, s5_re=state_s5_re, s5_im=state_s5_im, gla=state_gla,
                  conv=state_ffn_conv)
    y_sample, st_s = _trunk(x_sample, cache_mem_k, cache_mem_v, states, p, prompt=False)

    stack = lambda xs: jnp.stack(xs)
    return (y_prompt, y_sample,
            stack(st_p["hgrn"]), stack(st_p["s5_re"]), stack(st_p["s5_im"]), stack(st_p["gla"]),
            mem_k_p, mem_v_p, stack(st_p["conv"]),
            stack(st_s["hgrn"]), stack(st_s["s5_re"]), stack(st_s["s5_im"]), stack(st_s["gla"]),
            stack(st_s["conv"]))
```

```python
import functools
import math

import jax
import jax.numpy as jnp
from jax import lax
from jax.experimental import pallas as pl
from jax.experimental.pallas import tpu as pltpu

F32 = jnp.float32
BF16 = jnp.bfloat16

EPS = 1e-6
S5_MAX_RE = -1e-4
GLA_GATE_TAU = 16.0
N_HEADS = 4
HEAD_DK = 128
S5_GROUP = 16
S5_STATE = 64
SUB_BLOCK = 16
GLA_SAFE_DECAY = 64.0
VMEM_LIMIT = 56 * 1024 * 1024

_NT = (((1,), (1,)), ((), ()))
_TN = (((0,), (0,)), ((), ()))


def _cparams(*sem):
    return pltpu.CompilerParams(dimension_semantics=sem, vmem_limit_bytes=VMEM_LIMIT)


def _rms(x, g):
    return x * lax.rsqrt(jnp.mean(x * x, axis=-1, keepdims=True) + EPS) * g


def _sigmoid(x):
    return 1.0 / (1.0 + jnp.exp(-x))


def _silu(x):
    return x * _sigmoid(x)


def _row_tile(rows, want):
    t = min(rows, want)
    assert rows % t == 0, (rows, t)
    return t


def _norm_proj_kernel(x_ref, g_ref, w_ref, *o_refs, widths, has_norm):
    x = x_ref[...]
    h = (_rms(x, g_ref[...]) if has_norm else x).astype(BF16)
    col = 0
    for o_ref, wd in zip(o_refs, widths):
        for c0 in range(0, wd, 512):
            cw = min(512, wd - c0)
            o_ref[:, c0:c0 + cw] = jnp.dot(
                h, w_ref[:, col + c0:col + c0 + cw],
                preferred_element_type=F32).astype(o_ref.dtype)
        col += wd


def _norm_proj(x, g, w, widths, out_dtypes, *, has_norm=True, tm=512):
    m, k = x.shape
    nl, _, n = w.shape
    assert n == sum(widths)
    tm = _row_tile(m, tm)
    return pl.pallas_call(
        functools.partial(_norm_proj_kernel, widths=tuple(widths), has_norm=has_norm),
        grid=(nl, m // tm),
        in_specs=[
            pl.BlockSpec((tm, k), lambda l, i: (i, 0)),
            pl.BlockSpec((None, 1, k), lambda l, i: (l, 0, 0)),
            pl.BlockSpec((None, k, n), lambda l, i: (l, 0, 0)),
        ],
        out_specs=[pl.BlockSpec((None, tm, wd), lambda l, i: (l, i, 0)) for wd in widths],
        out_shape=[jax.ShapeDtypeStruct((nl, m, wd), dt) for wd, dt in zip(widths, out_dtypes)],
        compiler_params=_cparams("arbitrary", "arbitrary"),
        name="norm_proj",
    )(x, g, w)


def _mem_kv_kernel(x_ref, g_ref, w_ref, k_ref, v_ref):
    h = _rms(x_ref[...], g_ref[...]).astype(BF16)
    d = x_ref.shape[-1]
    hd = d // N_HEADS
    for o_ref, base in ((k_ref, 0), (v_ref, d)):
        for hh in range(N_HEADS):
            o_ref[:, hh, :] = jnp.dot(h, w_ref[:, base + hh * hd:base + (hh + 1) * hd],
                                      preferred_element_type=F32)


def _mem_kv(mem, g, w):
    bsz, n_mem, d = mem.shape
    depth = w.shape[0]
    out_spec = pl.BlockSpec((None, None, n_mem, N_HEADS, d // N_HEADS), lambda l, b: (l, b, 0, 0, 0))
    out_shape = jax.ShapeDtypeStruct((depth, bsz, n_mem, N_HEADS, d // N_HEADS), F32)
    return pl.pallas_call(
        _mem_kv_kernel,
        grid=(depth, bsz),
        in_specs=[pl.BlockSpec((None, n_mem, d), lambda l, b: (b, 0, 0)),
                  pl.BlockSpec((None, 1, d), lambda l, b: (l, 0, 0)),
                  pl.BlockSpec((None, d, 2 * d), lambda l, b: (l, 0, 0))],
        out_specs=[out_spec, out_spec],
        out_shape=[out_shape, out_shape],
        compiler_params=_cparams("arbitrary", "arbitrary"),
        name="mem_kv",
    )(mem, g, w)


def _proj_res_kernel(res_ref, *refs, n_in):
    acc = res_ref[...]
    for a_ref, w_ref in zip(refs[:n_in], refs[n_in:2 * n_in]):
        acc = acc + jnp.dot(a_ref[...].astype(BF16), w_ref[...], preferred_element_type=F32)
    refs[2 * n_in][...] = acc


def _proj_res(res, a_list, w_list, *, tm=512):
    m, n = res.shape
    tm = _row_tile(m, tm)
    n_in = len(a_list)
    in_specs = [pl.BlockSpec((tm, n), lambda i: (i, 0))]
    in_specs += [pl.BlockSpec((tm, a.shape[1]), lambda i: (i, 0)) for a in a_list]
    in_specs += [pl.BlockSpec(w.shape, lambda i: (0, 0)) for w in w_list]
    return pl.pallas_call(
        functools.partial(_proj_res_kernel, n_in=n_in),
        grid=(m // tm,),
        in_specs=in_specs,
        out_specs=pl.BlockSpec((tm, n), lambda i: (i, 0)),
        out_shape=jax.ShapeDtypeStruct((m, n), F32),
        compiler_params=_cparams("arbitrary"),
        name="proj_res",
    )(res, *a_list, *w_list)


def _cumsum_rows(x, c):
    rows = lax.broadcasted_iota(jnp.int32, x.shape, 0) % c
    s = 1
    while s < c:
        x = x + jnp.where(rows >= s, pltpu.roll(x, s, axis=0), 0.0)
        s *= 2
    return x


def _gla_head(qh, kh, bh, vh, st_ref, small_decay):
    c = qh.shape[0]
    sb = min(SUB_BLOCK, c)
    vb = vh.astype(BF16)
    q_in = (qh * jnp.exp(bh)).astype(BF16)
    o_inter = lax.dot_general(q_in, st_ref[...].astype(BF16), _NT, preferred_element_type=F32)
    if small_decay:
        kf = (kh * jnp.exp(-bh)).astype(BF16)
        a = lax.dot_general(q_in, kf, _NT, preferred_element_type=F32)
        causal = (lax.broadcasted_iota(jnp.int32, (c, c), 0) >= lax.broadcasted_iota(jnp.int32, (c, c), 1))
        return o_inter + jnp.dot(jnp.where(causal, a, 0.0).astype(BF16), vb, preferred_element_type=F32)
    rows = lax.broadcasted_iota(jnp.int32, (sb, 1), 0)
    parts = []
    for s in range(c // sb):
        r0 = s * sb
        qs, ks, bs, vs = qh[r0:r0 + sb], kh[r0:r0 + sb], bh[r0:r0 + sb], vh[r0:r0 + sb]
        acc = o_inter[r0:r0 + sb]
        if s > 0:
            ref_b = bh[r0 - 1:r0, :]
            qf = (qs * jnp.exp(bs - ref_b)).astype(BF16)
            kf = (kh[0:r0] * jnp.exp(ref_b - bh[0:r0])).astype(BF16)
            a_off = lax.dot_general(qf, kf, _NT, preferred_element_type=F32)
            acc = acc + jnp.dot(a_off.astype(BF16), vb[0:r0], preferred_element_type=F32)
        for j in range(sb):
            w = jnp.exp(jnp.minimum(bs - bs[j:j + 1], 0.0)) * qs * ks[j:j + 1]
            col = jnp.where(rows >= j, jnp.sum(w, axis=-1, keepdims=True), 0.0)
            acc = acc + col * vs[j:j + 1]
        parts.append(acc)
    return parts[0] if len(parts) == 1 else jnp.concatenate(parts, axis=0)


def _gla_state_update(kh, bh, vh, st_ref):
    c = kh.shape[0]
    b_last = bh[c - 1:c, :]
    kd = (kh * jnp.exp(b_last - bh)).astype(BF16)
    st_ref[...] = st_ref[...] * jnp.exp(b_last) + lax.dot_general(vh.astype(BF16), kd, _TN,
                                                                   preferred_element_type=F32)


def _gla_kernel(*refs, mode, layer, dv, nb, rows_in, chunk, n_chunks, has_s0):
    n_in = (6 if mode == "hgrn" else 8) + (1 if has_s0 else 0)
    ins, (o_ref, sout_ref), scr = refs[:n_in], refs[n_in:n_in + 2], refs[n_in + 2:]
    st_ref = scr[0]
    pad_refs = scr[1:]
    n_act = 4 if mode == "hgrn" else 5
    t_idx = pl.program_id(1)
    padded = rows_in < chunk
    assert padded or nb == 1

    @pl.when(t_idx == 0)
    def _():
        for i in range(nb):
            for h in range(N_HEADS):
                if has_s0:
                    st_ref[i * N_HEADS + h] = ins[-1][i, h].T
                else:
                    st_ref[i * N_HEADS + h] = jnp.zeros(st_ref.shape[1:], F32)

    if padded:
        for p_ref, a_ref in zip(pad_refs, ins[:n_act]):
            p_ref[...] = jnp.zeros(p_ref.shape, F32)
            for i in range(nb):
                p_ref[i * chunk:i * chunk + rows_in, :] = a_ref[i]
        acts = pad_refs
    else:
        acts = [a.at[0] for a in ins[:n_act]]
    span = nb * chunk

    def one_chunk(c, carry):
        r0 = c * span if isinstance(c, int) else pl.multiple_of(c * span, span)
        ld = [a[pl.ds(r0, span), :] for a in acts]
        if mode == "hgrn":
            q_raw, f, v, gate = ld
            lb_ref, gn_ref = ins[4], ins[5]
            lbv = lb_ref[...]
            e = jnp.exp(lbv - jnp.max(lbv, axis=0, keepdims=True))
            lb = jnp.sum(e[0:layer + 1], axis=0, keepdims=True) / jnp.sum(e, axis=0, keepdims=True)
            forget = lb + (1.0 - lb) * _sigmoid(f)
            k = 1.0 - forget
            lg = jnp.log(forget)
            q = _silu(q_raw)
        else:
            q_raw, k, v, gate, gd = ld
            wg_ref, bg_ref, gn_ref = ins[5], ins[6], ins[7]
            z = jnp.dot(gd.astype(BF16), wg_ref[...], preferred_element_type=F32) + bg_ref[...]
            lg = (jnp.minimum(z, 0.0) - jnp.log(1.0 + jnp.exp(-jnp.abs(z)))) / GLA_GATE_TAU
            q = q_raw * (HEAD_DK ** -0.5)
        if padded:
            live = lax.broadcasted_iota(jnp.int32, (span, 1), 0) % chunk < rows_in
            lg = jnp.where(live, lg, 0.0)
            k = jnp.where(live, k, 0.0)
        b = _cumsum_rows(lg, chunk)

        def piece(x, i, h, width):
            return x[i * chunk:(i + 1) * chunk, h * width:(h + 1) * width]

        def heads(small_decay):
            return jnp.concatenate(
                [jnp.concatenate(
                    [_gla_head(piece(q, i, h, HEAD_DK), piece(k, i, h, HEAD_DK), piece(b, i, h, HEAD_DK),
                               piece(v, i, h, dv), st_ref.at[i * N_HEADS + h], small_decay)
                     for h in range(N_HEADS)], axis=1) for i in range(nb)], axis=0)

        o_raw = lax.cond(jnp.min(b) >= -GLA_SAFE_DECAY,
                         functools.partial(heads, True), functools.partial(heads, False))
        for i in range(nb):
            for h in range(N_HEADS):
                _gla_state_update(piece(k, i, h, HEAD_DK), piece(b, i, h, HEAD_DK), piece(v, i, h, dv),
                                  st_ref.at[i * N_HEADS + h])
        o_all = jnp.concatenate(
            [_rms(o_raw[:, h * dv:(h + 1) * dv], gn_ref[...]) * _silu(gate[:, h * dv:(h + 1) * dv])
             for h in range(N_HEADS)], axis=1).astype(o_ref.dtype)
        if padded:
            for i in range(nb):
                o_ref[i] = o_all[i * chunk:i * chunk + rows_in]
        else:
            o_ref[0, pl.ds(r0, span), :] = o_all
        return carry

    if n_chunks == 1:
        one_chunk(0, 0)
    else:
        lax.fori_loop(0, n_chunks, one_chunk, 0)

    @pl.when(t_idx == pl.num_programs(1) - 1)
    def _():
        for i in range(nb):
            for h in range(N_HEADS):
                sout_ref[i, h] = st_ref[i * N_HEADS + h].T


def _gla_call(mode, proj, params, s0, *, layer, dv, tb, chunk, nb):
    bsz, seq, _ = proj.shape
    rows_in = min(tb, seq)
    if rows_in < chunk:
        assert seq == rows_in and bsz % nb == 0
        nt, n_chunks = 1, 1
    else:
        assert seq % tb == 0 and tb % chunk == 0 and nb == 1
        nt, n_chunks = seq // tb, tb // chunk
    kw, vw = N_HEADS * HEAD_DK, N_HEADS * dv

    def act(width, col_block):
        return pl.BlockSpec((nb, rows_in, width), lambda b, t: (b, t, col_block))

    def whole(a):
        return pl.BlockSpec(a.shape, lambda b, t: (0,) * a.ndim)

    if mode == "hgrn":
        act_specs = [act(kw, 0), act(kw, 1), act(vw, 2), act(vw, 3)]
        act_widths = [kw, kw, vw, vw]
    else:
        act_specs = [act(kw, 0), act(kw, 1), act(vw, kw * 2 // vw), act(vw, kw * 2 // vw + 1),
                     act(128, (2 * kw + 2 * vw) // 128)]
        act_widths = [kw, kw, vw, vw, 128]
    in_specs = act_specs + [whole(p) for p in params]
    args = [proj] * len(act_specs) + list(params)
    state_spec = pl.BlockSpec((nb, N_HEADS, HEAD_DK, dv), lambda b, t: (b, 0, 0, 0))
    if s0 is not None:
        in_specs.append(state_spec)
        args.append(s0)
    scratch = [pltpu.VMEM((nb * N_HEADS, dv, HEAD_DK), F32)]
    if rows_in < chunk:
        scratch += [pltpu.VMEM((nb * chunk, w), F32) for w in act_widths]
    return pl.pallas_call(
        functools.partial(_gla_kernel, mode=mode, layer=layer, dv=dv, nb=nb, rows_in=rows_in,
                          chunk=chunk, n_chunks=n_chunks, has_s0=s0 is not None),
        grid=(bsz // nb, nt),
        in_specs=in_specs,
        out_specs=[pl.BlockSpec((nb, rows_in, vw), lambda b, t: (b, t, 0)), state_spec],
        out_shape=[jax.ShapeDtypeStruct((bsz, seq, vw), BF16),
                   jax.ShapeDtypeStruct((bsz, N_HEADS, HEAD_DK, dv), F32)],
        scratch_shapes=scratch,
        compiler_params=_cparams("arbitrary", "arbitrary"),
        name="gla_" + mode,
    )(*args)


def _s5_prep_kernel(lre_ref, lim_ref, ls_ref, lre_x_ref, lim_x_ref, ls_x_ref, bre_ref, bim_ref,
                    are_ref, aim_ref, bbre_ref, bbim_ref):
    def disc(lre, lim, ls):
        lr = jnp.minimum(lre, S5_MAX_RE)
        dt = jnp.exp(ls)
        mag = jnp.exp(lr * dt)
        a_re = mag * jnp.cos(lim * dt)
        a_im = mag * jnp.sin(lim * dt)
        den = lr * lr + lim * lim
        z_re = ((a_re - 1.0) * lr + a_im * lim) / den
        z_im = (a_im * lr - (a_re - 1.0) * lim) / den
        return a_re, a_im, z_re, z_im

    a_re, a_im, _, _ = disc(lre_ref[...], lim_ref[...], ls_ref[...])
    are_ref[...] = a_re
    aim_ref[...] = a_im
    _, _, z_re, z_im = disc(lre_x_ref[...], lim_x_ref[...], ls_x_ref[...])
    bbre_ref[...] = z_re * bre_ref[...] - z_im * bim_ref[...]
    bbim_ref[...] = z_re * bim_ref[...] + z_im * bre_ref[...]


def _s5_prep(lam_re, lam_im, log_step, b_re, b_im):
    g, n = lam_re.shape
    p = b_re.shape[-1]
    ls = jnp.broadcast_to(log_step[:, None], (g, n))
    rep = lambda a: jnp.repeat(a, p, axis=1)
    outs = pl.pallas_call(
        _s5_prep_kernel,
        out_shape=[jax.ShapeDtypeStruct((g, n), F32)] * 2 + [jax.ShapeDtypeStruct((g, n * p), F32)] * 2,
        name="s5_prep",
    )(lam_re, lam_im, ls, rep(lam_re), rep(lam_im), rep(ls),
      b_re.reshape(g, n * p), b_im.reshape(g, n * p))
    a_re, a_im, bb_re, bb_im = outs
    return a_re, a_im, bb_re.reshape(g, n, p), bb_im.reshape(g, n, p)


def _s5_kernel(*refs, bg, tc, nt, cw, has_x0):
    n_in = 9 if has_x0 else 7
    u_ref, bm_ref, cm_ref, a_ref, d_ref, wg_ref, bgl_ref = refs[:7]
    o_ref, sre_ref, sim_ref = refs[n_in:n_in + 3]
    utm_ref, xs_ref, st_ref = refs[n_in + 3:]
    t_idx = pl.program_id(1)
    half = xs_ref.shape[1] // 2
    uw = u_ref.shape[-1] // 2

    @pl.when(t_idx == 0)
    def _():
        for hf in range(2):
            if has_x0:
                st_ref[:, hf * 2 * half:hf * 2 * half + half] = refs[7][:, hf * half:(hf + 1) * half]
                st_ref[:, hf * 2 * half + half:(hf + 1) * 2 * half] = refs[8][:, hf * half:(hf + 1) * half]
            else:
                st_ref[...] = jnp.zeros(st_ref.shape, F32)

    for t in range(tc):
        utm_ref[t * bg:(t + 1) * bg, :] = u_ref[:, t, :]
    u = utm_ref[...]
    ys = []
    for hf in range(2):
        xs_ref[...] = jnp.dot(u[:, hf * uw:(hf + 1) * uw].astype(BF16), bm_ref[hf],
                              preferred_element_type=F32)
        base = hf * 2 * half
        for c0 in range(0, half, cw):
            ar = a_ref[0:1, base + c0:base + c0 + cw]
            ai = a_ref[0:1, base + half + c0:base + half + c0 + cw]
            xr = st_ref[:, base + c0:base + c0 + cw]
            xi = st_ref[:, base + half + c0:base + half + c0 + cw]
            for t in range(tc):
                rows = slice(t * bg, (t + 1) * bg)
                nr = ar * xr - ai * xi + xs_ref[rows, c0:c0 + cw]
                ni = ar * xi + ai * xr + xs_ref[rows, half + c0:half + c0 + cw]
                xs_ref[rows, c0:c0 + cw] = nr
                xs_ref[rows, half + c0:half + c0 + cw] = ni
                xr, xi = nr, ni
            st_ref[:, base + c0:base + c0 + cw] = xr
            st_ref[:, base + half + c0:base + half + c0 + cw] = xi
        ys.append(jnp.dot(xs_ref[...].astype(BF16), cm_ref[hf], preferred_element_type=F32))
    y = jnp.concatenate(ys, axis=1) + d_ref[...] * u
    y = 0.5 * y * (1.0 + jnp.tanh(math.sqrt(2.0 / math.pi) * (y + 0.044715 * (y * y * y))))
    gate = jnp.dot(y.astype(BF16), wg_ref[...], preferred_element_type=F32) + bgl_ref[...]
    utm_ref[...] = y * _sigmoid(gate)
    for t in range(tc):
        o_ref[:, t, :] = utm_ref[t * bg:(t + 1) * bg, :]

    @pl.when(t_idx == nt - 1)
    def _():
        for hf in range(2):
            sre_ref[:, hf * half:(hf + 1) * half] = st_ref[:, hf * 2 * half:hf * 2 * half + half]
            sim_ref[:, hf * half:(hf + 1) * half] = st_ref[:, hf * 2 * half + half:(hf + 1) * 2 * half]


def _s5_call(u_src, col_block, mats, x0, *, tc, cw):
    bg, seq, _ = u_src.shape
    assert seq % tc == 0
    nt = seq // tc
    bmat, cmat, a_flat, d_row, w_glu, b_glu = mats
    width = d_row.shape[1]
    nstate = a_flat.shape[1] // 2
    whole = lambda a: pl.BlockSpec(a.shape, lambda g, t: (0,) * a.ndim)
    in_specs = [pl.BlockSpec((bg, tc, width), lambda g, t: (0, t, col_block))]
    in_specs += [whole(m) for m in mats]
    args = [u_src] + list(mats)
    st_spec = pl.BlockSpec((bg, nstate), lambda g, t: (0, 0))
    if x0 is not None:
        in_specs += [st_spec, st_spec]
        args += list(x0)
    return pl.pallas_call(
        functools.partial(_s5_kernel, bg=bg, tc=tc, nt=nt, cw=cw, has_x0=x0 is not None),
        grid=(1, nt),
        in_specs=in_specs,
        out_specs=[pl.BlockSpec((bg, tc, width), lambda g, t: (0, t, 0)), st_spec, st_spec],
        out_shape=[jax.ShapeDtypeStruct((bg, seq, width), F32),
                   jax.ShapeDtypeStruct((bg, nstate), F32),
                   jax.ShapeDtypeStruct((bg, nstate), F32)],
        scratch_shapes=[pltpu.VMEM((bg * tc, width), F32), pltpu.VMEM((bg * tc, nstate), F32),
                        pltpu.VMEM((bg, 2 * nstate), F32)],
        compiler_params=_cparams("arbitrary", "arbitrary"),
        name="s5",
    )(*args)


def _s5_matrices(a_re, a_im, bb_re, bb_im, c_re, c_im, d, w_glu, b_glu):
    g, n, p = bb_re.shape
    gh = g // 2
    eye = jnp.eye(gh, dtype=F32)

    def block_diag(t):
        return (eye[:, None, :, None] * t[:, :, None, :]).reshape(gh * t.shape[1], gh * t.shape[2])

    def in_mat(bb):
        return block_diag(bb.transpose(0, 2, 1))

    def out_mat(cc):
        return block_diag(cc.transpose(0, 2, 1))

    bmat = jnp.stack([jnp.concatenate([in_mat(bb_re[h * gh:(h + 1) * gh]),
                                       in_mat(bb_im[h * gh:(h + 1) * gh])], axis=1)
                      for h in range(2)]).astype(BF16)
    cmat = jnp.stack([jnp.concatenate([out_mat(c_re[h * gh:(h + 1) * gh]),
                                       out_mat(-c_im[h * gh:(h + 1) * gh])], axis=0)
                      for h in range(2)]).astype(BF16)
    a_flat = jnp.concatenate([jnp.concatenate([a_re[h * gh:(h + 1) * gh].reshape(1, gh * n),
                                               a_im[h * gh:(h + 1) * gh].reshape(1, gh * n)], axis=1)
                              for h in range(2)], axis=1)
    return (bmat, cmat, a_flat, d.reshape(1, g * p), w_glu.astype(BF16), b_glu.reshape(1, -1))


def _xattn_kernel(*refs, n_pre):
    x_ref = refs[0]
    g_ref, wq_ref, wo_ref, k_ref, v_ref, y_ref, kt_ref, vt_ref = refs[1 + 2 * n_pre:]

    @pl.when(pl.program_id(1) == 0)
    def _():
        kt_ref[...] = jnp.transpose(k_ref[...], (1, 0, 2)).astype(BF16)
        vt_ref[...] = jnp.transpose(v_ref[...], (1, 0, 2)).astype(BF16)

    x = x_ref[...]
    for a_ref, w_ref in zip(refs[1:1 + n_pre], refs[1 + n_pre:1 + 2 * n_pre]):
        x = x + jnp.dot(a_ref[...].astype(BF16), w_ref[...], preferred_element_type=F32)
    q = jnp.dot(_rms(x, g_ref[...]).astype(BF16), wq_ref[...], preferred_element_type=F32).astype(BF16)
    hd = q.shape[1] // N_HEADS
    outs = []
    for h in range(N_HEADS):
        s = lax.dot_general(q[:, h * hd:(h + 1) * hd], kt_ref[h], _NT,
                            preferred_element_type=F32) * (hd ** -0.5)
        p = jnp.exp(s - jnp.max(s, axis=-1, keepdims=True))
        p = p / jnp.sum(p, axis=-1, keepdims=True)
        outs.append(jnp.dot(p.astype(BF16), vt_ref[h], preferred_element_type=F32).astype(BF16))
    y_ref[...] = x + jnp.dot(jnp.concatenate(outs, axis=1), wo_ref[...], preferred_element_type=F32)


def _xattn_call(x, pre_a, pre_w, g, w_q, w_o, mem_k, mem_v, layer, *, tq):
    bsz, seq, d = x.shape
    n_mem, nh, hd = mem_k.shape[2:]
    assert seq % tq == 0
    kv_spec = pl.BlockSpec((None, None, n_mem, nh, hd), lambda b, t: (layer, b, 0, 0, 0))
    whole = lambda a: pl.BlockSpec(a.shape, lambda b, t: (0,) * a.ndim)
    rows = lambda a: pl.BlockSpec((None, tq, a.shape[-1]), lambda b, t: (b, t, 0))
    return pl.pallas_call(
        functools.partial(_xattn_kernel, n_pre=len(pre_a)),
        grid=(bsz, seq // tq),
        in_specs=[rows(x)] + [rows(a) for a in pre_a] + [whole(w) for w in pre_w]
        + [whole(g), whole(w_q), whole(w_o), kv_spec, kv_spec],
        out_specs=rows(x),
        out_shape=jax.ShapeDtypeStruct((bsz, seq, d), F32),
        scratch_shapes=[pltpu.VMEM((nh, n_mem, hd), BF16), pltpu.VMEM((nh, n_mem, hd), BF16)],
        compiler_params=_cparams("arbitrary", "arbitrary"),
        name="xattn",
    )(x, *pre_a, *pre_w, g, w_q, w_o, mem_k, mem_v)


def _attn_rows_kernel(q_ref, k_ref, v_ref, o_ref, pad_ref, *, rows_in):
    nb, rows, d = pad_ref.shape
    nblk = d // 128
    half_blk = nblk // 2
    lanes = k_ref.shape[1]
    lane_blk = lax.broadcasted_iota(jnp.int32, (1, lanes), 1) % nblk
    row_head = lax.broadcasted_iota(jnp.int32, (N_HEADS * rows, 1), 0) // rows
    live = lane_blk == row_head
    scale = (d // N_HEADS) ** -0.5
    for i in range(nb):
        pad_ref[i] = jnp.zeros((rows, d), F32)
        pad_ref[i, 0:rows_in, :] = q_ref[i]
        q = pad_ref[i]
        qx = jnp.concatenate([q[:, j * 128:(j + 1) * 128] for j in range(nblk)], axis=0)
        g = lax.dot_general(qx.astype(BF16), k_ref[i].astype(BF16), _NT, preferred_element_type=F32)
        s = jnp.concatenate(
            [g[2 * h * rows:(2 * h + 1) * rows]
             + pltpu.roll(g[(2 * h + 1) * rows:(2 * h + 2) * rows], lanes - half_blk, axis=1)
             for h in range(N_HEADS)], axis=0) * scale
        s = jnp.where(live, s, -1e30)
        e = jnp.exp(s - jnp.max(s, axis=-1, keepdims=True))
        p = e / jnp.sum(e, axis=-1, keepdims=True)
        px = jnp.concatenate(
            [blk for h in range(N_HEADS)
             for blk in (p[h * rows:(h + 1) * rows], pltpu.roll(p[h * rows:(h + 1) * rows], half_blk, axis=1))],
            axis=0)
        o = jnp.dot(px.astype(BF16), v_ref[i].astype(BF16), preferred_element_type=F32)
        o = jnp.concatenate([o[j * rows:(j + 1) * rows] for j in range(nblk)], axis=1)
        o_ref[i] = o[0:rows_in].astype(o_ref.dtype)


def _attn_rows_call(q, mem_k, mem_v, layer, *, nb):
    bsz, seq, d = q.shape
    depth, _, n_mem, nh, hd = mem_k.shape
    assert bsz % nb == 0 and hd == 256 and nh == N_HEADS and seq <= 16
    as_rows = lambda a: a.reshape(depth, bsz, n_mem, nh, 2, 128).transpose(0, 1, 2, 4, 3, 5).reshape(
        depth, bsz, n_mem * 2 * nh, 128)
    kv_spec = pl.BlockSpec((None, nb, n_mem * 2 * nh, 128), lambda b: (layer, b, 0, 0))
    return pl.pallas_call(
        functools.partial(_attn_rows_kernel, rows_in=seq),
        grid=(bsz // nb,),
        in_specs=[pl.BlockSpec((nb, seq, d), lambda b: (b, 0, 0)), kv_spec, kv_spec],
        out_specs=pl.BlockSpec((nb, seq, d), lambda b: (b, 0, 0)),
        out_shape=jax.ShapeDtypeStruct((bsz, seq, d), BF16),
        scratch_shapes=[pltpu.VMEM((nb, 16, d), F32)],
        compiler_params=_cparams("arbitrary"),
        name="mem_attn_rows",
    )(q, as_rows(mem_k), as_rows(mem_v))


FFN_COLS = 256


def _ffn_kernel(*refs, tm, ts, hs, f_dim, has_hist, has_final):
    n_in = 6 + int(has_hist) + int(has_final)
    x_ref, g_ref, wup_ref, cw_ref, cb_ref, wdn_ref = refs[:6]
    y_ref, state_ref = refs[n_in:n_in + 2]
    gated_ref, hist_ref = refs[n_in + 2:n_in + 4]
    t_idx = pl.program_id(1)

    @pl.when(t_idx == 0)
    def _():
        hist_ref[...] = jnp.zeros(hist_ref.shape, F32)
        if has_hist:
            hist_ref[hs - 2 * ts:hs, :] = refs[6][...]

    x = x_ref[...]
    h = _rms(x, g_ref[...]).astype(BF16)
    row = lax.broadcasted_iota(jnp.int32, (tm, 1), 0)
    for c in range(f_dim // FFN_COLS):
        conv = []
        for part in range(2):
            cols = slice(part * f_dim + c * FFN_COLS, part * f_dim + (c + 1) * FFN_COLS)
            u = jnp.dot(h, wup_ref[:, cols], preferred_element_type=F32)
            if ts == 1:
                prev2, prev1 = hist_ref[hs - 2:hs - 1, cols], hist_ref[hs - 1:hs, cols]
                m1 = jnp.where(row == 0, prev1, pltpu.roll(u, 1, axis=0))
                m2 = jnp.where(row == 0, prev2, jnp.where(row == 1, prev1, pltpu.roll(u, 2, axis=0)))
            else:
                ext = jnp.concatenate([hist_ref[hs - 2 * ts:hs, cols], u], axis=0)
                m2, m1 = ext[0:tm], ext[ts:ts + tm]
            conv.append(cb_ref[:, cols] + cw_ref[0:1, cols] * m2 + cw_ref[1:2, cols] * m1
                        + cw_ref[2:3, cols] * u)
            hist_ref[:, cols] = u[tm - hs:tm]
        gated_ref[:, c * FFN_COLS:(c + 1) * FFN_COLS] = (_silu(conv[0]) * conv[1]).astype(BF16)
    out = x + jnp.dot(gated_ref[...], wdn_ref[...], preferred_element_type=F32)
    if has_final:
        out = _rms(out, refs[n_in - 1][...])
    y_ref[...] = out

    @pl.when(t_idx == pl.num_programs(1) - 1)
    def _():
        state_ref[...] = hist_ref[hs - 2 * ts:hs, :]


def _ffn_call(x, g, w_up, conv_w, conv_b, w_down, hist0, g_final, *, tm, ts):
    ngrp, rows, d = x.shape
    f2 = w_up.shape[1]
    f_dim = f2 // 2
    hs = max(8, 2 * ts)
    assert rows % tm == 0 and tm >= hs and f_dim % FFN_COLS == 0 and (ts == 1 or ts % 8 == 0)
    whole = lambda a: pl.BlockSpec(a.shape, lambda s, t: (0,) * a.ndim)
    args = [x, g, w_up, conv_w, conv_b, w_down]
    in_specs = [pl.BlockSpec((None, tm, d), lambda s, t: (s, t, 0))] + [whole(a) for a in args[1:]]
    st_spec = pl.BlockSpec((None, 2 * ts, f2), lambda s, t: (s, 0, 0))
    if hist0 is not None:
        in_specs.append(st_spec)
        args.append(hist0)
    if g_final is not None:
        in_specs.append(whole(g_final))
        args.append(g_final)
    return pl.pallas_call(
        functools.partial(_ffn_kernel, tm=tm, ts=ts, hs=hs, f_dim=f_dim,
                          has_hist=hist0 is not None, has_final=g_final is not None),
        grid=(ngrp, rows // tm),
        in_specs=in_specs,
        out_specs=[pl.BlockSpec((None, tm, d), lambda s, t: (s, t, 0)), st_spec],
        out_shape=[jax.ShapeDtypeStruct((ngrp, rows, d), F32),
                   jax.ShapeDtypeStruct((ngrp, 2 * ts, f2), F32)],
        scratch_shapes=[pltpu.VMEM((tm, f_dim), BF16), pltpu.VMEM((hs, f2), F32)],
        compiler_params=_cparams("arbitrary", "arbitrary"),
        name="conv_ffn",
    )(*args)


def _trunk(x, mem_k, mem_v, states, p, *, prompt):
    bsz, seq, d = x.shape
    depth = p["norm_mix"].shape[0]
    m = bsz * seq
    x2 = x.reshape(m, d)
    new = {"hgrn": [], "s5_re": [], "s5_im": [], "gla": [], "conv": []}
    if prompt:
        gla_tiles = dict(tb=256, chunk=64, nb=1)
        attn_tq = 512
    else:
        gla_tiles = dict(tb=seq, chunk=16, nb=8)
        attn_tq = seq
    for l in range(depth):
        g_mix = p["norm_mix"][l].reshape(1, 1, d)
        if l % 2 == 0:
            e = l // 2
            w_in = p["w_in_ab"][e]
            proj = _norm_proj(x2, g_mix, w_in[None], [w_in.shape[1]], [F32])[0][0]
            kw = N_HEADS * HEAD_DK
            o_a, s_a = _gla_call(
                "hgrn", proj.reshape(bsz, seq, -1),
                [p["hgrn_lb"], p["hgrn_gnorm"][e].reshape(1, -1)],
                None if states is None else states["hgrn"][e],
                layer=l, dv=kw // N_HEADS, **gla_tiles)
            mats = p["s5_mats"][e]
            if prompt:
                o_b, sr, si = _s5_call(proj.reshape(bsz, seq, -1), 4, mats, None, tc=64, cw=512)
            else:
                x0 = (states["s5_re"][e].reshape(bsz, -1), states["s5_im"][e].reshape(bsz, -1))
                o_b, sr, si = _s5_call(proj.reshape(bsz, seq, -1), 4, mats, x0, tc=seq, cw=128)
            w_out = p["w_out_ab"][e]
            mixed, mixed_w = [o_a, o_b], [w_out[:kw], w_out[kw:]]
            new["hgrn"].append(s_a)
            new["s5_re"].append(sr.reshape(bsz, -1, S5_STATE))
            new["s5_im"].append(si.reshape(bsz, -1, S5_STATE))
        else:
            o_idx = l // 2
            w_in = p["w_in_c"][o_idx]
            proj = _norm_proj(x2, g_mix, w_in[None], [w_in.shape[1]], [F32])[0][0]
            o_c, s_c = _gla_call(
                "gla", proj.reshape(bsz, seq, -1),
                [p["gla_w_gate"][o_idx], p["gla_b_gate"][o_idx].reshape(1, -1),
                 p["gla_gnorm"][o_idx].reshape(1, -1)],
                None if states is None else states["gla"][o_idx],
                layer=l, dv=d // N_HEADS, **gla_tiles)
            mixed, mixed_w = [o_c], [p["w_out_c"][o_idx]]
            new["gla"].append(s_c)
        if prompt:
            x2 = _xattn_call(x2.reshape(bsz, seq, d), mixed, mixed_w, p["norm_cross"][l].reshape(1, d),
                             p["xa_w_q"][l], p["xa_w_o"][l], mem_k, mem_v, l, tq=attn_tq).reshape(m, d)
        else:
            x2 = _proj_res(x2, [a.reshape(m, -1) for a in mixed], mixed_w)
            q = _norm_proj(x2, p["norm_cross"][l].reshape(1, 1, d), p["xa_w_q"][l][None], [d], [F32])[0][0]
            o_x = _attn_rows_call(q.reshape(bsz, seq, d), mem_k, mem_v, l, nb=4)
            x2 = _proj_res(x2, [o_x.reshape(m, d)], [p["xa_w_o"][l]])
        g_final = p["norm_final"].reshape(1, d) if l == depth - 1 else None
        ffn_w = (p["norm_ffn"][l].reshape(1, d), p["ffn_w_up"][l], p["ffn_conv_w"][l],
                 p["ffn_conv_b"][l].reshape(1, -1), p["ffn_w_down"][l])
        if prompt:
            y, cst = _ffn_call(x2.reshape(bsz, seq, d), *ffn_w, None, g_final, tm=512, ts=1)
            x2 = y.reshape(m, d)
        else:
            xt = x2.reshape(bsz, seq, d).transpose(1, 0, 2).reshape(1, m, d)
            hist0 = states["conv"][l].transpose(1, 0, 2).reshape(1, 2 * bsz, -1)
            y, cst = _ffn_call(xt, *ffn_w, hist0, g_final, tm=m, ts=bsz)
            x2 = y.reshape(seq, bsz, d).transpose(1, 0, 2).reshape(m, d)
            cst = cst.reshape(2, bsz, -1).transpose(1, 0, 2)
        new["conv"].append(cst)
    return x2.reshape(bsz, seq, d), new


def kernel(x_prompt, x_sample, mem_prompt, cache_mem_k, cache_mem_v, state_hgrn, state_s5_re, state_s5_im, state_gla, state_ffn_conv, norm_mix, norm_cross, norm_mem, norm_ffn, norm_final, w_in_ab, hgrn_lb, hgrn_gnorm, s5_lam_re, s5_lam_im, s5_log_step, s5_b_re, s5_b_im, s5_c_re, s5_c_im, s5_d, s5_w_glu, s5_b_glu, w_out_ab, w_in_c, gla_w_gate_up, gla_b_gate, gla_gnorm, w_out_c, xa_w_q, xa_w_kv, xa_w_o, ffn_w_up, ffn_conv_w, ffn_conv_b, ffn_w_down):
    depth, d = norm_mix.shape
    n_mem = mem_prompt.shape[1]
    bsz = x_prompt.shape[0]
    dec_bsz = x_sample.shape[0]

    gla_cols = w_in_c.shape[2]
    gate_rank = gla_w_gate_up.shape[1]
    pad_c = (-gla_cols) % 128
    w_in_c_p = jnp.pad(w_in_c, ((0, 0), (0, 0), (0, pad_c))).astype(BF16)
    gla_w_gate = jnp.pad(gla_w_gate_up, ((0, 0), (0, 128 - gate_rank), (0, 0))).astype(BF16)

    s5_mats = []
    for e in range(s5_lam_re.shape[0]):
        a_re, a_im, bb_re, bb_im = _s5_prep(s5_lam_re[e], s5_lam_im[e], s5_log_step[e],
                                            s5_b_re[e], s5_b_im[e])
        s5_mats.append(_s5_matrices(a_re, a_im, bb_re, bb_im, s5_c_re[e], s5_c_im[e], s5_d[e],
                                    s5_w_glu[e], s5_b_glu[e]))

    p = dict(norm_mix=norm_mix, norm_cross=norm_cross, norm_ffn=norm_ffn, norm_final=norm_final,
             w_in_ab=w_in_ab.astype(BF16), hgrn_lb=hgrn_lb, hgrn_gnorm=hgrn_gnorm, s5_mats=s5_mats,
             w_out_ab=w_out_ab.astype(BF16), w_in_c=w_in_c_p, gla_w_gate=gla_w_gate,
             gla_b_gate=gla_b_gate, gla_gnorm=gla_gnorm, w_out_c=w_out_c.astype(BF16),
             xa_w_q=xa_w_q.astype(BF16), xa_w_o=xa_w_o.astype(BF16),
             ffn_w_up=ffn_w_up.astype(BF16), ffn_conv_w=ffn_conv_w, ffn_conv_b=ffn_conv_b,
             ffn_w_down=ffn_w_down.astype(BF16))

    mem_k_p, mem_v_p = _mem_kv(mem_prompt, norm_mem.reshape(depth, 1, d), xa_w_kv.astype(BF16))
    y_prompt, st_p = _trunk(x_prompt, mem_k_p, mem_v_p, None, p, prompt=True)

    states = dict(hgrn=state_hgrn, s5_re=state_s5_re, s5_im=state_s5_im, gla=state_gla,
                  conv=state_ffn_conv)
    y_sample, st_s = _trunk(x_sample, cache_mem_k, cache_mem_v, states, p, prompt=False)

    stack = lambda xs: jnp.stack(xs)
    return (y_prompt, y_sample,
            stack(st_p["hgrn"]), stack(st_p["s5_re"]), stack(st_p["s5_im"]), stack(st_p["gla"]),
            mem_k_p, mem_v_p, stack(st_p["conv"]),
            stack(st_s["hgrn"]), stack(st_s["s5_re"]), stack(st_s["s5_im"]), stack(st_s["gla"]),
            stack(st_s["conv"]))
```

```python
import functools
import math

import jax
import jax.numpy as jnp
from jax import lax
from jax.experimental import pallas as pl
from jax.experimental.pallas import tpu as pltpu

F32 = jnp.float32
BF16 = jnp.bfloat16

EPS = 1e-6
S5_MAX_RE = -1e-4
GLA_GATE_TAU = 16.0
N_HEADS = 4
HEAD_DK = 128
S5_GROUP = 16
S5_STATE = 64
SUB_BLOCK = 16
GLA_SAFE_DECAY = 64.0
VMEM_LIMIT = 56 * 1024 * 1024

_NT = (((1,), (1,)), ((), ()))
_TN = (((0,), (0,)), ((), ()))


def _cparams(*sem):
    return pltpu.CompilerParams(dimension_semantics=sem, vmem_limit_bytes=VMEM_LIMIT)


def _rms(x, g):
    return x * lax.rsqrt(jnp.mean(x * x, axis=-1, keepdims=True) + EPS) * g


def _sigmoid(x):
    return 1.0 / (1.0 + jnp.exp(-x))


def _silu(x):
    return x * _sigmoid(x)


def _row_tile(rows, want):
    t = min(rows, want)
    assert rows % t == 0, (rows, t)
    return t


def _norm_proj_kernel(x_ref, g_ref, w_ref, *o_refs, widths, has_norm):
    x = x_ref[...]
    h = (_rms(x, g_ref[...]) if has_norm else x).astype(BF16)
    col = 0
    for o_ref, wd in zip(o_refs, widths):
        for c0 in range(0, wd, 512):
            cw = min(512, wd - c0)
            o_ref[:, c0:c0 + cw] = jnp.dot(
                h, w_ref[:, col + c0:col + c0 + cw],
                preferred_element_type=F32).astype(o_ref.dtype)
        col += wd


def _norm_proj(x, g, w, widths, out_dtypes, *, has_norm=True, tm=512):
    m, k = x.shape
    nl, _, n = w.shape
    assert n == sum(widths)
    tm = _row_tile(m, tm)
    return pl.pallas_call(
        functools.partial(_norm_proj_kernel, widths=tuple(widths), has_norm=has_norm),
        grid=(nl, m // tm),
        in_specs=[
            pl.BlockSpec((tm, k), lambda l, i: (i, 0)),
            pl.BlockSpec((None, 1, k), lambda l, i: (l, 0, 0)),
            pl.BlockSpec((None, k, n), lambda l, i: (l, 0, 0)),
        ],
        out_specs=[pl.BlockSpec((None, tm, wd), lambda l, i: (l, i, 0)) for wd in widths],
        out_shape=[jax.ShapeDtypeStruct((nl, m, wd), dt) for wd, dt in zip(widths, out_dtypes)],
        compiler_params=_cparams("arbitrary", "arbitrary"),
        name="norm_proj",
    )(x, g, w)


def _mem_kv_kernel(x_ref, g_ref, w_ref, k_ref, v_ref):
    h = _rms(x_ref[...], g_ref[...]).astype(BF16)
    d = x_ref.shape[-1]
    hd = d // N_HEADS
    for o_ref, base in ((k_ref, 0), (v_ref, d)):
        for hh in range(N_HEADS):
            o_ref[:, hh, :] = jnp.dot(h, w_ref[:, base + hh * hd:base + (hh + 1) * hd],
                                      preferred_element_type=F32)


def _mem_kv(mem, g, w):
    bsz, n_mem, d = mem.shape
    depth = w.shape[0]
    out_spec = pl.BlockSpec((None, None, n_mem, N_HEADS, d // N_HEADS), lambda l, b: (l, b, 0, 0, 0))
    out_shape = jax.ShapeDtypeStruct((depth, bsz, n_mem, N_HEADS, d // N_HEADS), F32)
    return pl.pallas_call(
        _mem_kv_kernel,
        grid=(depth, bsz),
        in_specs=[pl.BlockSpec((None, n_mem, d), lambda l, b: (b, 0, 0)),
                  pl.BlockSpec((None, 1, d), lambda l, b: (l, 0, 0)),
                  pl.BlockSpec((None, d, 2 * d), lambda l, b: (l, 0, 0))],
        out_specs=[out_spec, out_spec],
        out_shape=[out_shape, out_shape],
        compiler_params=_cparams("arbitrary", "arbitrary"),
        name="mem_kv",
    )(mem, g, w)


def _proj_res_kernel(res_ref, *refs, n_in):
    acc = res_ref[...]
    for a_ref, w_ref in zip(refs[:n_in], refs[n_in:2 * n_in]):
        acc = acc + jnp.dot(a_ref[...].astype(BF16), w_ref[...], preferred_element_type=F32)
    refs[2 * n_in][...] = acc


def _proj_res(res, a_list, w_list, *, tm=512):
    m, n = res.shape
    tm = _row_tile(m, tm)
    n_in = len(a_list)
    in_specs = [pl.BlockSpec((tm, n), lambda i: (i, 0))]
    in_specs += [pl.BlockSpec((tm, a.shape[1]), lambda i: (i, 0)) for a in a_list]
    in_specs += [pl.BlockSpec(w.shape, lambda i: (0, 0)) for w in w_list]
    return pl.pallas_call(
        functools.partial(_proj_res_kernel, n_in=n_in),
        grid=(m // tm,),
        in_specs=in_specs,
        out_specs=pl.BlockSpec((tm, n), lambda i: (i, 0)),
        out_shape=jax.ShapeDtypeStruct((m, n), F32),
        compiler_params=_cparams("arbitrary"),
        name="proj_res",
    )(res, *a_list, *w_list)


def _cumsum_rows(x, c):
    n = x.shape[0]
    hi = x.astype(BF16)
    rest = x - hi.astype(F32)
    mid = rest.astype(BF16)
    lo = (rest - mid.astype(F32)).astype(BF16)
    r = lax.broadcasted_iota(jnp.int32, (n, n), 0)
    col = lax.broadcasted_iota(jnp.int32, (n, n), 1)
    tri = jnp.where((r >= col) & (r // c == col // c), 1.0, 0.0).astype(BF16)
    return jnp.dot(jnp.concatenate([tri, tri, tri], axis=1), jnp.concatenate([hi, mid, lo], axis=0),
                   preferred_element_type=F32)


def _gla_head(qh, kh, bh, vh, st_ref, small_decay):
    c = qh.shape[0]
    sb = min(SUB_BLOCK, c)
    vb = vh.astype(BF16)
    st = st_ref[...]
    b_last = bh[c - 1:c, :]
    q_in = (qh * jnp.exp(bh)).astype(BF16)
    o_inter = lax.dot_general(q_in, st.astype(BF16), _NT, preferred_element_type=F32)
    if small_decay:
        k_up = kh * jnp.exp(-bh)
        a = lax.dot_general(q_in, k_up.astype(BF16), _NT, preferred_element_type=F32)
        causal = (lax.broadcasted_iota(jnp.int32, (c, c), 0) >= lax.broadcasted_iota(jnp.int32, (c, c), 1))
        kd = (k_up * jnp.exp(b_last)).astype(BF16)
        st_ref[...] = st * jnp.exp(b_last) + lax.dot_general(vb, kd, _TN, preferred_element_type=F32)
        return o_inter + jnp.dot(jnp.where(causal, a, 0.0).astype(BF16), vb, preferred_element_type=F32)
    kd = (kh * jnp.exp(b_last - bh)).astype(BF16)
    st_ref[...] = st * jnp.exp(b_last) + lax.dot_general(vb, kd, _TN, preferred_element_type=F32)
    rows = lax.broadcasted_iota(jnp.int32, (sb, 1), 0)
    parts = []
    for s in range(c // sb):
        r0 = s * sb
        qs, ks, bs, vs = qh[r0:r0 + sb], kh[r0:r0 + sb], bh[r0:r0 + sb], vh[r0:r0 + sb]
        acc = o_inter[r0:r0 + sb]
        if s > 0:
            ref_b = bh[r0 - 1:r0, :]
            qf = (qs * jnp.exp(bs - ref_b)).astype(BF16)
            kf = (kh[0:r0] * jnp.exp(ref_b - bh[0:r0])).astype(BF16)
            a_off = lax.dot_general(qf, kf, _NT, preferred_element_type=F32)
            acc = acc + jnp.dot(a_off.astype(BF16), vb[0:r0], preferred_element_type=F32)
        for j in range(sb):
            w = jnp.exp(jnp.minimum(bs - bs[j:j + 1], 0.0)) * qs * ks[j:j + 1]
            col = jnp.where(rows >= j, jnp.sum(w, axis=-1, keepdims=True), 0.0)
            acc = acc + col * vs[j:j + 1]
        parts.append(acc)
    return parts[0] if len(parts) == 1 else jnp.concatenate(parts, axis=0)


def _gla_kernel(*refs, mode, layer, dv, nb, rows_in, chunk, n_chunks, has_s0):
    n_in = (6 if mode == "hgrn" else 8) + (1 if has_s0 else 0)
    ins, (o_ref, sout_ref), scr = refs[:n_in], refs[n_in:n_in + 2], refs[n_in + 2:]
    st_ref = scr[0]
    pad_refs = scr[1:]
    n_act = 4 if mode == "hgrn" else 5
    t_idx = pl.program_id(1)
    padded = rows_in < chunk
    assert padded or nb == 1

    @pl.when(t_idx == 0)
    def _():
        for i in range(nb):
            for h in range(N_HEADS):
                if has_s0:
                    st_ref[i * N_HEADS + h] = ins[-1][i, h].T
                else:
                    st_ref[i * N_HEADS + h] = jnp.zeros(st_ref.shape[1:], F32)

    if padded:
        for p_ref, a_ref in zip(pad_refs, ins[:n_act]):
            p_ref[...] = jnp.zeros(p_ref.shape, F32)
            for i in range(nb):
                p_ref[i * chunk:i * chunk + rows_in, :] = a_ref[i]
        acts = pad_refs
    else:
        acts = [a.at[0] for a in ins[:n_act]]
    groups = nb if padded else n_chunks
    span = groups * chunk

    ld = [a[...] for a in acts]
    if mode == "hgrn":
        q_raw, f, v, gate = ld
        lb_ref, gn_ref = ins[4], ins[5]
        lbv = lb_ref[...]
        e = jnp.exp(lbv - jnp.max(lbv, axis=0, keepdims=True))
        lb = jnp.sum(e[0:layer + 1], axis=0, keepdims=True) / jnp.sum(e, axis=0, keepdims=True)
        forget = lb + (1.0 - lb) * _sigmoid(f)
        k = 1.0 - forget
        lg = jnp.log(forget)
        q = _silu(q_raw)
    else:
        q_raw, k, v, gate, gd = ld
        wg_ref, bg_ref, gn_ref = ins[5], ins[6], ins[7]
        z = jnp.dot(gd.astype(BF16), wg_ref[...], preferred_element_type=F32) + bg_ref[...]
        lg = (jnp.minimum(z, 0.0) - jnp.log(1.0 + jnp.exp(-jnp.abs(z)))) / GLA_GATE_TAU
        q = q_raw * (HEAD_DK ** -0.5)
    if padded:
        live = lax.broadcasted_iota(jnp.int32, (span, 1), 0) % chunk < rows_in
        lg = jnp.where(live, lg, 0.0)
        k = jnp.where(live, k, 0.0)
    b = _cumsum_rows(lg, chunk)

    def piece(x, i, h, width):
        return x[i * chunk:(i + 1) * chunk, h * width:(h + 1) * width]

    def heads(small_decay):
        rows = []
        for i in range(groups):
            cols = []
            for h in range(N_HEADS):
                cols.append(_gla_head(piece(q, i, h, HEAD_DK), piece(k, i, h, HEAD_DK),
                                      piece(b, i, h, HEAD_DK), piece(v, i, h, dv),
                                      st_ref.at[(i if padded else 0) * N_HEADS + h], small_decay))
            rows.append(jnp.concatenate(cols, axis=1))
        return rows[0] if groups == 1 else jnp.concatenate(rows, axis=0)

    o_raw = lax.cond(jnp.min(b) >= -GLA_SAFE_DECAY,
                     functools.partial(heads, True), functools.partial(heads, False))
    o_all = jnp.concatenate(
        [_rms(o_raw[:, h * dv:(h + 1) * dv], gn_ref[...]) * _silu(gate[:, h * dv:(h + 1) * dv])
         for h in range(N_HEADS)], axis=1).astype(o_ref.dtype)
    if padded:
        for i in range(nb):
            o_ref[i] = o_all[i * chunk:i * chunk + rows_in]
    else:
        o_ref[0] = o_all

    @pl.when(t_idx == pl.num_programs(1) - 1)
    def _():
        for i in range(nb):
            for h in range(N_HEADS):
                sout_ref[i, h] = st_ref[i * N_HEADS + h].T


def _gla_call(mode, proj, params, s0, *, layer, dv, tb, chunk, nb):
    bsz, seq, _ = proj.shape
    rows_in = min(tb, seq)
    if rows_in < chunk:
        assert seq == rows_in and bsz % nb == 0
        nt, n_chunks = 1, 1
    else:
        assert seq % tb == 0 and tb % chunk == 0 and nb == 1
        nt, n_chunks = seq // tb, tb // chunk
    kw, vw = N_HEADS * HEAD_DK, N_HEADS * dv

    def act(width, col_block):
        return pl.BlockSpec((nb, rows_in, width), lambda b, t: (b, t, col_block))

    def whole(a):
        return pl.BlockSpec(a.shape, lambda b, t: (0,) * a.ndim)

    if mode == "hgrn":
        act_specs = [act(kw, 0), act(kw, 1), act(vw, 2), act(vw, 3)]
        act_widths = [kw, kw, vw, vw]
    else:
        act_specs = [act(kw, 0), act(kw, 1), act(vw, kw * 2 // vw), act(vw, kw * 2 // vw + 1),
                     act(128, (2 * kw + 2 * vw) // 128)]
        act_widths = [kw, kw, vw, vw, 128]
    in_specs = act_specs + [whole(p) for p in params]
    args = [proj] * len(act_specs) + list(params)
    state_spec = pl.BlockSpec((nb, N_HEADS, HEAD_DK, dv), lambda b, t: (b, 0, 0, 0))
    if s0 is not None:
        in_specs.append(state_spec)
        args.append(s0)
    scratch = [pltpu.VMEM((nb * N_HEADS, dv, HEAD_DK), F32)]
    if rows_in < chunk:
        scratch += [pltpu.VMEM((nb * chunk, w), F32) for w in act_widths]
    return pl.pallas_call(
        functools.partial(_gla_kernel, mode=mode, layer=layer, dv=dv, nb=nb, rows_in=rows_in,
                          chunk=chunk, n_chunks=n_chunks, has_s0=s0 is not None),
        grid=(bsz // nb, nt),
        in_specs=in_specs,
        out_specs=[pl.BlockSpec((nb, rows_in, vw), lambda b, t: (b, t, 0)), state_spec],
        out_shape=[jax.ShapeDtypeStruct((bsz, seq, vw), BF16),
                   jax.ShapeDtypeStruct((bsz, N_HEADS, HEAD_DK, dv), F32)],
        scratch_shapes=scratch,
        compiler_params=_cparams("arbitrary", "arbitrary"),
        name="gla_" + mode,
    )(*args)


def _s5_prep_kernel(lre_ref, lim_ref, ls_ref, lre_x_ref, lim_x_ref, ls_x_ref, bre_ref, bim_ref,
                    are_ref, aim_ref, bbre_ref, bbim_ref):
    def disc(lre, lim, ls):
        lr = jnp.minimum(lre, S5_MAX_RE)
        dt = jnp.exp(ls)
        mag = jnp.exp(lr * dt)
        a_re = mag * jnp.cos(lim * dt)
        a_im = mag * jnp.sin(lim * dt)
        den = lr * lr + lim * lim
        z_re = ((a_re - 1.0) * lr + a_im * lim) / den
        z_im = (a_im * lr - (a_re - 1.0) * lim) / den
        return a_re, a_im, z_re, z_im

    a_re, a_im, _, _ = disc(lre_ref[...], lim_ref[...], ls_ref[...])
    are_ref[...] = a_re
    aim_ref[...] = a_im
    _, _, z_re, z_im = disc(lre_x_ref[...], lim_x_ref[...], ls_x_ref[...])
    bbre_ref[...] = z_re * bre_ref[...] - z_im * bim_ref[...]
    bbim_ref[...] = z_re * bim_ref[...] + z_im * bre_ref[...]


def _s5_prep(lam_re, lam_im, log_step, b_re, b_im):
    g, n = lam_re.shape
    p = b_re.shape[-1]
    ls = jnp.broadcast_to(log_step[:, None], (g, n))
    rep = lambda a: jnp.repeat(a, p, axis=1)
    outs = pl.pallas_call(
        _s5_prep_kernel,
        out_shape=[jax.ShapeDtypeStruct((g, n), F32)] * 2 + [jax.ShapeDtypeStruct((g, n * p), F32)] * 2,
        name="s5_prep",
    )(lam_re, lam_im, ls, rep(lam_re), rep(lam_im), rep(ls),
      b_re.reshape(g, n * p), b_im.reshape(g, n * p))
    a_re, a_im, bb_re, bb_im = outs
    return a_re, a_im, bb_re.reshape(g, n, p), bb_im.reshape(g, n, p)


def _s5_kernel(*refs, bg, tc, nt, cw, has_x0):
    n_in = 9 if has_x0 else 7
    u_ref, bm_ref, cm_ref, a_ref, d_ref, wg_ref, bgl_ref = refs[:7]
    o_ref, sre_ref, sim_ref = refs[n_in:n_in + 3]
    utm_ref, xs_ref, st_ref = refs[n_in + 3:]
    t_idx = pl.program_id(1)
    half = xs_ref.shape[1] // 2
    uw = u_ref.shape[-1] // 2

    @pl.when(t_idx == 0)
    def _():
        for hf in range(2):
            if has_x0:
                st_ref[:, hf * 2 * half:hf * 2 * half + half] = refs[7][:, hf * half:(hf + 1) * half]
                st_ref[:, hf * 2 * half + half:(hf + 1) * 2 * half] = refs[8][:, hf * half:(hf + 1) * half]
            else:
                st_ref[...] = jnp.zeros(st_ref.shape, F32)

    for t in range(tc):
        utm_ref[t * bg:(t + 1) * bg, :] = u_ref[:, t, :]
    u = utm_ref[...]
    ys = []
    for hf in range(2):
        xs_ref[...] = jnp.dot(u[:, hf * uw:(hf + 1) * uw].astype(BF16), bm_ref[hf],
                              preferred_element_type=F32)
        base = hf * 2 * half
        for c0 in range(0, half, cw):
            ar = a_ref[0:1, base + c0:base + c0 + cw]
            ai = a_ref[0:1, base + half + c0:base + half + c0 + cw]
            xr = st_ref[:, base + c0:base + c0 + cw]
            xi = st_ref[:, base + half + c0:base + half + c0 + cw]
            for t in range(tc):
                rows = slice(t * bg, (t + 1) * bg)
                nr = ar * xr - ai * xi + xs_ref[rows, c0:c0 + cw]
                ni = ar * xi + ai * xr + xs_ref[rows, half + c0:half + c0 + cw]
                xs_ref[rows, c0:c0 + cw] = nr
                xs_ref[rows, half + c0:half + c0 + cw] = ni
                xr, xi = nr, ni
            st_ref[:, base + c0:base + c0 + cw] = xr
            st_ref[:, base + half + c0:base + half + c0 + cw] = xi
        ys.append(jnp.dot(xs_ref[...].astype(BF16), cm_ref[hf], preferred_element_type=F32))
    y = jnp.concatenate(ys, axis=1) + d_ref[...] * u
    y = 0.5 * y * (1.0 + jnp.tanh(math.sqrt(2.0 / math.pi) * (y + 0.044715 * (y * y * y))))
    gate = jnp.dot(y.astype(BF16), wg_ref[...], preferred_element_type=F32) + bgl_ref[...]
    utm_ref[...] = y * _sigmoid(gate)
    for t in range(tc):
        o_ref[:, t, :] = utm_ref[t * bg:(t + 1) * bg, :]

    @pl.when(t_idx == nt - 1)
    def _():
        for hf in range(2):
            sre_ref[:, hf * half:(hf + 1) * half] = st_ref[:, hf * 2 * half:hf * 2 * half + half]
            sim_ref[:, hf * half:(hf + 1) * half] = st_ref[:, hf * 2 * half + half:(hf + 1) * 2 * half]


def _s5_call(u_src, col_block, mats, x0, *, tc, cw):
    bg, seq, _ = u_src.shape
    assert seq % tc == 0
    nt = seq // tc
    bmat, cmat, a_flat, d_row, w_glu, b_glu = mats
    width = d_row.shape[1]
    nstate = a_flat.shape[1] // 2
    whole = lambda a: pl.BlockSpec(a.shape, lambda g, t: (0,) * a.ndim)
    in_specs = [pl.BlockSpec((bg, tc, width), lambda g, t: (0, t, col_block))]
    in_specs += [whole(m) for m in mats]
    args = [u_src] + list(mats)
    st_spec = pl.BlockSpec((bg, nstate), lambda g, t: (0, 0))
    if x0 is not None:
        in_specs += [st_spec, st_spec]
        args += list(x0)
    return pl.pallas_call(
        functools.partial(_s5_kernel, bg=bg, tc=tc, nt=nt, cw=cw, has_x0=x0 is not None),
        grid=(1, nt),
        in_specs=in_specs,
        out_specs=[pl.BlockSpec((bg, tc, width), lambda g, t: (0, t, 0)), st_spec, st_spec],
        out_shape=[jax.ShapeDtypeStruct((bg, seq, width), F32),
                   jax.ShapeDtypeStruct((bg, nstate), F32),
                   jax.ShapeDtypeStruct((bg, nstate), F32)],
        scratch_shapes=[pltpu.VMEM((bg * tc, width), F32), pltpu.VMEM((bg * tc, nstate), F32),
                        pltpu.VMEM((bg, 2 * nstate), F32)],
        compiler_params=_cparams("arbitrary", "arbitrary"),
        name="s5",
    )(*args)


def _s5_matrices(a_re, a_im, bb_re, bb_im, c_re, c_im, d, w_glu, b_glu):
    g, n, p = bb_re.shape
    gh = g // 2
    eye = jnp.eye(gh, dtype=F32)

    def block_diag(t):
        return (eye[:, None, :, None] * t[:, :, None, :]).reshape(gh * t.shape[1], gh * t.shape[2])

    def in_mat(bb):
        return block_diag(bb.transpose(0, 2, 1))

    def out_mat(cc):
        return block_diag(cc.transpose(0, 2, 1))

    bmat = jnp.stack([jnp.concatenate([in_mat(bb_re[h * gh:(h + 1) * gh]),
                                       in_mat(bb_im[h * gh:(h + 1) * gh])], axis=1)
                      for h in range(2)]).astype(BF16)
    cmat = jnp.stack([jnp.concatenate([out_mat(c_re[h * gh:(h + 1) * gh]),
                                       out_mat(-c_im[h * gh:(h + 1) * gh])], axis=0)
                      for h in range(2)]).astype(BF16)
    a_flat = jnp.concatenate([jnp.concatenate([a_re[h * gh:(h + 1) * gh].reshape(1, gh * n),
                                               a_im[h * gh:(h + 1) * gh].reshape(1, gh * n)], axis=1)
                              for h in range(2)], axis=1)
    return (bmat, cmat, a_flat, d.reshape(1, g * p), w_glu.astype(BF16), b_glu.reshape(1, -1))


def _xattn_kernel(*refs, n_pre):
    x_ref = refs[0]
    g_ref, wq_ref, wo_ref, k_ref, v_ref, y_ref, kt_ref, vt_ref = refs[1 + 2 * n_pre:]

    @pl.when(pl.program_id(1) == 0)
    def _():
        kt_ref[...] = jnp.transpose(k_ref[...], (1, 0, 2)).astype(BF16)
        vt_ref[...] = jnp.transpose(v_ref[...], (1, 0, 2)).astype(BF16)

    x = x_ref[...]
    for a_ref, w_ref in zip(refs[1:1 + n_pre], refs[1 + n_pre:1 + 2 * n_pre]):
        x = x + jnp.dot(a_ref[...].astype(BF16), w_ref[...], preferred_element_type=F32)
    q = jnp.dot(_rms(x, g_ref[...]).astype(BF16), wq_ref[...], preferred_element_type=F32).astype(BF16)
    hd = q.shape[1] // N_HEADS
    outs = []
    for h in range(N_HEADS):
        s = lax.dot_general(q[:, h * hd:(h + 1) * hd], kt_ref[h], _NT,
                            preferred_element_type=F32) * (hd ** -0.5)
        p = jnp.exp(s - jnp.max(s, axis=-1, keepdims=True))
        p = p / jnp.sum(p, axis=-1, keepdims=True)
        outs.append(jnp.dot(p.astype(BF16), vt_ref[h], preferred_element_type=F32).astype(BF16))
    y_ref[...] = x + jnp.dot(jnp.concatenate(outs, axis=1), wo_ref[...], preferred_element_type=F32)


def _xattn_call(x, pre_a, pre_w, g, w_q, w_o, mem_k, mem_v, layer, *, tq):
    bsz, seq, d = x.shape
    n_mem, nh, hd = mem_k.shape[2:]
    assert seq % tq == 0
    kv_spec = pl.BlockSpec((None, None, n_mem, nh, hd), lambda b, t: (layer, b, 0, 0, 0))
    whole = lambda a: pl.BlockSpec(a.shape, lambda b, t: (0,) * a.ndim)
    rows = lambda a: pl.BlockSpec((None, tq, a.shape[-1]), lambda b, t: (b, t, 0))
    return pl.pallas_call(
        functools.partial(_xattn_kernel, n_pre=len(pre_a)),
        grid=(bsz, seq // tq),
        in_specs=[rows(x)] + [rows(a) for a in pre_a] + [whole(w) for w in pre_w]
        + [whole(g), whole(w_q), whole(w_o), kv_spec, kv_spec],
        out_specs=rows(x),
        out_shape=jax.ShapeDtypeStruct((bsz, seq, d), F32),
        scratch_shapes=[pltpu.VMEM((nh, n_mem, hd), BF16), pltpu.VMEM((nh, n_mem, hd), BF16)],
        compiler_params=_cparams("arbitrary", "arbitrary"),
        name="xattn",
    )(x, *pre_a, *pre_w, g, w_q, w_o, mem_k, mem_v)


def _attn_rows_kernel(q_ref, k_ref, v_ref, o_ref, pad_ref, *, rows_in):
    nb, rows, d = pad_ref.shape
    nblk = d // 128
    half_blk = nblk // 2
    lanes = k_ref.shape[1]
    lane_blk = lax.broadcasted_iota(jnp.int32, (1, lanes), 1) % nblk
    row_head = lax.broadcasted_iota(jnp.int32, (N_HEADS * rows, 1), 0) // rows
    live = lane_blk == row_head
    scale = (d // N_HEADS) ** -0.5
    for i in range(nb):
        pad_ref[i] = jnp.zeros((rows, d), F32)
        pad_ref[i, 0:rows_in, :] = q_ref[i]
        q = pad_ref[i]
        qx = jnp.concatenate([q[:, j * 128:(j + 1) * 128] for j in range(nblk)], axis=0)
        g = lax.dot_general(qx.astype(BF16), k_ref[i].astype(BF16), _NT, preferred_element_type=F32)
        s = jnp.concatenate(
            [g[2 * h * rows:(2 * h + 1) * rows]
             + pltpu.roll(g[(2 * h + 1) * rows:(2 * h + 2) * rows], lanes - half_blk, axis=1)
             for h in range(N_HEADS)], axis=0) * scale
        s = jnp.where(live, s, -1e30)
        e = jnp.exp(s - jnp.max(s, axis=-1, keepdims=True))
        p = e / jnp.sum(e, axis=-1, keepdims=True)
        px = jnp.concatenate(
            [blk for h in range(N_HEADS)
             for blk in (p[h * rows:(h + 1) * rows], pltpu.roll(p[h * rows:(h + 1) * rows], half_blk, axis=1))],
            axis=0)
        o = jnp.dot(px.astype(BF16), v_ref[i].astype(BF16), preferred_element_type=F32)
        o = jnp.concatenate([o[j * rows:(j + 1) * rows] for j in range(nblk)], axis=1)
        o_ref[i] = o[0:rows_in].astype(o_ref.dtype)


def _attn_rows_call(q, mem_k, mem_v, layer, *, nb):
    bsz, seq, d = q.shape
    depth, _, n_mem, nh, hd = mem_k.shape
    assert bsz % nb == 0 and hd == 256 and nh == N_HEADS and seq <= 16
    as_rows = lambda a: a.reshape(depth, bsz, n_mem, nh, 2, 128).transpose(0, 1, 2, 4, 3, 5).reshape(
        depth, bsz, n_mem * 2 * nh, 128)
    kv_spec = pl.BlockSpec((None, nb, n_mem * 2 * nh, 128), lambda b: (layer, b, 0, 0))
    return pl.pallas_call(
        functools.partial(_attn_rows_kernel, rows_in=seq),
        grid=(bsz // nb,),
        in_specs=[pl.BlockSpec((nb, seq, d), lambda b: (b, 0, 0)), kv_spec, kv_spec],
        out_specs=pl.BlockSpec((nb, seq, d), lambda b: (b, 0, 0)),
        out_shape=jax.ShapeDtypeStruct((bsz, seq, d), BF16),
        scratch_shapes=[pltpu.VMEM((nb, 16, d), F32)],
        compiler_params=_cparams("arbitrary"),
        name="mem_attn_rows",
    )(q, as_rows(mem_k), as_rows(mem_v))


FFN_COLS = 256


def _ffn_kernel(*refs, tm, ts, hs, f_dim, has_hist, has_final):
    n_in = 6 + int(has_hist) + int(has_final)
    x_ref, g_ref, wup_ref, cw_ref, cb_ref, wdn_ref = refs[:6]
    y_ref, state_ref = refs[n_in:n_in + 2]
    gated_ref, hist_ref = refs[n_in + 2:n_in + 4]
    t_idx = pl.program_id(1)

    @pl.when(t_idx == 0)
    def _():
        hist_ref[...] = jnp.zeros(hist_ref.shape, F32)
        if has_hist:
            hist_ref[hs - 2 * ts:hs, :] = refs[6][...]

    x = x_ref[...]
    h = _rms(x, g_ref[...]).astype(BF16)
    row = lax.broadcasted_iota(jnp.int32, (tm, 1), 0)
    for c in range(f_dim // FFN_COLS):
        conv = []
        for part in range(2):
            cols = slice(part * f_dim + c * FFN_COLS, part * f_dim + (c + 1) * FFN_COLS)
            u = jnp.dot(h, wup_ref[:, cols], preferred_element_type=F32)
            if ts == 1:
                prev2, prev1 = hist_ref[hs - 2:hs - 1, cols], hist_ref[hs - 1:hs, cols]
                m1 = jnp.where(row == 0, prev1, pltpu.roll(u, 1, axis=0))
                m2 = jnp.where(row == 0, prev2, jnp.where(row == 1, prev1, pltpu.roll(u, 2, axis=0)))
            else:
                ext = jnp.concatenate([hist_ref[hs - 2 * ts:hs, cols], u], axis=0)
                m2, m1 = ext[0:tm], ext[ts:ts + tm]
            conv.append(cb_ref[:, cols] + cw_ref[0:1, cols] * m2 + cw_ref[1:2, cols] * m1
                        + cw_ref[2:3, cols] * u)
            hist_ref[:, cols] = u[tm - hs:tm]
        gated_ref[:, c * FFN_COLS:(c + 1) * FFN_COLS] = (_silu(conv[0]) * conv[1]).astype(BF16)
    out = x + jnp.dot(gated_ref[...], wdn_ref[...], preferred_element_type=F32)
    if has_final:
        out = _rms(out, refs[n_in - 1][...])
    y_ref[...] = out

    @pl.when(t_idx == pl.num_programs(1) - 1)
    def _():
        state_ref[...] = hist_ref[hs - 2 * ts:hs, :]


def _ffn_call(x, g, w_up, conv_w, conv_b, w_down, hist0, g_final, *, tm, ts):
    ngrp, rows, d = x.shape
    f2 = w_up.shape[1]
    f_dim = f2 // 2
    hs = max(8, 2 * ts)
    assert rows % tm == 0 and tm >= hs and f_dim % FFN_COLS == 0 and (ts == 1 or ts % 8 == 0)
    whole = lambda a: pl.BlockSpec(a.shape, lambda s, t: (0,) * a.ndim)
    args = [x, g, w_up, conv_w, conv_b, w_down]
    in_specs = [pl.BlockSpec((None, tm, d), lambda s, t: (s, t, 0))] + [whole(a) for a in args[1:]]
    st_spec = pl.BlockSpec((None, 2 * ts, f2), lambda s, t: (s, 0, 0))
    if hist0 is not None:
        in_specs.append(st_spec)
        args.append(hist0)
    if g_final is not None:
        in_specs.append(whole(g_final))
        args.append(g_final)
    return pl.pallas_call(
        functools.partial(_ffn_kernel, tm=tm, ts=ts, hs=hs, f_dim=f_dim,
                          has_hist=hist0 is not None, has_final=g_final is not None),
        grid=(ngrp, rows // tm),
        in_specs=in_specs,
        out_specs=[pl.BlockSpec((None, tm, d), lambda s, t: (s, t, 0)), st_spec],
        out_shape=[jax.ShapeDtypeStruct((ngrp, rows, d), F32),
                   jax.ShapeDtypeStruct((ngrp, 2 * ts, f2), F32)],
        scratch_shapes=[pltpu.VMEM((tm, f_dim), BF16), pltpu.VMEM((hs, f2), F32)],
        compiler_params=_cparams("arbitrary", "arbitrary"),
        name="conv_ffn",
    )(*args)


def _trunk(x, mem_k, mem_v, states, p, *, prompt):
    bsz, seq, d = x.shape
    depth = p["norm_mix"].shape[0]
    m = bsz * seq
    x2 = x.reshape(m, d)
    new = {"hgrn": [], "s5_re": [], "s5_im": [], "gla": [], "conv": []}
    if prompt:
        gla_tiles = dict(tb=256, chunk=64, nb=1)
        attn_tq = 512
    else:
        gla_tiles = dict(tb=seq, chunk=16, nb=8)
        attn_tq = seq
    for l in range(depth):
        g_mix = p["norm_mix"][l].reshape(1, 1, d)
        if l % 2 == 0:
            e = l // 2
            w_in = p["w_in_ab"][e]
            proj = _norm_proj(x2, g_mix, w_in[None], [w_in.shape[1]], [F32])[0][0]
            kw = N_HEADS * HEAD_DK
            o_a, s_a = _gla_call(
                "hgrn", proj.reshape(bsz, seq, -1),
                [p["hgrn_lb"], p["hgrn_gnorm"][e].reshape(1, -1)],
                None if states is None else states["hgrn"][e],
                layer=l, dv=kw // N_HEADS, **gla_tiles)
            mats = p["s5_mats"][e]
            if prompt:
                o_b, sr, si = _s5_call(proj.reshape(bsz, seq, -1), 4, mats, None, tc=64, cw=512)
            else:
                x0 = (states["s5_re"][e].reshape(bsz, -1), states["s5_im"][e].reshape(bsz, -1))
                o_b, sr, si = _s5_call(proj.reshape(bsz, seq, -1), 4, mats, x0, tc=seq, cw=128)
            w_out = p["w_out_ab"][e]
            mixed, mixed_w = [o_a, o_b], [w_out[:kw], w_out[kw:]]
            new["hgrn"].append(s_a)
            new["s5_re"].append(sr.reshape(bsz, -1, S5_STATE))
            new["s5_im"].append(si.reshape(bsz, -1, S5_STATE))
        else:
            o_idx = l // 2
            w_in = p["w_in_c"][o_idx]
            proj = _norm_proj(x2, g_mix, w_in[None], [w_in.shape[1]], [F32])[0][0]
            o_c, s_c = _gla_call(
                "gla", proj.reshape(bsz, seq, -1),
                [p["gla_w_gate"][o_idx], p["gla_b_gate"][o_idx].reshape(1, -1),
                 p["gla_gnorm"][o_idx].reshape(1, -1)],
                None if states is None else states["gla"][o_idx],
                layer=l, dv=d // N_HEADS, **gla_tiles)
            mixed, mixed_w = [o_c], [p["w_out_c"][o_idx]]
            new["gla"].append(s_c)
        if prompt:
            x2 = _xattn_call(x2.reshape(bsz, seq, d), mixed, mixed_w, p["norm_cross"][l].reshape(1, d),
                             p["xa_w_q"][l], p["xa_w_o"][l], mem_k, mem_v, l, tq=attn_tq).reshape(m, d)
        else:
            x2 = _proj_res(x2, [a.reshape(m, -1) for a in mixed], mixed_w)
            q = _norm_proj(x2, p["norm_cross"][l].reshape(1, 1, d), p["xa_w_q"][l][None], [d], [F32])[0][0]
            o_x = _attn_rows_call(q.reshape(bsz, seq, d), mem_k, mem_v, l, nb=4)
            x2 = _proj_res(x2, [o_x.reshape(m, d)], [p["xa_w_o"][l]])
        g_final = p["norm_final"].reshape(1, d) if l == depth - 1 else None
        ffn_w = (p["norm_ffn"][l].reshape(1, d), p["ffn_w_up"][l], p["ffn_conv_w"][l],
                 p["ffn_conv_b"][l].reshape(1, -1), p["ffn_w_down"][l])
        if prompt:
            y, cst = _ffn_call(x2.reshape(bsz, seq, d), *ffn_w, None, g_final, tm=512, ts=1)
            x2 = y.reshape(m, d)
        else:
            xt = x2.reshape(bsz, seq, d).transpose(1, 0, 2).reshape(1, m, d)
            hist0 = states["conv"][l].transpose(1, 0, 2).reshape(1, 2 * bsz, -1)
            y, cst = _ffn_call(xt, *ffn_w, hist0, g_final, tm=m, ts=bsz)
            x2 = y.reshape(seq, bsz, d).transpose(1, 0, 2).reshape(m, d)
            cst = cst.reshape(2, bsz, -1).transpose(1, 0, 2)
        new["conv"].append(cst)
    return x2.reshape(bsz, seq, d), new


def kernel(x_prompt, x_sample, mem_prompt, cache_mem_k, cache_mem_v, state_hgrn, state_s5_re, state_s5_im, state_gla, state_ffn_conv, norm_mix, norm_cross, norm_mem, norm_ffn, norm_final, w_in_ab, hgrn_lb, hgrn_gnorm, s5_lam_re, s5_lam_im, s5_log_step, s5_b_re, s5_b_im, s5_c_re, s5_c_im, s5_d, s5_w_glu, s5_b_glu, w_out_ab, w_in_c, gla_w_gate_up, gla_b_gate, gla_gnorm, w_out_c, xa_w_q, xa_w_kv, xa_w_o, ffn_w_up, ffn_conv_w, ffn_conv_b, ffn_w_down):
    depth, d = norm_mix.shape
    n_mem = mem_prompt.shape[1]
    bsz = x_prompt.shape[0]
    dec_bsz = x_sample.shape[0]

    gla_cols = w_in_c.shape[2]
    gate_rank = gla_w_gate_up.shape[1]
    pad_c = (-gla_cols) % 128
    w_in_c_p = jnp.pad(w_in_c, ((0, 0), (0, 0), (0, pad_c))).astype(BF16)
    gla_w_gate = jnp.pad(gla_w_gate_up, ((0, 0), (0, 128 - gate_rank), (0, 0))).astype(BF16)

    s5_mats = []
    for e in range(s5_lam_re.shape[0]):
        a_re, a_im, bb_re, bb_im = _s5_prep(s5_lam_re[e], s5_lam_im[e], s5_log_step[e],
                                            s5_b_re[e], s5_b_im[e])
        s5_mats.append(_s5_matrices(a_re, a_im, bb_re, bb_im, s5_c_re[e], s5_c_im[e], s5_d[e],
                                    s5_w_glu[e], s5_b_glu[e]))

    p = dict(norm_mix=norm_mix, norm_cross=norm_cross, norm_ffn=norm_ffn, norm_final=norm_final,
             w_in_ab=w_in_ab.astype(BF16), hgrn_lb=hgrn_lb, hgrn_gnorm=hgrn_gnorm, s5_mats=s5_mats,
             w_out_ab=w_out_ab.astype(BF16), w_in_c=w_in_c_p, gla_w_gate=gla_w_gate,
             gla_b_gate=gla_b_gate, gla_gnorm=gla_gnorm, w_out_c=w_out_c.astype(BF16),
             xa_w_q=xa_w_q.astype(BF16), xa_w_o=xa_w_o.astype(BF16),
             ffn_w_up=ffn_w_up.astype(BF16), ffn_conv_w=ffn_conv_w, ffn_conv_b=ffn_conv_b,
             ffn_w_down=ffn_w_down.astype(BF16))

    mem_k_p, mem_v_p = _mem_kv(mem_prompt, norm_mem.reshape(depth, 1, d), xa_w_kv.astype(BF16))
    y_prompt, st_p = _trunk(x_prompt, mem_k_p, mem_v_p, None, p, prompt=True)

    states = dict(hgrn=state_hgrn, s5_re=state_s5_re, s5_im=state_s5_im, gla=state_gla,
                  conv=state_ffn_conv)
    y_sample, st_s = _trunk(x_sample, cache_mem_k, cache_mem_v, states, p, prompt=False)

    stack = lambda xs: jnp.stack(xs)
    return (y_prompt, y_sample,
            stack(st_p["hgrn"]), stack(st_p["s5_re"]), stack(st_p["s5_im"]), stack(st_p["gla"]),
            mem_k_p, mem_v_p, stack(st_p["conv"]),
            stack(st_s["hgrn"]), stack(st_s["s5_re"]), stack(st_s["s5_im"]), stack(st_s["gla"]),
            stack(st_s["conv"]))
```

```python
import functools
import math

import jax
import jax.numpy as jnp
from jax import lax
from jax.experimental import pallas as pl
from jax.experimental.pallas import tpu as pltpu

F32 = jnp.float32
BF16 = jnp.bfloat16

EPS = 1e-6
S5_MAX_RE = -1e-4
GLA_GATE_TAU = 16.0
N_HEADS = 4
HEAD_DK = 128
S5_GROUP = 16
S5_STATE = 64
SUB_BLOCK = 16
GLA_SAFE_DECAY = 64.0
VMEM_LIMIT = 56 * 1024 * 1024

_NT = (((1,), (1,)), ((), ()))
_TN = (((0,), (0,)), ((), ()))


def _cparams(*sem):
    return pltpu.CompilerParams(dimension_semantics=sem, vmem_limit_bytes=VMEM_LIMIT)


def _rms(x, g):
    return x * lax.rsqrt(jnp.mean(x * x, axis=-1, keepdims=True) + EPS) * g


def _sigmoid(x):
    return 1.0 / (1.0 + jnp.exp(-x))


def _silu(x):
    return x * _sigmoid(x)


def _row_tile(rows, want):
    t = min(rows, want)
    assert rows % t == 0, (rows, t)
    return t


def _pick(a, layer, rows=None, row_block=0):
    block = (None,) + ((rows,) + a.shape[2:] if rows else a.shape[1:])
    return a, pl.BlockSpec(block, lambda *_: (layer, row_block) + (0,) * (a.ndim - 2))


def _whole(a):
    return a if isinstance(a, tuple) else (a, pl.BlockSpec(a.shape, lambda *_: (0,) * a.ndim))


def _norm_proj_kernel(x_ref, g_ref, w_ref, o_ref):
    h = _rms(x_ref[...], g_ref[...]).astype(BF16)
    n = o_ref.shape[1]
    for c0 in range(0, n, 512):
        cw = min(512, n - c0)
        o_ref[:, c0:c0 + cw] = jnp.dot(h, w_ref[:, c0:c0 + cw], preferred_element_type=F32)


def _norm_proj(x, g, w, *, tm=512):
    m, k = x.shape
    (g, g_spec), (w, w_spec) = _whole(g), _whole(w)
    n = w.shape[-1]
    tm = _row_tile(m, tm)
    return pl.pallas_call(
        _norm_proj_kernel,
        grid=(m // tm,),
        in_specs=[pl.BlockSpec((tm, k), lambda i: (i, 0)), g_spec, w_spec],
        out_specs=pl.BlockSpec((tm, n), lambda i: (i, 0)),
        out_shape=jax.ShapeDtypeStruct((m, n), F32),
        compiler_params=_cparams("arbitrary"),
        name="norm_proj",
    )(x, g, w)


def _mem_kv_kernel(x_ref, g_ref, w_ref, k_ref, v_ref):
    h = _rms(x_ref[...], g_ref[...]).astype(BF16)
    d = x_ref.shape[-1]
    hd = d // N_HEADS
    for o_ref, base in ((k_ref, 0), (v_ref, d)):
        for hh in range(N_HEADS):
            o_ref[:, hh, :] = jnp.dot(h, w_ref[:, base + hh * hd:base + (hh + 1) * hd],
                                      preferred_element_type=F32)


def _mem_kv(mem, g, w):
    bsz, n_mem, d = mem.shape
    depth = w.shape[0]
    out_spec = pl.BlockSpec((None, None, n_mem, N_HEADS, d // N_HEADS), lambda l, b: (l, b, 0, 0, 0))
    out_shape = jax.ShapeDtypeStruct((depth, bsz, n_mem, N_HEADS, d // N_HEADS), F32)
    return pl.pallas_call(
        _mem_kv_kernel,
        grid=(depth, bsz),
        in_specs=[pl.BlockSpec((None, n_mem, d), lambda l, b: (b, 0, 0)),
                  pl.BlockSpec((None, 1, d), lambda l, b: (l, 0, 0)),
                  pl.BlockSpec((None, d, 2 * d), lambda l, b: (l, 0, 0))],
        out_specs=[out_spec, out_spec],
        out_shape=[out_shape, out_shape],
        compiler_params=_cparams("arbitrary", "arbitrary"),
        name="mem_kv",
    )(mem, g, w)


def _proj_res_kernel(res_ref, *refs, n_in):
    acc = res_ref[...]
    for a_ref, w_ref in zip(refs[:n_in], refs[n_in:2 * n_in]):
        acc = acc + jnp.dot(a_ref[...].astype(BF16), w_ref[...], preferred_element_type=F32)
    refs[2 * n_in][...] = acc


def _proj_res(res, a_list, w_list, *, tm=512):
    m, n = res.shape
    tm = _row_tile(m, tm)
    n_in = len(a_list)
    w_list = [_whole(w) for w in w_list]
    in_specs = [pl.BlockSpec((tm, n), lambda i: (i, 0))]
    in_specs += [pl.BlockSpec((tm, a.shape[1]), lambda i: (i, 0)) for a in a_list]
    in_specs += [spec for _, spec in w_list]
    return pl.pallas_call(
        functools.partial(_proj_res_kernel, n_in=n_in),
        grid=(m // tm,),
        in_specs=in_specs,
        out_specs=pl.BlockSpec((tm, n), lambda i: (i, 0)),
        out_shape=jax.ShapeDtypeStruct((m, n), F32),
        compiler_params=_cparams("arbitrary"),
        name="proj_res",
    )(res, *a_list, *[w for w, _ in w_list])


def _cumsum_rows(x, c):
    n = x.shape[0]
    hi = x.astype(BF16)
    rest = x - hi.astype(F32)
    mid = rest.astype(BF16)
    lo = (rest - mid.astype(F32)).astype(BF16)
    r = lax.broadcasted_iota(jnp.int32, (n, n), 0)
    col = lax.broadcasted_iota(jnp.int32, (n, n), 1)
    tri = jnp.where((r >= col) & (r // c == col // c), 1.0, 0.0).astype(BF16)
    return jnp.dot(jnp.concatenate([tri, tri, tri], axis=1), jnp.concatenate([hi, mid, lo], axis=0),
                   preferred_element_type=F32)


def _gla_head(qh, kh, bh, vh, st_ref, small_decay):
    c = qh.shape[0]
    sb = min(SUB_BLOCK, c)
    vb = vh.astype(BF16)
    st = st_ref[...]
    b_last = bh[c - 1:c, :]
    q_in = (qh * jnp.exp(bh)).astype(BF16)
    o_inter = lax.dot_general(q_in, st.astype(BF16), _NT, preferred_element_type=F32)
    if small_decay:
        k_up = kh * jnp.exp(-bh)
        a = lax.dot_general(q_in, k_up.astype(BF16), _NT, preferred_element_type=F32)
        causal = (lax.broadcasted_iota(jnp.int32, (c, c), 0) >= lax.broadcasted_iota(jnp.int32, (c, c), 1))
        kd = (k_up * jnp.exp(b_last)).astype(BF16)
        st_ref[...] = st * jnp.exp(b_last) + lax.dot_general(vb, kd, _TN, preferred_element_type=F32)
        return o_inter + jnp.dot(jnp.where(causal, a, 0.0).astype(BF16), vb, preferred_element_type=F32)
    kd = (kh * jnp.exp(b_last - bh)).astype(BF16)
    st_ref[...] = st * jnp.exp(b_last) + lax.dot_general(vb, kd, _TN, preferred_element_type=F32)
    rows = lax.broadcasted_iota(jnp.int32, (sb, 1), 0)
    parts = []
    for s in range(c // sb):
        r0 = s * sb
        qs, ks, bs, vs = qh[r0:r0 + sb], kh[r0:r0 + sb], bh[r0:r0 + sb], vh[r0:r0 + sb]
        acc = o_inter[r0:r0 + sb]
        if s > 0:
            ref_b = bh[r0 - 1:r0, :]
            qf = (qs * jnp.exp(bs - ref_b)).astype(BF16)
            kf = (kh[0:r0] * jnp.exp(ref_b - bh[0:r0])).astype(BF16)
            a_off = lax.dot_general(qf, kf, _NT, preferred_element_type=F32)
            acc = acc + jnp.dot(a_off.astype(BF16), vb[0:r0], preferred_element_type=F32)
        for j in range(sb):
            w = jnp.exp(jnp.minimum(bs - bs[j:j + 1], 0.0)) * qs * ks[j:j + 1]
            col = jnp.where(rows >= j, jnp.sum(w, axis=-1, keepdims=True), 0.0)
            acc = acc + col * vs[j:j + 1]
        parts.append(acc)
    return parts[0] if len(parts) == 1 else jnp.concatenate(parts, axis=0)


def _gla_kernel(*refs, mode, layer, dv, nb, rows_in, chunk, n_chunks, has_s0):
    n_in = (6 if mode == "hgrn" else 8) + (1 if has_s0 else 0)
    ins, (o_ref, sout_ref), scr = refs[:n_in], refs[n_in:n_in + 2], refs[n_in + 2:]
    st_ref = scr[0]
    pad_refs = scr[1:]
    n_act = 4 if mode == "hgrn" else 5
    t_idx = pl.program_id(1)
    padded = rows_in < chunk
    assert padded or nb == 1

    @pl.when(t_idx == 0)
    def _():
        for i in range(nb):
            for h in range(N_HEADS):
                if has_s0:
                    st_ref[i * N_HEADS + h] = ins[-1][i, h].T
                else:
                    st_ref[i * N_HEADS + h] = jnp.zeros(st_ref.shape[1:], F32)

    if padded:
        for p_ref, a_ref in zip(pad_refs, ins[:n_act]):
            p_ref[...] = jnp.zeros(p_ref.shape, F32)
            for i in range(nb):
                p_ref[i * chunk:i * chunk + rows_in, :] = a_ref[i]
        acts = pad_refs
    else:
        acts = [a.at[0] for a in ins[:n_act]]
    groups = nb if padded else n_chunks
    span = groups * chunk

    ld = [a[...] for a in acts]
    if mode == "hgrn":
        q_raw, f, v, gate = ld
        lb_ref, gn_ref = ins[4], ins[5]
        lbv = lb_ref[...]
        e = jnp.exp(lbv - jnp.max(lbv, axis=0, keepdims=True))
        lb = jnp.sum(e[0:layer + 1], axis=0, keepdims=True) / jnp.sum(e, axis=0, keepdims=True)
        forget = lb + (1.0 - lb) * _sigmoid(f)
        k = 1.0 - forget
        lg = jnp.log(forget)
        q = _silu(q_raw)
    else:
        q_raw, k, v, gate, gd = ld
        wg_ref, bg_ref, gn_ref = ins[5], ins[6], ins[7]
        z = jnp.dot(gd.astype(BF16), wg_ref[...], preferred_element_type=F32) + bg_ref[...]
        lg = (jnp.minimum(z, 0.0) - jnp.log(1.0 + jnp.exp(-jnp.abs(z)))) / GLA_GATE_TAU
        q = q_raw * (HEAD_DK ** -0.5)
    if padded:
        live = lax.broadcasted_iota(jnp.int32, (span, 1), 0) % chunk < rows_in
        lg = jnp.where(live, lg, 0.0)
        k = jnp.where(live, k, 0.0)
    b = _cumsum_rows(lg, chunk)

    def piece(x, i, h, width):
        return x[i * chunk:(i + 1) * chunk, h * width:(h + 1) * width]

    def heads(small_decay):
        rows = []
        for i in range(groups):
            cols = []
            for h in range(N_HEADS):
                cols.append(_gla_head(piece(q, i, h, HEAD_DK), piece(k, i, h, HEAD_DK),
                                      piece(b, i, h, HEAD_DK), piece(v, i, h, dv),
                                      st_ref.at[(i if padded else 0) * N_HEADS + h], small_decay))
            rows.append(jnp.concatenate(cols, axis=1))
        return rows[0] if groups == 1 else jnp.concatenate(rows, axis=0)

    if chunk <= SUB_BLOCK:
        o_raw = heads(False)
    else:
        o_raw = lax.cond(jnp.min(b) >= -GLA_SAFE_DECAY,
                         functools.partial(heads, True), functools.partial(heads, False))
    o_all = jnp.concatenate(
        [_rms(o_raw[:, h * dv:(h + 1) * dv], gn_ref[...]) * _silu(gate[:, h * dv:(h + 1) * dv])
         for h in range(N_HEADS)], axis=1).astype(o_ref.dtype)
    if padded:
        for i in range(nb):
            o_ref[i] = o_all[i * chunk:i * chunk + rows_in]
    else:
        o_ref[0] = o_all

    @pl.when(t_idx == pl.num_programs(1) - 1)
    def _():
        for i in range(nb):
            for h in range(N_HEADS):
                sout_ref[i, h] = st_ref[i * N_HEADS + h].T


def _gla_call(mode, proj, params, s0, *, layer, dv, tb, chunk, nb):
    bsz, seq, _ = proj.shape
    rows_in = min(tb, seq)
    if rows_in < chunk:
        assert seq == rows_in and bsz % nb == 0
        nt, n_chunks = 1, 1
    else:
        assert seq % tb == 0 and tb % chunk == 0 and nb == 1
        nt, n_chunks = seq // tb, tb // chunk
    kw, vw = N_HEADS * HEAD_DK, N_HEADS * dv

    def act(width, col_block):
        return pl.BlockSpec((nb, rows_in, width), lambda b, t: (b, t, col_block))

    def whole(a):
        return pl.BlockSpec(a.shape, lambda b, t: (0,) * a.ndim)

    if mode == "hgrn":
        act_specs = [act(kw, 0), act(kw, 1), act(vw, 2), act(vw, 3)]
        act_widths = [kw, kw, vw, vw]
    else:
        act_specs = [act(kw, 0), act(kw, 1), act(vw, kw * 2 // vw), act(vw, kw * 2 // vw + 1),
                     act(128, (2 * kw + 2 * vw) // 128)]
        act_widths = [kw, kw, vw, vw, 128]
    in_specs = act_specs + [whole(p) for p in params]
    args = [proj] * len(act_specs) + list(params)
    state_spec = pl.BlockSpec((nb, N_HEADS, HEAD_DK, dv), lambda b, t: (b, 0, 0, 0))
    if s0 is not None:
        in_specs.append(state_spec)
        args.append(s0)
    scratch = [pltpu.VMEM((nb * N_HEADS, dv, HEAD_DK), F32)]
    if rows_in < chunk:
        scratch += [pltpu.VMEM((nb * chunk, w), F32) for w in act_widths]
    return pl.pallas_call(
        functools.partial(_gla_kernel, mode=mode, layer=layer, dv=dv, nb=nb, rows_in=rows_in,
                          chunk=chunk, n_chunks=n_chunks, has_s0=s0 is not None),
        grid=(bsz // nb, nt),
        in_specs=in_specs,
        out_specs=[pl.BlockSpec((nb, rows_in, vw), lambda b, t: (b, t, 0)), state_spec],
        out_shape=[jax.ShapeDtypeStruct((bsz, seq, vw), BF16),
                   jax.ShapeDtypeStruct((bsz, N_HEADS, HEAD_DK, dv), F32)],
        scratch_shapes=scratch,
        compiler_params=_cparams("arbitrary", "arbitrary"),
        name="gla_" + mode,
    )(*args)


def _s5_prep_kernel(lre_ref, lim_ref, ls_ref, lre_x_ref, lim_x_ref, ls_x_ref, bre_ref, bim_ref,
                    are_ref, aim_ref, bbre_ref, bbim_ref):
    def disc(lre, lim, ls):
        lr = jnp.minimum(lre, S5_MAX_RE)
        dt = jnp.exp(ls)
        mag = jnp.exp(lr * dt)
        a_re = mag * jnp.cos(lim * dt)
        a_im = mag * jnp.sin(lim * dt)
        den = lr * lr + lim * lim
        z_re = ((a_re - 1.0) * lr + a_im * lim) / den
        z_im = (a_im * lr - (a_re - 1.0) * lim) / den
        return a_re, a_im, z_re, z_im

    a_re, a_im, _, _ = disc(lre_ref[...], lim_ref[...], ls_ref[...])
    are_ref[...] = a_re
    aim_ref[...] = a_im
    _, _, z_re, z_im = disc(lre_x_ref[...], lim_x_ref[...], ls_x_ref[...])
    bbre_ref[...] = z_re * bre_ref[...] - z_im * bim_ref[...]
    bbim_ref[...] = z_re * bim_ref[...] + z_im * bre_ref[...]


def _s5_prep(lam_re, lam_im, log_step, b_re, b_im):
    g, n = lam_re.shape
    p = b_re.shape[-1]
    ls = jnp.broadcast_to(log_step[:, None], (g, n))
    rep = lambda a: jnp.repeat(a, p, axis=1)
    outs = pl.pallas_call(
        _s5_prep_kernel,
        out_shape=[jax.ShapeDtypeStruct((g, n), F32)] * 2 + [jax.ShapeDtypeStruct((g, n * p), F32)] * 2,
        name="s5_prep",
    )(lam_re, lam_im, ls, rep(lam_re), rep(lam_im), rep(ls),
      b_re.reshape(g, n * p), b_im.reshape(g, n * p))
    a_re, a_im, bb_re, bb_im = outs
    return a_re, a_im, bb_re.reshape(g, n, p), bb_im.reshape(g, n, p)


def _s5_kernel(*refs, bg, tc, nt, cw, has_x0):
    n_in = 9 if has_x0 else 7
    u_ref, bm_ref, cm_ref, a_ref, d_ref, wg_ref, bgl_ref = refs[:7]
    o_ref, sre_ref, sim_ref = refs[n_in:n_in + 3]
    utm_ref, xs_ref, st_ref = refs[n_in + 3:]
    t_idx = pl.program_id(1)
    half = xs_ref.shape[1] // 2
    uw = u_ref.shape[-1] // 2

    @pl.when(t_idx == 0)
    def _():
        for hf in range(2):
            if has_x0:
                st_ref[:, hf * 2 * half:hf * 2 * half + half] = refs[7][:, hf * half:(hf + 1) * half]
                st_ref[:, hf * 2 * half + half:(hf + 1) * 2 * half] = refs[8][:, hf * half:(hf + 1) * half]
            else:
                st_ref[...] = jnp.zeros(st_ref.shape, F32)

    for t in range(tc):
        utm_ref[t * bg:(t + 1) * bg, :] = u_ref[:, t, :]
    u = utm_ref[...]
    ys = []
    for hf in range(2):
        xs_ref[...] = jnp.dot(u[:, hf * uw:(hf + 1) * uw].astype(BF16), bm_ref[hf],
                              preferred_element_type=F32)
        base = hf * 2 * half
        for c0 in range(0, half, cw):
            ar = a_ref[0:1, base + c0:base + c0 + cw]
            ai = a_ref[0:1, base + half + c0:base + half + c0 + cw]
            xr = st_ref[:, base + c0:base + c0 + cw]
            xi = st_ref[:, base + half + c0:base + half + c0 + cw]
            for t in range(tc):
                rows = slice(t * bg, (t + 1) * bg)
                nr = ar * xr - ai * xi + xs_ref[rows, c0:c0 + cw]
                ni = ar * xi + ai * xr + xs_ref[rows, half + c0:half + c0 + cw]
                xs_ref[rows, c0:c0 + cw] = nr
                xs_ref[rows, half + c0:half + c0 + cw] = ni
                xr, xi = nr, ni
            st_ref[:, base + c0:base + c0 + cw] = xr
            st_ref[:, base + half + c0:base + half + c0 + cw] = xi
        ys.append(jnp.dot(xs_ref[...].astype(BF16), cm_ref[hf], preferred_element_type=F32))
    y = jnp.concatenate(ys, axis=1) + d_ref[...] * u
    y = 0.5 * y * (1.0 + jnp.tanh(math.sqrt(2.0 / math.pi) * (y + 0.044715 * (y * y * y))))
    gate = jnp.dot(y.astype(BF16), wg_ref[...], preferred_element_type=F32) + bgl_ref[...]
    utm_ref[...] = y * _sigmoid(gate)
    for t in range(tc):
        o_ref[:, t, :] = utm_ref[t * bg:(t + 1) * bg, :]

    @pl.when(t_idx == nt - 1)
    def _():
        for hf in range(2):
            sre_ref[:, hf * half:(hf + 1) * half] = st_ref[:, hf * 2 * half:hf * 2 * half + half]
            sim_ref[:, hf * half:(hf + 1) * half] = st_ref[:, hf * 2 * half + half:(hf + 1) * 2 * half]


def _s5_call(u_src, col_block, mats, x0, *, tc, cw):
    bg, seq, _ = u_src.shape
    assert seq % tc == 0
    nt = seq // tc
    bmat, cmat, a_flat, d_row, w_glu, b_glu = mats
    width = d_row.shape[1]
    nstate = a_flat.shape[1] // 2
    whole = lambda a: pl.BlockSpec(a.shape, lambda g, t: (0,) * a.ndim)
    in_specs = [pl.BlockSpec((bg, tc, width), lambda g, t: (0, t, col_block))]
    in_specs += [whole(m) for m in mats]
    args = [u_src] + list(mats)
    st_spec = pl.BlockSpec((bg, nstate), lambda g, t: (0, 0))
    if x0 is not None:
        in_specs += [st_spec, st_spec]
        args += list(x0)
    return pl.pallas_call(
        functools.partial(_s5_kernel, bg=bg, tc=tc, nt=nt, cw=cw, has_x0=x0 is not None),
        grid=(1, nt),
        in_specs=in_specs,
        out_specs=[pl.BlockSpec((bg, tc, width), lambda g, t: (0, t, 0)), st_spec, st_spec],
        out_shape=[jax.ShapeDtypeStruct((bg, seq, width), F32),
                   jax.ShapeDtypeStruct((bg, nstate), F32),
                   jax.ShapeDtypeStruct((bg, nstate), F32)],
        scratch_shapes=[pltpu.VMEM((bg * tc, width), F32), pltpu.VMEM((bg * tc, nstate), F32),
                        pltpu.VMEM((bg, 2 * nstate), F32)],
        compiler_params=_cparams("arbitrary", "arbitrary"),
        name="s5",
    )(*args)


def _s5_matrices(a_re, a_im, bb_re, bb_im, c_re, c_im, d, w_glu, b_glu):
    g, n, p = bb_re.shape
    gh = g // 2
    eye = jnp.eye(gh, dtype=F32)

    def block_diag(t):
        return (eye[:, None, :, None] * t[:, :, None, :]).reshape(gh * t.shape[1], gh * t.shape[2])

    def in_mat(bb):
        return block_diag(bb.transpose(0, 2, 1))

    def out_mat(cc):
        return block_diag(cc.transpose(0, 2, 1))

    bmat = jnp.stack([jnp.concatenate([in_mat(bb_re[h * gh:(h + 1) * gh]),
                                       in_mat(bb_im[h * gh:(h + 1) * gh])], axis=1)
                      for h in range(2)]).astype(BF16)
    cmat = jnp.stack([jnp.concatenate([out_mat(c_re[h * gh:(h + 1) * gh]),
                                       out_mat(-c_im[h * gh:(h + 1) * gh])], axis=0)
                      for h in range(2)]).astype(BF16)
    a_flat = jnp.concatenate([jnp.concatenate([a_re[h * gh:(h + 1) * gh].reshape(1, gh * n),
                                               a_im[h * gh:(h + 1) * gh].reshape(1, gh * n)], axis=1)
                              for h in range(2)], axis=1)
    return (bmat, cmat, a_flat, d.reshape(1, g * p), w_glu.astype(BF16), b_glu.reshape(1, -1))


def _xattn_kernel(*refs, n_pre):
    x_ref = refs[0]
    g_ref, wq_ref, wo_ref, k_ref, v_ref, y_ref, kt_ref, vt_ref = refs[1 + 2 * n_pre:]

    @pl.when(pl.program_id(1) == 0)
    def _():
        kt_ref[...] = jnp.transpose(k_ref[...], (1, 0, 2)).astype(BF16)
        vt_ref[...] = jnp.transpose(v_ref[...], (1, 0, 2)).astype(BF16)

    x = x_ref[...]
    for a_ref, w_ref in zip(refs[1:1 + n_pre], refs[1 + n_pre:1 + 2 * n_pre]):
        x = x + jnp.dot(a_ref[...].astype(BF16), w_ref[...], preferred_element_type=F32)
    q = jnp.dot(_rms(x, g_ref[...]).astype(BF16), wq_ref[...], preferred_element_type=F32).astype(BF16)
    hd = q.shape[1] // N_HEADS
    outs = []
    for h in range(N_HEADS):
        s = lax.dot_general(q[:, h * hd:(h + 1) * hd], kt_ref[h], _NT,
                            preferred_element_type=F32) * (hd ** -0.5)
        p = jnp.exp(s - jnp.max(s, axis=-1, keepdims=True))
        p = p / jnp.sum(p, axis=-1, keepdims=True)
        outs.append(jnp.dot(p.astype(BF16), vt_ref[h], preferred_element_type=F32).astype(BF16))
    y_ref[...] = x + jnp.dot(jnp.concatenate(outs, axis=1), wo_ref[...], preferred_element_type=F32)


def _xattn_call(x, pre_a, pre_w, g, w_q, w_o, mem_k, mem_v, layer, *, tq):
    bsz, seq, d = x.shape
    n_mem, nh, hd = mem_k.shape[2:]
    assert seq % tq == 0
    kv_spec = pl.BlockSpec((None, None, n_mem, nh, hd), lambda b, t: (layer, b, 0, 0, 0))
    rows = lambda a: pl.BlockSpec((None, tq, a.shape[-1]), lambda b, t: (b, t, 0))
    params = [_whole(w) for w in (*pre_w, g, w_q, w_o)]
    return pl.pallas_call(
        functools.partial(_xattn_kernel, n_pre=len(pre_a)),
        grid=(bsz, seq // tq),
        in_specs=[rows(x)] + [rows(a) for a in pre_a] + [spec for _, spec in params] + [kv_spec, kv_spec],
        out_specs=rows(x),
        out_shape=jax.ShapeDtypeStruct((bsz, seq, d), F32),
        scratch_shapes=[pltpu.VMEM((nh, n_mem, hd), BF16), pltpu.VMEM((nh, n_mem, hd), BF16)],
        compiler_params=_cparams("arbitrary", "arbitrary"),
        name="xattn",
    )(x, *pre_a, *[w for w, _ in params], mem_k, mem_v)


def _attn_rows_kernel(q_ref, k_ref, v_ref, o_ref, pad_ref, *, rows_in):
    nb, rows, d = pad_ref.shape
    nblk = d // 128
    half_blk = nblk // 2
    lanes = k_ref.shape[1]
    lane_blk = lax.broadcasted_iota(jnp.int32, (1, lanes), 1) % nblk
    row_head = lax.broadcasted_iota(jnp.int32, (N_HEADS * rows, 1), 0) // rows
    live = lane_blk == row_head
    scale = (d // N_HEADS) ** -0.5
    for i in range(nb):
        pad_ref[i] = jnp.zeros((rows, d), F32)
        pad_ref[i, 0:rows_in, :] = q_ref[i]
        q = pad_ref[i]
        qx = jnp.concatenate([q[:, j * 128:(j + 1) * 128] for j in range(nblk)], axis=0)
        g = lax.dot_general(qx.astype(BF16), k_ref[i].astype(BF16), _NT, preferred_element_type=F32)
        s = jnp.concatenate(
            [g[2 * h * rows:(2 * h + 1) * rows]
             + pltpu.roll(g[(2 * h + 1) * rows:(2 * h + 2) * rows], lanes - half_blk, axis=1)
             for h in range(N_HEADS)], axis=0) * scale
        s = jnp.where(live, s, -1e30)
        e = jnp.exp(s - jnp.max(s, axis=-1, keepdims=True))
        p = e / jnp.sum(e, axis=-1, keepdims=True)
        px = jnp.concatenate(
            [blk for h in range(N_HEADS)
             for blk in (p[h * rows:(h + 1) * rows], pltpu.roll(p[h * rows:(h + 1) * rows], half_blk, axis=1))],
            axis=0)
        o = jnp.dot(px.astype(BF16), v_ref[i].astype(BF16), preferred_element_type=F32)
        o = jnp.concatenate([o[j * rows:(j + 1) * rows] for j in range(nblk)], axis=1)
        o_ref[i] = o[0:rows_in].astype(o_ref.dtype)


def _attn_rows_call(q, mem_k, mem_v, layer, *, nb):
    bsz, seq, d = q.shape
    depth, _, n_mem, nh, hd = mem_k.shape
    assert bsz % nb == 0 and hd == 256 and nh == N_HEADS and seq <= 16
    as_rows = lambda a: a.reshape(depth, bsz, n_mem, nh, 2, 128).transpose(0, 1, 2, 4, 3, 5).reshape(
        depth, bsz, n_mem * 2 * nh, 128)
    kv_spec = pl.BlockSpec((None, nb, n_mem * 2 * nh, 128), lambda b: (layer, b, 0, 0))
    return pl.pallas_call(
        functools.partial(_attn_rows_kernel, rows_in=seq),
        grid=(bsz // nb,),
        in_specs=[pl.BlockSpec((nb, seq, d), lambda b: (b, 0, 0)), kv_spec, kv_spec],
        out_specs=pl.BlockSpec((nb, seq, d), lambda b: (b, 0, 0)),
        out_shape=jax.ShapeDtypeStruct((bsz, seq, d), BF16),
        scratch_shapes=[pltpu.VMEM((nb, 16, d), F32)],
        compiler_params=_cparams("arbitrary"),
        name="mem_attn_rows",
    )(q, as_rows(mem_k), as_rows(mem_v))


FFN_COLS = 256


def _ffn_kernel(*refs, tm, ts, hs, f_dim, has_hist, has_final):
    n_in = 6 + int(has_hist) + int(has_final)
    x_ref, g_ref, wup_ref, cw_ref, cb_ref, wdn_ref = refs[:6]
    y_ref, state_ref = refs[n_in:n_in + 2]
    gated_ref, hist_ref = refs[n_in + 2:n_in + 4]
    t_idx = pl.program_id(1)

    @pl.when(t_idx == 0)
    def _():
        hist_ref[...] = jnp.zeros(hist_ref.shape, F32)
        if has_hist:
            hist_ref[hs - 2 * ts:hs, :] = refs[6][...]

    x = x_ref[...]
    h = _rms(x, g_ref[...]).astype(BF16)
    row = lax.broadcasted_iota(jnp.int32, (tm, 1), 0)
    for c in range(f_dim // FFN_COLS):
        conv = []
        for part in range(2):
            cols = slice(part * f_dim + c * FFN_COLS, part * f_dim + (c + 1) * FFN_COLS)
            u = jnp.dot(h, wup_ref[:, cols], preferred_element_type=F32)
            if ts == 1:
                prev2, prev1 = hist_ref[hs - 2:hs - 1, cols], hist_ref[hs - 1:hs, cols]
                m1 = jnp.where(row == 0, prev1, pltpu.roll(u, 1, axis=0))
                m2 = jnp.where(row == 0, prev2, jnp.where(row == 1, prev1, pltpu.roll(u, 2, axis=0)))
            else:
                ext = jnp.concatenate([hist_ref[hs - 2 * ts:hs, cols], u], axis=0)
                m2, m1 = ext[0:tm], ext[ts:ts + tm]
            conv.append(cb_ref[:, cols] + cw_ref[0:1, cols] * m2 + cw_ref[1:2, cols] * m1
                        + cw_ref[2:3, cols] * u)
            hist_ref[:, cols] = u[tm - hs:tm]
        gated_ref[:, c * FFN_COLS:(c + 1) * FFN_COLS] = (_silu(conv[0]) * conv[1]).astype(BF16)
    out = x + jnp.dot(gated_ref[...], wdn_ref[...], preferred_element_type=F32)
    if has_final:
        out = _rms(out, refs[n_in - 1][...])
    y_ref[...] = out

    @pl.when(t_idx == pl.num_programs(1) - 1)
    def _():
        state_ref[...] = hist_ref[hs - 2 * ts:hs, :]


def _ffn_call(x, g, w_up, conv_w, conv_b, w_down, hist0, g_final, *, tm, ts):
    ngrp, rows, d = x.shape
    params = [_whole(a) for a in (g, w_up, conv_w, conv_b, w_down)]
    f2 = params[1][0].shape[-1]
    f_dim = f2 // 2
    hs = max(8, 2 * ts)
    assert rows % tm == 0 and tm >= hs and f_dim % FFN_COLS == 0 and (ts == 1 or ts % 8 == 0)
    args = [x] + [a for a, _ in params]
    in_specs = [pl.BlockSpec((None, tm, d), lambda s, t: (s, t, 0))] + [spec for _, spec in params]
    st_spec = pl.BlockSpec((None, 2 * ts, f2), lambda s, t: (s, 0, 0))
    if hist0 is not None:
        in_specs.append(st_spec)
        args.append(hist0)
    if g_final is not None:
        in_specs.append(_whole(g_final)[1])
        args.append(g_final)
    return pl.pallas_call(
        functools.partial(_ffn_kernel, tm=tm, ts=ts, hs=hs, f_dim=f_dim,
                          has_hist=hist0 is not None, has_final=g_final is not None),
        grid=(ngrp, rows // tm),
        in_specs=in_specs,
        out_specs=[pl.BlockSpec((None, tm, d), lambda s, t: (s, t, 0)), st_spec],
        out_shape=[jax.ShapeDtypeStruct((ngrp, rows, d), F32),
                   jax.ShapeDtypeStruct((ngrp, 2 * ts, f2), F32)],
        scratch_shapes=[pltpu.VMEM((tm, f_dim), BF16), pltpu.VMEM((hs, f2), F32)],
        compiler_params=_cparams("arbitrary", "arbitrary"),
        name="conv_ffn",
    )(*args)


def _trunk(x, mem_k, mem_v, states, p, *, prompt):
    bsz, seq, d = x.shape
    depth = p["norm_mix"].shape[0]
    m = bsz * seq
    x2 = x.reshape(m, d)
    new = {"hgrn": [], "s5_re": [], "s5_im": [], "gla": [], "conv": []}
    if prompt:
        gla_tiles = dict(tb=256, chunk=64, nb=1)
        attn_tq = 512
    else:
        gla_tiles = dict(tb=seq, chunk=16, nb=8)
        attn_tq = seq
    for l in range(depth):
        g_mix = _pick(p["norm_mix"], l)
        if l % 2 == 0:
            e = l // 2
            proj = _norm_proj(x2, g_mix, _pick(p["w_in_ab"], e))
            kw = N_HEADS * HEAD_DK
            o_a, s_a = _gla_call(
                "hgrn", proj.reshape(bsz, seq, -1),
                [p["hgrn_lb"], p["hgrn_gnorm"][e].reshape(1, -1)],
                None if states is None else states["hgrn"][e],
                layer=l, dv=kw // N_HEADS, **gla_tiles)
            mats = p["s5_mats"][e]
            if prompt:
                o_b, sr, si = _s5_call(proj.reshape(bsz, seq, -1), 4, mats, None, tc=64, cw=512)
            else:
                x0 = (states["s5_re"][e].reshape(bsz, -1), states["s5_im"][e].reshape(bsz, -1))
                o_b, sr, si = _s5_call(proj.reshape(bsz, seq, -1), 4, mats, x0, tc=seq, cw=128)
            mixed = [o_a, o_b]
            mixed_w = [_pick(p["w_out_ab"], e, rows=kw, row_block=0), _pick(p["w_out_ab"], e, rows=kw, row_block=1)]
            new["hgrn"].append(s_a)
            new["s5_re"].append(sr.reshape(bsz, -1, S5_STATE))
            new["s5_im"].append(si.reshape(bsz, -1, S5_STATE))
        else:
            o_idx = l // 2
            proj = _norm_proj(x2, g_mix, _pick(p["w_in_c"], o_idx))
            o_c, s_c = _gla_call(
                "gla", proj.reshape(bsz, seq, -1),
                [p["gla_w_gate"][o_idx], p["gla_b_gate"][o_idx].reshape(1, -1),
                 p["gla_gnorm"][o_idx].reshape(1, -1)],
                None if states is None else states["gla"][o_idx],
                layer=l, dv=d // N_HEADS, **gla_tiles)
            mixed, mixed_w = [o_c], [_pick(p["w_out_c"], o_idx)]
            new["gla"].append(s_c)
        g_cross, w_q, w_o = _pick(p["norm_cross"], l), _pick(p["xa_w_q"], l), _pick(p["xa_w_o"], l)
        if prompt:
            x2 = _xattn_call(x2.reshape(bsz, seq, d), mixed, mixed_w, g_cross, w_q, w_o,
                             mem_k, mem_v, l, tq=attn_tq).reshape(m, d)
        else:
            x2 = _proj_res(x2, [a.reshape(m, -1) for a in mixed], mixed_w)
            q = _norm_proj(x2, g_cross, w_q)
            o_x = _attn_rows_call(q.reshape(bsz, seq, d), mem_k, mem_v, l, nb=4)
            x2 = _proj_res(x2, [o_x.reshape(m, d)], [w_o])
        g_final = p["norm_final"] if l == depth - 1 else None
        ffn_w = tuple(_pick(p[name], l) for name in
                      ("norm_ffn", "ffn_w_up", "ffn_conv_w", "ffn_conv_b", "ffn_w_down"))
        if prompt:
            y, cst = _ffn_call(x2.reshape(bsz, seq, d), *ffn_w, None, g_final, tm=512, ts=1)
            x2 = y.reshape(m, d)
        else:
            xt = x2.reshape(bsz, seq, d).transpose(1, 0, 2).reshape(1, m, d)
            hist0 = states["conv"][l].transpose(1, 0, 2).reshape(1, 2 * bsz, -1)
            y, cst = _ffn_call(xt, *ffn_w, hist0, g_final, tm=m, ts=bsz)
            x2 = y.reshape(seq, bsz, d).transpose(1, 0, 2).reshape(m, d)
            cst = cst.reshape(2, bsz, -1).transpose(1, 0, 2)
        new["conv"].append(cst)
    return x2.reshape(bsz, seq, d), new


def kernel(x_prompt, x_sample, mem_prompt, cache_mem_k, cache_mem_v, state_hgrn, state_s5_re, state_s5_im, state_gla, state_ffn_conv, norm_mix, norm_cross, norm_mem, norm_ffn, norm_final, w_in_ab, hgrn_lb, hgrn_gnorm, s5_lam_re, s5_lam_im, s5_log_step, s5_b_re, s5_b_im, s5_c_re, s5_c_im, s5_d, s5_w_glu, s5_b_glu, w_out_ab, w_in_c, gla_w_gate_up, gla_b_gate, gla_gnorm, w_out_c, xa_w_q, xa_w_kv, xa_w_o, ffn_w_up, ffn_conv_w, ffn_conv_b, ffn_w_down):
    depth, d = norm_mix.shape

    gla_cols = w_in_c.shape[2]
    gate_rank = gla_w_gate_up.shape[1]
    pad_c = (-gla_cols) % 128
    w_in_c_p = jnp.pad(w_in_c, ((0, 0), (0, 0), (0, pad_c))).astype(BF16)
    gla_w_gate = jnp.pad(gla_w_gate_up, ((0, 0), (0, 128 - gate_rank), (0, 0))).astype(BF16)

    s5_mats = []
    for e in range(s5_lam_re.shape[0]):
        a_re, a_im, bb_re, bb_im = _s5_prep(s5_lam_re[e], s5_lam_im[e], s5_log_step[e],
                                            s5_b_re[e], s5_b_im[e])
        s5_mats.append(_s5_matrices(a_re, a_im, bb_re, bb_im, s5_c_re[e], s5_c_im[e], s5_d[e],
                                    s5_w_glu[e], s5_b_glu[e]))

    row = lambda a: a.reshape(a.shape[0], 1, a.shape[1])
    p = dict(norm_mix=row(norm_mix), norm_cross=row(norm_cross), norm_ffn=row(norm_ffn),
             norm_final=norm_final.reshape(1, d),
             w_in_ab=w_in_ab.astype(BF16), hgrn_lb=hgrn_lb, hgrn_gnorm=hgrn_gnorm, s5_mats=s5_mats,
             w_out_ab=w_out_ab.astype(BF16), w_in_c=w_in_c_p, gla_w_gate=gla_w_gate,
             gla_b_gate=gla_b_gate, gla_gnorm=gla_gnorm, w_out_c=w_out_c.astype(BF16),
             xa_w_q=xa_w_q.astype(BF16), xa_w_o=xa_w_o.astype(BF16),
             ffn_w_up=ffn_w_up.astype(BF16), ffn_conv_w=ffn_conv_w, ffn_conv_b=row(ffn_conv_b),
             ffn_w_down=ffn_w_down.astype(BF16))

    mem_k_p, mem_v_p = _mem_kv(mem_prompt, norm_mem.reshape(depth, 1, d), xa_w_kv.astype(BF16))
    y_prompt, st_p = _trunk(x_prompt, mem_k_p, mem_v_p, None, p, prompt=True)

    states = dict(hgrn=state_hgrn, s5_re=state_s5_re, s5_im=state_s5_im, gla=state_gla,
                  conv=state_ffn_conv)
    y_sample, st_s = _trunk(x_sample, cache_mem_k, cache_mem_v, states, p, prompt=False)

    stack = lambda xs: xs[0][None] if len(xs) == 1 else jnp.stack(xs)
    return (y_prompt, y_sample,
            stack(st_p["hgrn"]), stack(st_p["s5_re"]), stack(st_p["s5_im"]), stack(st_p["gla"]),
            mem_k_p, mem_v_p, stack(st_p["conv"]),
            stack(st_s["hgrn"]), stack(st_s["s5_re"]), stack(st_s["s5_im"]), stack(st_s["gla"]),
            stack(st_s["conv"]))
```

```python
import functools
import math

import jax
import jax.numpy as jnp
from jax import lax
from jax.experimental import pallas as pl
from jax.experimental.pallas import tpu as pltpu

F32 = jnp.float32
BF16 = jnp.bfloat16

EPS = 1e-6
S5_MAX_RE = -1e-4
GLA_GATE_TAU = 16.0
N_HEADS = 4
HEAD_DK = 128
S5_GROUP = 16
S5_STATE = 64
SUB_BLOCK = 16
GLA_SAFE_DECAY = 64.0
VMEM_LIMIT = 56 * 1024 * 1024

_NT = (((1,), (1,)), ((), ()))
_TN = (((0,), (0,)), ((), ()))


def _cparams(*sem):
    return pltpu.CompilerParams(dimension_semantics=sem, vmem_limit_bytes=VMEM_LIMIT)


def _rms(x, g):
    return x * lax.rsqrt(jnp.mean(x * x, axis=-1, keepdims=True) + EPS) * g


def _sigmoid(x):
    return 1.0 / (1.0 + jnp.exp(-x))


def _silu(x):
    return x * _sigmoid(x)


def _row_tile(rows, want):
    t = min(rows, want)
    assert rows % t == 0, (rows, t)
    return t


def _pick(a, layer, rows=None, row_block=0, cols=None, col_block=0):
    block = (None, rows or a.shape[1], cols or a.shape[2])
    return a, pl.BlockSpec(block, lambda *_: (layer, row_block, col_block))


def _whole(a):
    return a if isinstance(a, tuple) else (a, pl.BlockSpec(a.shape, lambda *_: (0,) * a.ndim))


def _norm_proj_kernel(x_ref, g_ref, w_ref, o_ref):
    h = _rms(x_ref[...], g_ref[...]).astype(BF16)
    n = o_ref.shape[1]
    for c0 in range(0, n, 512):
        cw = min(512, n - c0)
        o_ref[:, c0:c0 + cw] = jnp.dot(h, w_ref[:, c0:c0 + cw], preferred_element_type=F32)


def _norm_proj(x, g, w, *, tm=512):
    m, k = x.shape
    (g, g_spec), (w, w_spec) = _whole(g), _whole(w)
    n = w.shape[-1]
    tm = _row_tile(m, tm)
    return pl.pallas_call(
        _norm_proj_kernel,
        grid=(m // tm,),
        in_specs=[pl.BlockSpec((tm, k), lambda i: (i, 0)), g_spec, w_spec],
        out_specs=pl.BlockSpec((tm, n), lambda i: (i, 0)),
        out_shape=jax.ShapeDtypeStruct((m, n), F32),
        compiler_params=_cparams("arbitrary"),
        name="norm_proj",
    )(x, g, w)


def _mem_kv_kernel(x_ref, g_ref, w_ref, k_ref, v_ref):
    h = _rms(x_ref[...], g_ref[...]).astype(BF16)
    d = x_ref.shape[-1]
    hd = d // N_HEADS
    for o_ref, base in ((k_ref, 0), (v_ref, d)):
        for hh in range(N_HEADS):
            o_ref[:, hh, :] = jnp.dot(h, w_ref[:, base + hh * hd:base + (hh + 1) * hd],
                                      preferred_element_type=F32)


def _mem_kv(mem, g, w):
    bsz, n_mem, d = mem.shape
    depth = w.shape[0]
    out_spec = pl.BlockSpec((None, None, n_mem, N_HEADS, d // N_HEADS), lambda l, b: (l, b, 0, 0, 0))
    out_shape = jax.ShapeDtypeStruct((depth, bsz, n_mem, N_HEADS, d // N_HEADS), F32)
    return pl.pallas_call(
        _mem_kv_kernel,
        grid=(depth, bsz),
        in_specs=[pl.BlockSpec((None, n_mem, d), lambda l, b: (b, 0, 0)),
                  pl.BlockSpec((None, 1, d), lambda l, b: (l, 0, 0)),
                  pl.BlockSpec((None, d, 2 * d), lambda l, b: (l, 0, 0))],
        out_specs=[out_spec, out_spec],
        out_shape=[out_shape, out_shape],
        compiler_params=_cparams("arbitrary", "arbitrary"),
        name="mem_kv",
    )(mem, g, w)


def _proj_res_kernel(res_ref, *refs, n_in):
    acc = res_ref[...]
    for a_ref, w_ref in zip(refs[:n_in], refs[n_in:2 * n_in]):
        acc = acc + jnp.dot(a_ref[...].astype(BF16), w_ref[...], preferred_element_type=F32)
    refs[2 * n_in][...] = acc


def _proj_res(res, a_list, w_list, *, tm=512):
    m, n = res.shape
    tm = _row_tile(m, tm)
    n_in = len(a_list)
    w_list = [_whole(w) for w in w_list]
    in_specs = [pl.BlockSpec((tm, n), lambda i: (i, 0))]
    in_specs += [pl.BlockSpec((tm, a.shape[1]), lambda i: (i, 0)) for a in a_list]
    in_specs += [spec for _, spec in w_list]
    return pl.pallas_call(
        functools.partial(_proj_res_kernel, n_in=n_in),
        grid=(m // tm,),
        in_specs=in_specs,
        out_specs=pl.BlockSpec((tm, n), lambda i: (i, 0)),
        out_shape=jax.ShapeDtypeStruct((m, n), F32),
        compiler_params=_cparams("arbitrary"),
        name="proj_res",
    )(res, *a_list, *[w for w, _ in w_list])


def _cumsum_rows(x, c):
    n = x.shape[0]
    hi = x.astype(BF16)
    rest = x - hi.astype(F32)
    mid = rest.astype(BF16)
    lo = (rest - mid.astype(F32)).astype(BF16)
    r = lax.broadcasted_iota(jnp.int32, (n, n), 0)
    col = lax.broadcasted_iota(jnp.int32, (n, n), 1)
    tri = jnp.where((r >= col) & (r // c == col // c), 1.0, 0.0).astype(BF16)
    return jnp.dot(jnp.concatenate([tri, tri, tri], axis=1), jnp.concatenate([hi, mid, lo], axis=0),
                   preferred_element_type=F32)


def _gla_head(qh, kh, bh, vh, st_ref, small_decay):
    c = qh.shape[0]
    sb = min(SUB_BLOCK, c)
    vb = vh.astype(BF16)
    st = st_ref[...]
    b_last = bh[c - 1:c, :]
    q_in = (qh * jnp.exp(bh)).astype(BF16)
    o_inter = lax.dot_general(q_in, st.astype(BF16), _NT, preferred_element_type=F32)
    if small_decay:
        k_up = kh * jnp.exp(-bh)
        a = lax.dot_general(q_in, k_up.astype(BF16), _NT, preferred_element_type=F32)
        causal = (lax.broadcasted_iota(jnp.int32, (c, c), 0) >= lax.broadcasted_iota(jnp.int32, (c, c), 1))
        kd = (k_up * jnp.exp(b_last)).astype(BF16)
        st_ref[...] = st * jnp.exp(b_last) + lax.dot_general(vb, kd, _TN, preferred_element_type=F32)
        return o_inter + jnp.dot(jnp.where(causal, a, 0.0).astype(BF16), vb, preferred_element_type=F32)
    kd = (kh * jnp.exp(b_last - bh)).astype(BF16)
    st_ref[...] = st * jnp.exp(b_last) + lax.dot_general(vb, kd, _TN, preferred_element_type=F32)
    rows = lax.broadcasted_iota(jnp.int32, (sb, 1), 0)
    parts = []
    for s in range(c // sb):
        r0 = s * sb
        qs, ks, bs, vs = qh[r0:r0 + sb], kh[r0:r0 + sb], bh[r0:r0 + sb], vh[r0:r0 + sb]
        acc = o_inter[r0:r0 + sb]
        if s > 0:
            ref_b = bh[r0 - 1:r0, :]
            qf = (qs * jnp.exp(bs - ref_b)).astype(BF16)
            kf = (kh[0:r0] * jnp.exp(ref_b - bh[0:r0])).astype(BF16)
            a_off = lax.dot_general(qf, kf, _NT, preferred_element_type=F32)
            acc = acc + jnp.dot(a_off.astype(BF16), vb[0:r0], preferred_element_type=F32)
        for j in range(sb):
            w = jnp.exp(jnp.minimum(bs - bs[j:j + 1], 0.0)) * qs * ks[j:j + 1]
            col = jnp.where(rows >= j, jnp.sum(w, axis=-1, keepdims=True), 0.0)
            acc = acc + col * vs[j:j + 1]
        parts.append(acc)
    return parts[0] if len(parts) == 1 else jnp.concatenate(parts, axis=0)


def _gla_kernel(*refs, mode, layer, dv, nb, rows_in, chunk, n_chunks, has_s0, in_widths):
    n_act = 4 if mode == "hgrn" else 5
    n_src = 3 if in_widths else n_act
    n_par = 2 if mode == "hgrn" else 3
    n_in = n_src + n_par + (1 if has_s0 else 0)
    n_out = 2 + (len(in_widths) - n_act if in_widths else 0)
    ins, (o_ref, sout_ref), scr = refs[:n_in], refs[n_in:n_in + 2], refs[n_in + n_out:]
    pars = ins[n_src:n_src + n_par]
    st_ref = scr[0]
    pad_refs = scr[1:]
    t_idx = pl.program_id(1)
    padded = rows_in < chunk
    assert (padded and not in_widths) or nb == 1

    @pl.when(t_idx == 0)
    def _():
        for i in range(nb):
            for h in range(N_HEADS):
                if has_s0:
                    st_ref[i * N_HEADS + h] = ins[-1][i, h].T
                else:
                    st_ref[i * N_HEADS + h] = jnp.zeros(st_ref.shape[1:], F32)

    if in_widths:
        x_ref, gm_ref, win_ref = ins[:n_src]
        hx = _rms(x_ref[0], gm_ref[...]).astype(BF16)
        offs = [sum(in_widths[:i]) for i in range(len(in_widths))]
        ld = [jnp.dot(hx, win_ref[:, o:o + w], preferred_element_type=F32) for o, w in zip(offs, in_widths)]
        for extra_ref, extra in zip(refs[n_in + 2:n_in + n_out], ld[n_act:]):
            extra_ref[0] = extra
        ld = ld[:n_act]
    elif padded:
        for p_ref, a_ref in zip(pad_refs, ins[:n_act]):
            p_ref[...] = jnp.zeros(p_ref.shape, F32)
            for i in range(nb):
                p_ref[i * chunk:i * chunk + rows_in, :] = a_ref[i]
        ld = [a[...] for a in pad_refs]
    else:
        ld = [a[0] for a in ins[:n_act]]
    groups = nb if padded else n_chunks
    span = groups * chunk

    if mode == "hgrn":
        q_raw, f, v, gate = ld
        lb_ref, gn_ref = pars
        lbv = lb_ref[...]
        e = jnp.exp(lbv - jnp.max(lbv, axis=0, keepdims=True))
        lb = jnp.sum(e[0:layer + 1], axis=0, keepdims=True) / jnp.sum(e, axis=0, keepdims=True)
        forget = lb + (1.0 - lb) * _sigmoid(f)
        k = 1.0 - forget
        lg = jnp.log(forget)
        q = _silu(q_raw)
    else:
        q_raw, k, v, gate, gd = ld
        wg_ref, bg_ref, gn_ref = pars
        z = jnp.dot(gd.astype(BF16), wg_ref[...], preferred_element_type=F32) + bg_ref[...]
        lg = (jnp.minimum(z, 0.0) - jnp.log(1.0 + jnp.exp(-jnp.abs(z)))) / GLA_GATE_TAU
        q = q_raw * (HEAD_DK ** -0.5)
    if padded:
        live = lax.broadcasted_iota(jnp.int32, (span, 1), 0) % chunk < rows_in
        lg = jnp.where(live, lg, 0.0)
        k = jnp.where(live, k, 0.0)
    b = _cumsum_rows(lg, chunk)

    def piece(x, i, h, width):
        return x[i * chunk:(i + 1) * chunk, h * width:(h + 1) * width]

    def heads(small_decay):
        rows = []
        for i in range(groups):
            cols = []
            for h in range(N_HEADS):
                cols.append(_gla_head(piece(q, i, h, HEAD_DK), piece(k, i, h, HEAD_DK),
                                      piece(b, i, h, HEAD_DK), piece(v, i, h, dv),
                                      st_ref.at[(i if padded else 0) * N_HEADS + h], small_decay))
            rows.append(jnp.concatenate(cols, axis=1))
        return rows[0] if groups == 1 else jnp.concatenate(rows, axis=0)

    if chunk <= SUB_BLOCK:
        o_raw = heads(False)
    else:
        o_raw = lax.cond(jnp.min(b) >= -GLA_SAFE_DECAY,
                         functools.partial(heads, True), functools.partial(heads, False))
    o_all = jnp.concatenate(
        [_rms(o_raw[:, h * dv:(h + 1) * dv], gn_ref[...]) * _silu(gate[:, h * dv:(h + 1) * dv])
         for h in range(N_HEADS)], axis=1).astype(o_ref.dtype)
    if padded:
        for i in range(nb):
            o_ref[i] = o_all[i * chunk:i * chunk + rows_in]
    else:
        o_ref[0] = o_all

    @pl.when(t_idx == pl.num_programs(1) - 1)
    def _():
        for i in range(nb):
            for h in range(N_HEADS):
                sout_ref[i, h] = st_ref[i * N_HEADS + h].T


def _gla_call(mode, src, params, s0, *, layer, dv, tb, chunk, nb, extra_widths=()):
    fused_in = isinstance(src, tuple)
    proj = src[0] if fused_in else src
    bsz, seq, _ = proj.shape
    rows_in = min(tb, seq)
    if rows_in < chunk:
        assert seq == rows_in and bsz % nb == 0
        nt, n_chunks = 1, 1
    else:
        assert seq % tb == 0 and tb % chunk == 0 and nb == 1
        nt, n_chunks = seq // tb, tb // chunk
    kw, vw = N_HEADS * HEAD_DK, N_HEADS * dv

    def act(width, col_block):
        return pl.BlockSpec((nb, rows_in, width), lambda b, t: (b, t, col_block))

    def whole(a):
        return pl.BlockSpec(a.shape, lambda b, t: (0,) * a.ndim)

    if mode == "hgrn":
        act_specs = [act(kw, 0), act(kw, 1), act(vw, 2), act(vw, 3)]
        act_widths = [kw, kw, vw, vw]
    else:
        act_specs = [act(kw, 0), act(kw, 1), act(vw, kw * 2 // vw), act(vw, kw * 2 // vw + 1),
                     act(128, (2 * kw + 2 * vw) // 128)]
        act_widths = [kw, kw, vw, vw, 128]
    if fused_in:
        x, g_mix, w_in = src[0], _whole(src[1]), _whole(src[2])
        in_specs = [pl.BlockSpec((nb, rows_in, x.shape[-1]), lambda b, t: (b, t, 0)), g_mix[1], w_in[1]]
        args = [x, g_mix[0], w_in[0]]
    else:
        in_specs, args = act_specs, [proj] * len(act_specs)
    in_specs = in_specs + [whole(p) for p in params]
    args = args + list(params)
    state_spec = pl.BlockSpec((nb, N_HEADS, HEAD_DK, dv), lambda b, t: (b, 0, 0, 0))
    if s0 is not None:
        in_specs.append(state_spec)
        args.append(s0)
    scratch = [pltpu.VMEM((nb * N_HEADS, dv, HEAD_DK), F32)]
    if rows_in < chunk:
        scratch += [pltpu.VMEM((nb * chunk, w), F32) for w in act_widths]
    return pl.pallas_call(
        functools.partial(_gla_kernel, mode=mode, layer=layer, dv=dv, nb=nb, rows_in=rows_in,
                          chunk=chunk, n_chunks=n_chunks, has_s0=s0 is not None,
                          in_widths=tuple(act_widths) + tuple(extra_widths) if fused_in else None),
        grid=(bsz // nb, nt),
        in_specs=in_specs,
        out_specs=[pl.BlockSpec((nb, rows_in, vw), lambda b, t: (b, t, 0)), state_spec]
        + [pl.BlockSpec((nb, rows_in, w), lambda b, t: (b, t, 0)) for w in extra_widths],
        out_shape=[jax.ShapeDtypeStruct((bsz, seq, vw), BF16),
                   jax.ShapeDtypeStruct((bsz, N_HEADS, HEAD_DK, dv), F32)]
        + [jax.ShapeDtypeStruct((bsz, seq, w), F32) for w in extra_widths],
        scratch_shapes=scratch,
        compiler_params=_cparams("arbitrary", "arbitrary"),
        name="gla_" + mode,
    )(*args)


def _s5_prep_kernel(lre_ref, lim_ref, ls_ref, lre_x_ref, lim_x_ref, ls_x_ref, bre_ref, bim_ref,
                    are_ref, aim_ref, bbre_ref, bbim_ref):
    def disc(lre, lim, ls):
        lr = jnp.minimum(lre, S5_MAX_RE)
        dt = jnp.exp(ls)
        mag = jnp.exp(lr * dt)
        a_re = mag * jnp.cos(lim * dt)
        a_im = mag * jnp.sin(lim * dt)
        den = lr * lr + lim * lim
        z_re = ((a_re - 1.0) * lr + a_im * lim) / den
        z_im = (a_im * lr - (a_re - 1.0) * lim) / den
        return a_re, a_im, z_re, z_im

    a_re, a_im, _, _ = disc(lre_ref[...], lim_ref[...], ls_ref[...])
    are_ref[...] = a_re
    aim_ref[...] = a_im
    _, _, z_re, z_im = disc(lre_x_ref[...], lim_x_ref[...], ls_x_ref[...])
    bbre_ref[...] = z_re * bre_ref[...] - z_im * bim_ref[...]
    bbim_ref[...] = z_re * bim_ref[...] + z_im * bre_ref[...]


def _s5_prep(lam_re, lam_im, log_step, b_re, b_im):
    g, n = lam_re.shape
    p = b_re.shape[-1]
    ls = jnp.broadcast_to(log_step[:, None], (g, n))
    rep = lambda a: jnp.repeat(a, p, axis=1)
    outs = pl.pallas_call(
        _s5_prep_kernel,
        out_shape=[jax.ShapeDtypeStruct((g, n), F32)] * 2 + [jax.ShapeDtypeStruct((g, n * p), F32)] * 2,
        name="s5_prep",
    )(lam_re, lam_im, ls, rep(lam_re), rep(lam_im), rep(ls),
      b_re.reshape(g, n * p), b_im.reshape(g, n * p))
    a_re, a_im, bb_re, bb_im = outs
    return a_re, a_im, bb_re.reshape(g, n, p), bb_im.reshape(g, n, p)


def _s5_kernel(*refs, bg, tc, nt, cw, has_x0):
    n_in = 9 if has_x0 else 7
    u_ref, bm_ref, cm_ref, a_ref, d_ref, wg_ref, bgl_ref = refs[:7]
    o_ref, sre_ref, sim_ref = refs[n_in:n_in + 3]
    utm_ref, xs_ref, st_ref = refs[n_in + 3:]
    t_idx = pl.program_id(1)
    half = xs_ref.shape[1] // 2
    uw = u_ref.shape[-1] // 2

    @pl.when(t_idx == 0)
    def _():
        for hf in range(2):
            if has_x0:
                st_ref[:, hf * 2 * half:hf * 2 * half + half] = refs[7][:, hf * half:(hf + 1) * half]
                st_ref[:, hf * 2 * half + half:(hf + 1) * 2 * half] = refs[8][:, hf * half:(hf + 1) * half]
            else:
                st_ref[...] = jnp.zeros(st_ref.shape, F32)

    for t in range(tc):
        utm_ref[t * bg:(t + 1) * bg, :] = u_ref[:, t, :]
    u = utm_ref[...]
    ys = []
    for hf in range(2):
        xs_ref[...] = jnp.dot(u[:, hf * uw:(hf + 1) * uw].astype(BF16), bm_ref[hf],
                              preferred_element_type=F32)
        base = hf * 2 * half
        for c0 in range(0, half, cw):
            ar = a_ref[0:1, base + c0:base + c0 + cw]
            ai = a_ref[0:1, base + half + c0:base + half + c0 + cw]
            xr = st_ref[:, base + c0:base + c0 + cw]
            xi = st_ref[:, base + half + c0:base + half + c0 + cw]
            for t in range(tc):
                rows = slice(t * bg, (t + 1) * bg)
                nr = ar * xr - ai * xi + xs_ref[rows, c0:c0 + cw]
                ni = ar * xi + ai * xr + xs_ref[rows, half + c0:half + c0 + cw]
                xs_ref[rows, c0:c0 + cw] = nr
                xs_ref[rows, half + c0:half + c0 + cw] = ni
                xr, xi = nr, ni
            st_ref[:, base + c0:base + c0 + cw] = xr
            st_ref[:, base + half + c0:base + half + c0 + cw] = xi
        ys.append(jnp.dot(xs_ref[...].astype(BF16), cm_ref[hf], preferred_element_type=F32))
    y = jnp.concatenate(ys, axis=1) + d_ref[...] * u
    y = 0.5 * y * (1.0 + jnp.tanh(math.sqrt(2.0 / math.pi) * (y + 0.044715 * (y * y * y))))
    gate = jnp.dot(y.astype(BF16), wg_ref[...], preferred_element_type=F32) + bgl_ref[...]
    utm_ref[...] = y * _sigmoid(gate)
    for t in range(tc):
        o_ref[:, t, :] = utm_ref[t * bg:(t + 1) * bg, :]

    @pl.when(t_idx == nt - 1)
    def _():
        for hf in range(2):
            sre_ref[:, hf * half:(hf + 1) * half] = st_ref[:, hf * 2 * half:hf * 2 * half + half]
            sim_ref[:, hf * half:(hf + 1) * half] = st_ref[:, hf * 2 * half + half:(hf + 1) * 2 * half]


def _s5_call(u_src, col_block, mats, x0, *, tc, cw):
    bg, seq, _ = u_src.shape
    assert seq % tc == 0
    nt = seq // tc
    bmat, cmat, a_flat, d_row, w_glu, b_glu = mats
    width = d_row.shape[1]
    nstate = a_flat.shape[1] // 2
    whole = lambda a: pl.BlockSpec(a.shape, lambda g, t: (0,) * a.ndim)
    in_specs = [pl.BlockSpec((bg, tc, width), lambda g, t: (0, t, col_block))]
    in_specs += [whole(m) for m in mats]
    args = [u_src] + list(mats)
    st_spec = pl.BlockSpec((bg, nstate), lambda g, t: (0, 0))
    if x0 is not None:
        in_specs += [st_spec, st_spec]
        args += list(x0)
    scratch = [pltpu.VMEM((bg * tc, width), F32), pltpu.VMEM((bg * tc, nstate), F32),
               pltpu.VMEM((bg, 2 * nstate), F32)]
    return pl.pallas_call(
        functools.partial(_s5_kernel, bg=bg, tc=tc, nt=nt, cw=cw, has_x0=x0 is not None),
        grid=(1, nt),
        in_specs=in_specs,
        out_specs=[pl.BlockSpec((bg, tc, width), lambda g, t: (0, t, 0)), st_spec, st_spec],
        out_shape=[jax.ShapeDtypeStruct((bg, seq, width), F32),
                   jax.ShapeDtypeStruct((bg, nstate), F32),
                   jax.ShapeDtypeStruct((bg, nstate), F32)],
        scratch_shapes=scratch,
        compiler_params=_cparams("arbitrary", "arbitrary"),
        name="s5",
    )(*args)


def _s5_matrices(a_re, a_im, bb_re, bb_im, c_re, c_im, d, w_glu, b_glu):
    g, n, p = bb_re.shape
    gh = g // 2
    eye = jnp.eye(gh, dtype=F32)

    def block_diag(t):
        return (eye[:, None, :, None] * t[:, :, None, :]).reshape(gh * t.shape[1], gh * t.shape[2])

    def in_mat(bb):
        return block_diag(bb.transpose(0, 2, 1))

    def out_mat(cc):
        return block_diag(cc.transpose(0, 2, 1))

    bmat = jnp.stack([jnp.concatenate([in_mat(bb_re[h * gh:(h + 1) * gh]),
                                       in_mat(bb_im[h * gh:(h + 1) * gh])], axis=1)
                      for h in range(2)]).astype(BF16)
    cmat = jnp.stack([jnp.concatenate([out_mat(c_re[h * gh:(h + 1) * gh]),
                                       out_mat(-c_im[h * gh:(h + 1) * gh])], axis=0)
                      for h in range(2)]).astype(BF16)
    a_flat = jnp.concatenate([jnp.concatenate([a_re[h * gh:(h + 1) * gh].reshape(1, gh * n),
                                               a_im[h * gh:(h + 1) * gh].reshape(1, gh * n)], axis=1)
                              for h in range(2)], axis=1)
    return (bmat, cmat, a_flat, d.reshape(1, g * p), w_glu.astype(BF16), b_glu.reshape(1, -1))


def _xattn_kernel(*refs, n_pre):
    x_ref = refs[0]
    g_ref, wq_ref, wo_ref, k_ref, v_ref, y_ref, kt_ref, vt_ref = refs[1 + 2 * n_pre:]

    @pl.when(pl.program_id(1) == 0)
    def _():
        kt_ref[...] = jnp.transpose(k_ref[...], (1, 0, 2)).astype(BF16)
        vt_ref[...] = jnp.transpose(v_ref[...], (1, 0, 2)).astype(BF16)

    x = x_ref[...]
    for a_ref, w_ref in zip(refs[1:1 + n_pre], refs[1 + n_pre:1 + 2 * n_pre]):
        x = x + jnp.dot(a_ref[...].astype(BF16), w_ref[...], preferred_element_type=F32)
    q = jnp.dot(_rms(x, g_ref[...]).astype(BF16), wq_ref[...], preferred_element_type=F32).astype(BF16)
    hd = q.shape[1] // N_HEADS
    outs = []
    for h in range(N_HEADS):
        s = lax.dot_general(q[:, h * hd:(h + 1) * hd], kt_ref[h], _NT,
                            preferred_element_type=F32) * (hd ** -0.5)
        p = jnp.exp(s - jnp.max(s, axis=-1, keepdims=True))
        p = p / jnp.sum(p, axis=-1, keepdims=True)
        outs.append(jnp.dot(p.astype(BF16), vt_ref[h], preferred_element_type=F32).astype(BF16))
    y_ref[...] = x + jnp.dot(jnp.concatenate(outs, axis=1), wo_ref[...], preferred_element_type=F32)


def _xattn_call(x, pre_a, pre_w, g, w_q, w_o, mem_k, mem_v, layer, *, tq):
    bsz, seq, d = x.shape
    n_mem, nh, hd = mem_k.shape[2:]
    assert seq % tq == 0
    kv_spec = pl.BlockSpec((None, None, n_mem, nh, hd), lambda b, t: (layer, b, 0, 0, 0))
    rows = lambda a: pl.BlockSpec((None, tq, a.shape[-1]), lambda b, t: (b, t, 0))
    params = [_whole(w) for w in (*pre_w, g, w_q, w_o)]
    return pl.pallas_call(
        functools.partial(_xattn_kernel, n_pre=len(pre_a)),
        grid=(bsz, seq // tq),
        in_specs=[rows(x)] + [rows(a) for a in pre_a] + [spec for _, spec in params] + [kv_spec, kv_spec],
        out_specs=rows(x),
        out_shape=jax.ShapeDtypeStruct((bsz, seq, d), F32),
        scratch_shapes=[pltpu.VMEM((nh, n_mem, hd), BF16), pltpu.VMEM((nh, n_mem, hd), BF16)],
        compiler_params=_cparams("arbitrary", "arbitrary"),
        name="xattn",
    )(x, *pre_a, *[w for w, _ in params], mem_k, mem_v)


def _attn_rows_kernel(q_ref, k_ref, v_ref, o_ref, pad_ref, *, rows_in):
    nb, rows, d = pad_ref.shape
    nblk = d // 128
    half_blk = nblk // 2
    lanes = k_ref.shape[1]
    lane_blk = lax.broadcasted_iota(jnp.int32, (1, lanes), 1) % nblk
    row_head = lax.broadcasted_iota(jnp.int32, (N_HEADS * rows, 1), 0) // rows
    live = lane_blk == row_head
    scale = (d // N_HEADS) ** -0.5
    for i in range(nb):
        pad_ref[i] = jnp.zeros((rows, d), F32)
        pad_ref[i, 0:rows_in, :] = q_ref[i]
        q = pad_ref[i]
        qx = jnp.concatenate([q[:, j * 128:(j + 1) * 128] for j in range(nblk)], axis=0)
        g = lax.dot_general(qx.astype(BF16), k_ref[i].astype(BF16), _NT, preferred_element_type=F32)
        s = jnp.concatenate(
            [g[2 * h * rows:(2 * h + 1) * rows]
             + pltpu.roll(g[(2 * h + 1) * rows:(2 * h + 2) * rows], lanes - half_blk, axis=1)
             for h in range(N_HEADS)], axis=0) * scale
        s = jnp.where(live, s, -1e30)
        e = jnp.exp(s - jnp.max(s, axis=-1, keepdims=True))
        p = e / jnp.sum(e, axis=-1, keepdims=True)
        px = jnp.concatenate(
            [blk for h in range(N_HEADS)
             for blk in (p[h * rows:(h + 1) * rows], pltpu.roll(p[h * rows:(h + 1) * rows], half_blk, axis=1))],
            axis=0)
        o = jnp.dot(px.astype(BF16), v_ref[i].astype(BF16), preferred_element_type=F32)
        o = jnp.concatenate([o[j * rows:(j + 1) * rows] for j in range(nblk)], axis=1)
        o_ref[i] = o[0:rows_in].astype(o_ref.dtype)


def _attn_rows_call(q, mem_k, mem_v, layer, *, nb):
    bsz, seq, d = q.shape
    depth, _, n_mem, nh, hd = mem_k.shape
    assert bsz % nb == 0 and hd == 256 and nh == N_HEADS and seq <= 16
    as_rows = lambda a: a.reshape(depth, bsz, n_mem, nh, 2, 128).transpose(0, 1, 2, 4, 3, 5).reshape(
        depth, bsz, n_mem * 2 * nh, 128)
    kv_spec = pl.BlockSpec((None, nb, n_mem * 2 * nh, 128), lambda b: (layer, b, 0, 0))
    return pl.pallas_call(
        functools.partial(_attn_rows_kernel, rows_in=seq),
        grid=(bsz // nb,),
        in_specs=[pl.BlockSpec((nb, seq, d), lambda b: (b, 0, 0)), kv_spec, kv_spec],
        out_specs=pl.BlockSpec((nb, seq, d), lambda b: (b, 0, 0)),
        out_shape=jax.ShapeDtypeStruct((bsz, seq, d), BF16),
        scratch_shapes=[pltpu.VMEM((nb, 16, d), F32)],
        compiler_params=_cparams("arbitrary"),
        name="mem_attn_rows",
    )(q, as_rows(mem_k), as_rows(mem_v))


FFN_COLS = 256


def _ffn_kernel(*refs, tm, ts, hs, f_dim, has_hist, has_final):
    n_in = 6 + int(has_hist) + int(has_final)
    x_ref, g_ref, wup_ref, cw_ref, cb_ref, wdn_ref = refs[:6]
    y_ref, state_ref = refs[n_in:n_in + 2]
    gated_ref, hist_ref = refs[n_in + 2:n_in + 4]
    t_idx = pl.program_id(1)

    @pl.when(t_idx == 0)
    def _():
        hist_ref[...] = jnp.zeros(hist_ref.shape, F32)
        if has_hist:
            hist_ref[hs - 2 * ts:hs, :] = refs[6][...]

    x = x_ref[...]
    h = _rms(x, g_ref[...]).astype(BF16)
    row = lax.broadcasted_iota(jnp.int32, (tm, 1), 0)
    for c in range(f_dim // FFN_COLS):
        conv = []
        for part in range(2):
            cols = slice(part * f_dim + c * FFN_COLS, part * f_dim + (c + 1) * FFN_COLS)
            u = jnp.dot(h, wup_ref[:, cols], preferred_element_type=F32)
            if ts == 1:
                prev2, prev1 = hist_ref[hs - 2:hs - 1, cols], hist_ref[hs - 1:hs, cols]
                m1 = jnp.where(row == 0, prev1, pltpu.roll(u, 1, axis=0))
                m2 = jnp.where(row == 0, prev2, jnp.where(row == 1, prev1, pltpu.roll(u, 2, axis=0)))
            else:
                ext = jnp.concatenate([hist_ref[hs - 2 * ts:hs, cols], u], axis=0)
                m2, m1 = ext[0:tm], ext[ts:ts + tm]
            conv.append(cb_ref[:, cols] + cw_ref[0:1, cols] * m2 + cw_ref[1:2, cols] * m1
                        + cw_ref[2:3, cols] * u)
            hist_ref[:, cols] = u[tm - hs:tm]
        gated_ref[:, c * FFN_COLS:(c + 1) * FFN_COLS] = (_silu(conv[0]) * conv[1]).astype(BF16)
    out = x + jnp.dot(gated_ref[...], wdn_ref[...], preferred_element_type=F32)
    if has_final:
        out = _rms(out, refs[n_in - 1][...])
    y_ref[...] = out

    @pl.when(t_idx == pl.num_programs(1) - 1)
    def _():
        state_ref[...] = hist_ref[hs - 2 * ts:hs, :]


def _ffn_call(x, g, w_up, conv_w, conv_b, w_down, hist0, g_final, *, tm, ts):
    ngrp, rows, d = x.shape
    params = [_whole(a) for a in (g, w_up, conv_w, conv_b, w_down)]
    f2 = params[1][0].shape[-1]
    f_dim = f2 // 2
    hs = max(8, 2 * ts)
    assert rows % tm == 0 and tm >= hs and f_dim % FFN_COLS == 0 and (ts == 1 or ts % 8 == 0)
    args = [x] + [a for a, _ in params]
    in_specs = [pl.BlockSpec((None, tm, d), lambda s, t: (s, t, 0))] + [spec for _, spec in params]
    st_spec = pl.BlockSpec((None, 2 * ts, f2), lambda s, t: (s, 0, 0))
    if hist0 is not None:
        in_specs.append(st_spec)
        args.append(hist0)
    if g_final is not None:
        in_specs.append(_whole(g_final)[1])
        args.append(g_final)
    return pl.pallas_call(
        functools.partial(_ffn_kernel, tm=tm, ts=ts, hs=hs, f_dim=f_dim,
                          has_hist=hist0 is not None, has_final=g_final is not None),
        grid=(ngrp, rows // tm),
        in_specs=in_specs,
        out_specs=[pl.BlockSpec((None, tm, d), lambda s, t: (s, t, 0)), st_spec],
        out_shape=[jax.ShapeDtypeStruct((ngrp, rows, d), F32),
                   jax.ShapeDtypeStruct((ngrp, 2 * ts, f2), F32)],
        scratch_shapes=[pltpu.VMEM((tm, f_dim), BF16), pltpu.VMEM((hs, f2), F32)],
        compiler_params=_cparams("arbitrary", "arbitrary"),
        name="conv_ffn",
    )(*args)


def _trunk(x, mem_k, mem_v, states, p, *, prompt):
    bsz, seq, d = x.shape
    depth = p["norm_mix"].shape[0]
    m = bsz * seq
    x2 = x.reshape(m, d)
    new = {"hgrn": [], "s5_re": [], "s5_im": [], "gla": [], "conv": []}
    if prompt:
        gla_tiles = dict(tb=256, chunk=64, nb=1)
        attn_tq = 512
    else:
        gla_tiles = dict(tb=seq, chunk=16, nb=8)
        attn_tq = seq
    for l in range(depth):
        g_mix = _pick(p["norm_mix"], l)
        if l % 2 == 0:
            e = l // 2
            kw = N_HEADS * HEAD_DK
            x3 = x2.reshape(bsz, seq, d)
            hgrn_par = [p["hgrn_lb"], p["hgrn_gnorm"][e].reshape(1, -1)]
            mats = p["s5_mats"][e]
            if prompt:
                o_a, s_a, u = _gla_call("hgrn", (x3, g_mix, _pick(p["w_in_ab"], e)), hgrn_par, None,
                                        layer=l, dv=kw // N_HEADS, extra_widths=(kw,), **gla_tiles)
                o_b, sr, si = _s5_call(u, 0, mats, None, tc=64, cw=512)
            else:
                proj = _norm_proj(x2, g_mix, _pick(p["w_in_ab"], e)).reshape(bsz, seq, -1)
                o_a, s_a = _gla_call("hgrn", proj, hgrn_par, states["hgrn"][e],
                                     layer=l, dv=kw // N_HEADS, **gla_tiles)
                x0 = (states["s5_re"][e].reshape(bsz, -1), states["s5_im"][e].reshape(bsz, -1))
                o_b, sr, si = _s5_call(proj, 4, mats, x0, tc=seq, cw=128)
            mixed = [o_a, o_b]
            mixed_w = [_pick(p["w_out_ab"], e, rows=kw, row_block=0), _pick(p["w_out_ab"], e, rows=kw, row_block=1)]
            new["hgrn"].append(s_a)
            new["s5_re"].append(sr.reshape(bsz, -1, S5_STATE))
            new["s5_im"].append(si.reshape(bsz, -1, S5_STATE))
        else:
            o_idx = l // 2
            if prompt:
                src_c = (x2.reshape(bsz, seq, d), g_mix, _pick(p["w_in_c"], o_idx))
            else:
                src_c = _norm_proj(x2, g_mix, _pick(p["w_in_c"], o_idx)).reshape(bsz, seq, -1)
            o_c, s_c = _gla_call(
                "gla", src_c,
                [p["gla_w_gate"][o_idx], p["gla_b_gate"][o_idx].reshape(1, -1),
                 p["gla_gnorm"][o_idx].reshape(1, -1)],
                None if states is None else states["gla"][o_idx],
                layer=l, dv=d // N_HEADS, **gla_tiles)
            mixed, mixed_w = [o_c], [_pick(p["w_out_c"], o_idx)]
            new["gla"].append(s_c)
        g_cross, w_q, w_o = _pick(p["norm_cross"], l), _pick(p["xa_w_q"], l), _pick(p["xa_w_o"], l)
        if prompt:
            x2 = _xattn_call(x2.reshape(bsz, seq, d), mixed, mixed_w, g_cross, w_q, w_o,
                             mem_k, mem_v, l, tq=attn_tq).reshape(m, d)
        else:
            x2 = _proj_res(x2, [a.reshape(m, -1) for a in mixed], mixed_w)
            q = _norm_proj(x2, g_cross, w_q)
            o_x = _attn_rows_call(q.reshape(bsz, seq, d), mem_k, mem_v, l, nb=4)
            x2 = _proj_res(x2, [o_x.reshape(m, d)], [w_o])
        g_final = p["norm_final"] if l == depth - 1 else None
        ffn_w = tuple(_pick(p[name], l) for name in
                      ("norm_ffn", "ffn_w_up", "ffn_conv_w", "ffn_conv_b", "ffn_w_down"))
        if prompt:
            y, cst = _ffn_call(x2.reshape(bsz, seq, d), *ffn_w, None, g_final, tm=1024, ts=1)
            x2 = y.reshape(m, d)
        else:
            xt = x2.reshape(bsz, seq, d).transpose(1, 0, 2).reshape(1, m, d)
            hist0 = states["conv"][l].transpose(1, 0, 2).reshape(1, 2 * bsz, -1)
            y, cst = _ffn_call(xt, *ffn_w, hist0, g_final, tm=m, ts=bsz)
            x2 = y.reshape(seq, bsz, d).transpose(1, 0, 2).reshape(m, d)
            cst = cst.reshape(2, bsz, -1).transpose(1, 0, 2)
        new["conv"].append(cst)
    return x2.reshape(bsz, seq, d), new


def kernel(x_prompt, x_sample, mem_prompt, cache_mem_k, cache_mem_v, state_hgrn, state_s5_re, state_s5_im, state_gla, state_ffn_conv, norm_mix, norm_cross, norm_mem, norm_ffn, norm_final, w_in_ab, hgrn_lb, hgrn_gnorm, s5_lam_re, s5_lam_im, s5_log_step, s5_b_re, s5_b_im, s5_c_re, s5_c_im, s5_d, s5_w_glu, s5_b_glu, w_out_ab, w_in_c, gla_w_gate_up, gla_b_gate, gla_gnorm, w_out_c, xa_w_q, xa_w_kv, xa_w_o, ffn_w_up, ffn_conv_w, ffn_conv_b, ffn_w_down):
    depth, d = norm_mix.shape

    gla_cols = w_in_c.shape[2]
    gate_rank = gla_w_gate_up.shape[1]
    pad_c = (-gla_cols) % 128
    w_in_c_p = jnp.pad(w_in_c, ((0, 0), (0, 0), (0, pad_c))).astype(BF16)
    gla_w_gate = jnp.pad(gla_w_gate_up, ((0, 0), (0, 128 - gate_rank), (0, 0))).astype(BF16)

    s5_mats = []
    for e in range(s5_lam_re.shape[0]):
        a_re, a_im, bb_re, bb_im = _s5_prep(s5_lam_re[e], s5_lam_im[e], s5_log_step[e],
                                            s5_b_re[e], s5_b_im[e])
        s5_mats.append(_s5_matrices(a_re, a_im, bb_re, bb_im, s5_c_re[e], s5_c_im[e], s5_d[e],
                                    s5_w_glu[e], s5_b_glu[e]))

    row = lambda a: a.reshape(a.shape[0], 1, a.shape[1])
    p = dict(norm_mix=row(norm_mix), norm_cross=row(norm_cross), norm_ffn=row(norm_ffn),
             norm_final=norm_final.reshape(1, d),
             w_in_ab=w_in_ab.astype(BF16), hgrn_lb=hgrn_lb, hgrn_gnorm=hgrn_gnorm, s5_mats=s5_mats,
             w_out_ab=w_out_ab.astype(BF16), w_in_c=w_in_c_p, gla_w_gate=gla_w_gate,
             gla_b_gate=gla_b_gate, gla_gnorm=gla_gnorm, w_out_c=w_out_c.astype(BF16),
             xa_w_q=xa_w_q.astype(BF16), xa_w_o=xa_w_o.astype(BF16),
             ffn_w_up=ffn_w_up.astype(BF16), ffn_conv_w=ffn_conv_w, ffn_conv_b=row(ffn_conv_b),
             ffn_w_down=ffn_w_down.astype(BF16))

    mem_k_p, mem_v_p = _mem_kv(mem_prompt, norm_mem.reshape(depth, 1, d), xa_w_kv.astype(BF16))
    y_prompt, st_p = _trunk(x_prompt, mem_k_p, mem_v_p, None, p, prompt=True)

    states = dict(hgrn=state_hgrn, s5_re=state_s5_re, s5_im=state_s5_im, gla=state_gla,
                  conv=state_ffn_conv)
    y_sample, st_s = _trunk(x_sample, cache_mem_k, cache_mem_v, states, p, prompt=False)

    stack = lambda xs: xs[0][None] if len(xs) == 1 else jnp.stack(xs)
    return (y_prompt, y_sample,
            stack(st_p["hgrn"]), stack(st_p["s5_re"]), stack(st_p["s5_im"]), stack(st_p["gla"]),
            mem_k_p, mem_v_p, stack(st_p["conv"]),
            stack(st_s["hgrn"]), stack(st_s["s5_re"]), stack(st_s["s5_im"]), stack(st_s["gla"]),
            stack(st_s["conv"]))
```

```python
import functools
import math

import jax
import jax.numpy as jnp
from jax import lax
from jax.experimental import pallas as pl
from jax.experimental.pallas import tpu as pltpu

F32 = jnp.float32
BF16 = jnp.bfloat16

EPS = 1e-6
S5_MAX_RE = -1e-4
GLA_GATE_TAU = 16.0
N_HEADS = 4
HEAD_DK = 128
S5_GROUP = 16
S5_STATE = 64
SUB_BLOCK = 16
GLA_SAFE_DECAY = 64.0
VMEM_LIMIT = 56 * 1024 * 1024

_NT = (((1,), (1,)), ((), ()))
_TN = (((0,), (0,)), ((), ()))


def _cparams(*sem):
    return pltpu.CompilerParams(dimension_semantics=sem, vmem_limit_bytes=VMEM_LIMIT)


def _rms(x, g):
    return x * lax.rsqrt(jnp.mean(x * x, axis=-1, keepdims=True) + EPS) * g


def _sigmoid(x):
    return 1.0 / (1.0 + jnp.exp(-x))


def _silu(x):
    return x * _sigmoid(x)


def _row_tile(rows, want):
    t = min(rows, want)
    assert rows % t == 0, (rows, t)
    return t


def _pick(a, layer, rows=None, row_block=0, cols=None, col_block=0):
    block = (None, rows or a.shape[1], cols or a.shape[2])
    return a, pl.BlockSpec(block, lambda *_: (layer, row_block, col_block))


def _whole(a):
    return a if isinstance(a, tuple) else (a, pl.BlockSpec(a.shape, lambda *_: (0,) * a.ndim))


def _norm_proj_kernel(x_ref, g_ref, w_ref, o_ref):
    h = _rms(x_ref[...], g_ref[...]).astype(BF16)
    n = o_ref.shape[1]
    for c0 in range(0, n, 512):
        cw = min(512, n - c0)
        o_ref[:, c0:c0 + cw] = jnp.dot(h, w_ref[:, c0:c0 + cw], preferred_element_type=F32)


def _norm_proj(x, g, w, *, tm=512):
    m, k = x.shape
    (g, g_spec), (w, w_spec) = _whole(g), _whole(w)
    n = w.shape[-1]
    tm = _row_tile(m, tm)
    return pl.pallas_call(
        _norm_proj_kernel,
        grid=(m // tm,),
        in_specs=[pl.BlockSpec((tm, k), lambda i: (i, 0)), g_spec, w_spec],
        out_specs=pl.BlockSpec((tm, n), lambda i: (i, 0)),
        out_shape=jax.ShapeDtypeStruct((m, n), F32),
        compiler_params=_cparams("arbitrary"),
        name="norm_proj",
    )(x, g, w)


def _mem_kv_kernel(x_ref, g_ref, w_ref, k_ref, v_ref):
    nb, n_mem, d = x_ref.shape
    h = _rms(x_ref[...].reshape(nb * n_mem, d), g_ref[...]).astype(BF16)
    hd = d // N_HEADS
    for o_ref, base in ((k_ref, 0), (v_ref, d)):
        for hh in range(N_HEADS):
            kv = jnp.dot(h, w_ref[:, base + hh * hd:base + (hh + 1) * hd], preferred_element_type=F32)
            for i in range(nb):
                o_ref[i, :, hh, :] = kv[i * n_mem:(i + 1) * n_mem]


def _mem_kv(mem, g, w, *, nb=2):
    bsz, n_mem, d = mem.shape
    depth = w.shape[0]
    assert bsz % nb == 0
    out_spec = pl.BlockSpec((None, nb, n_mem, N_HEADS, d // N_HEADS), lambda l, b: (l, b, 0, 0, 0))
    out_shape = jax.ShapeDtypeStruct((depth, bsz, n_mem, N_HEADS, d // N_HEADS), F32)
    return pl.pallas_call(
        _mem_kv_kernel,
        grid=(depth, bsz // nb),
        in_specs=[pl.BlockSpec((nb, n_mem, d), lambda l, b: (b, 0, 0)),
                  pl.BlockSpec((None, 1, d), lambda l, b: (l, 0, 0)),
                  pl.BlockSpec((None, d, 2 * d), lambda l, b: (l, 0, 0))],
        out_specs=[out_spec, out_spec],
        out_shape=[out_shape, out_shape],
        compiler_params=_cparams("arbitrary", "arbitrary"),
        name="mem_kv",
    )(mem, g, w)


def _proj_res_kernel(res_ref, *refs, n_in):
    acc = res_ref[...]
    for a_ref, w_ref in zip(refs[:n_in], refs[n_in:2 * n_in]):
        acc = acc + jnp.dot(a_ref[...].astype(BF16), w_ref[...], preferred_element_type=F32)
    refs[2 * n_in][...] = acc


def _proj_res(res, a_list, w_list, *, tm=512):
    m, n = res.shape
    tm = _row_tile(m, tm)
    n_in = len(a_list)
    w_list = [_whole(w) for w in w_list]
    in_specs = [pl.BlockSpec((tm, n), lambda i: (i, 0))]
    in_specs += [pl.BlockSpec((tm, a.shape[1]), lambda i: (i, 0)) for a in a_list]
    in_specs += [spec for _, spec in w_list]
    return pl.pallas_call(
        functools.partial(_proj_res_kernel, n_in=n_in),
        grid=(m // tm,),
        in_specs=in_specs,
        out_specs=pl.BlockSpec((tm, n), lambda i: (i, 0)),
        out_shape=jax.ShapeDtypeStruct((m, n), F32),
        compiler_params=_cparams("arbitrary"),
        name="proj_res",
    )(res, *a_list, *[w for w, _ in w_list])


def _cumsum_rows(x, c):
    n = x.shape[0]
    hi = x.astype(BF16)
    rest = x - hi.astype(F32)
    mid = rest.astype(BF16)
    lo = (rest - mid.astype(F32)).astype(BF16)
    r = lax.broadcasted_iota(jnp.int32, (n, n), 0)
    col = lax.broadcasted_iota(jnp.int32, (n, n), 1)
    tri = jnp.where((r >= col) & (r // c == col // c), 1.0, 0.0).astype(BF16)
    return jnp.dot(jnp.concatenate([tri, tri, tri], axis=1), jnp.concatenate([hi, mid, lo], axis=0),
                   preferred_element_type=F32)


def _gla_head(qh, kh, bh, vh, st_ref, small_decay, live_rows):
    c = qh.shape[0]
    sb = min(SUB_BLOCK, c)
    vb = vh.astype(BF16)
    st = st_ref[...]
    b_last = bh[c - 1:c, :]
    q_in = (qh * jnp.exp(bh)).astype(BF16)
    o_inter = lax.dot_general(q_in, st.astype(BF16), _NT, preferred_element_type=F32)
    if small_decay:
        k_up = kh * jnp.exp(-bh)
        a = lax.dot_general(q_in, k_up.astype(BF16), _NT, preferred_element_type=F32)
        causal = (lax.broadcasted_iota(jnp.int32, (c, c), 0) >= lax.broadcasted_iota(jnp.int32, (c, c), 1))
        kd = (k_up * jnp.exp(b_last)).astype(BF16)
        st_ref[...] = st * jnp.exp(b_last) + lax.dot_general(vb, kd, _TN, preferred_element_type=F32)
        return o_inter + jnp.dot(jnp.where(causal, a, 0.0).astype(BF16), vb, preferred_element_type=F32)
    kd = (kh * jnp.exp(b_last - bh)).astype(BF16)
    st_ref[...] = st * jnp.exp(b_last) + lax.dot_general(vb, kd, _TN, preferred_element_type=F32)
    rows = lax.broadcasted_iota(jnp.int32, (sb, 1), 0)
    parts = []
    for s in range(c // sb):
        r0 = s * sb
        qs, ks, bs, vs = qh[r0:r0 + sb], kh[r0:r0 + sb], bh[r0:r0 + sb], vh[r0:r0 + sb]
        acc = o_inter[r0:r0 + sb]
        if s > 0:
            ref_b = bh[r0 - 1:r0, :]
            qf = (qs * jnp.exp(bs - ref_b)).astype(BF16)
            kf = (kh[0:r0] * jnp.exp(ref_b - bh[0:r0])).astype(BF16)
            a_off = lax.dot_general(qf, kf, _NT, preferred_element_type=F32)
            acc = acc + jnp.dot(a_off.astype(BF16), vb[0:r0], preferred_element_type=F32)
        for j in range(max(0, min(sb, live_rows - r0))):
            w = jnp.exp(jnp.minimum(bs - bs[j:j + 1], 0.0)) * qs * ks[j:j + 1]
            col = jnp.where(rows >= j, jnp.sum(w, axis=-1, keepdims=True), 0.0)
            acc = acc + col * vs[j:j + 1]
        parts.append(acc)
    return parts[0] if len(parts) == 1 else jnp.concatenate(parts, axis=0)


def _gla_kernel(*refs, mode, layer, dv, nb, rows_in, chunk, n_chunks, has_s0, in_widths):
    n_act = 4 if mode == "hgrn" else 5
    n_src = 3 if in_widths else n_act
    n_par = 2 if mode == "hgrn" else 3
    n_in = n_src + n_par + (1 if has_s0 else 0)
    n_out = 2 + (len(in_widths) - n_act if in_widths else 0)
    ins, (o_ref, sout_ref), scr = refs[:n_in], refs[n_in:n_in + 2], refs[n_in + n_out:]
    pars = ins[n_src:n_src + n_par]
    st_ref = scr[0]
    pad_refs = scr[1:]
    t_idx = pl.program_id(1)
    padded = rows_in < chunk
    assert (padded and not in_widths) or nb == 1

    @pl.when(t_idx == 0)
    def _():
        for i in range(nb):
            for h in range(N_HEADS):
                if has_s0:
                    st_ref[i * N_HEADS + h] = ins[-1][i, h].T
                else:
                    st_ref[i * N_HEADS + h] = jnp.zeros(st_ref.shape[1:], F32)

    if in_widths:
        x_ref, gm_ref, win_ref = ins[:n_src]
        hx = _rms(x_ref[0], gm_ref[...]).astype(BF16)
        offs = [sum(in_widths[:i]) for i in range(len(in_widths))]
        ld = [jnp.dot(hx, win_ref[:, o:o + w], preferred_element_type=F32) for o, w in zip(offs, in_widths)]
        for extra_ref, extra in zip(refs[n_in + 2:n_in + n_out], ld[n_act:]):
            extra_ref[0] = extra
        ld = ld[:n_act]
    elif padded:
        for p_ref, a_ref in zip(pad_refs, ins[:n_act]):
            p_ref[...] = jnp.zeros(p_ref.shape, F32)
            for i in range(nb):
                p_ref[i * chunk:i * chunk + rows_in, :] = a_ref[i]
        ld = [a[...] for a in pad_refs]
    else:
        ld = [a[0] for a in ins[:n_act]]
    groups = nb if padded else n_chunks
    span = groups * chunk

    if mode == "hgrn":
        q_raw, f, v, gate = ld
        lb_ref, gn_ref = pars
        lbv = lb_ref[...]
        e = jnp.exp(lbv - jnp.max(lbv, axis=0, keepdims=True))
        lb = jnp.sum(e[0:layer + 1], axis=0, keepdims=True) / jnp.sum(e, axis=0, keepdims=True)
        forget = lb + (1.0 - lb) * _sigmoid(f)
        k = 1.0 - forget
        lg = jnp.log(forget)
        q = _silu(q_raw)
    else:
        q_raw, k, v, gate, gd = ld
        wg_ref, bg_ref, gn_ref = pars
        z = jnp.dot(gd.astype(BF16), wg_ref[...], preferred_element_type=F32) + bg_ref[...]
        lg = (jnp.minimum(z, 0.0) - jnp.log(1.0 + jnp.exp(-jnp.abs(z)))) / GLA_GATE_TAU
        q = q_raw * (HEAD_DK ** -0.5)
    if padded:
        live = lax.broadcasted_iota(jnp.int32, (span, 1), 0) % chunk < rows_in
        lg = jnp.where(live, lg, 0.0)
        k = jnp.where(live, k, 0.0)
    b = _cumsum_rows(lg, chunk)

    def piece(x, i, h, width):
        return x[i * chunk:(i + 1) * chunk, h * width:(h + 1) * width]

    def heads(small_decay):
        rows = []
        for i in range(groups):
            cols = []
            for h in range(N_HEADS):
                cols.append(_gla_head(piece(q, i, h, HEAD_DK), piece(k, i, h, HEAD_DK),
                                      piece(b, i, h, HEAD_DK), piece(v, i, h, dv),
                                      st_ref.at[(i if padded else 0) * N_HEADS + h], small_decay,
                                      rows_in if padded else chunk))
            rows.append(jnp.concatenate(cols, axis=1))
        return rows[0] if groups == 1 else jnp.concatenate(rows, axis=0)

    if chunk <= SUB_BLOCK:
        o_raw = heads(False)
    else:
        o_raw = lax.cond(jnp.min(b) >= -GLA_SAFE_DECAY,
                         functools.partial(heads, True), functools.partial(heads, False))
    o_all = jnp.concatenate(
        [_rms(o_raw[:, h * dv:(h + 1) * dv], gn_ref[...]) * _silu(gate[:, h * dv:(h + 1) * dv])
         for h in range(N_HEADS)], axis=1).astype(o_ref.dtype)
    if padded:
        for i in range(nb):
            o_ref[i] = o_all[i * chunk:i * chunk + rows_in]
    else:
        o_ref[0] = o_all

    @pl.when(t_idx == pl.num_programs(1) - 1)
    def _():
        for i in range(nb):
            for h in range(N_HEADS):
                sout_ref[i, h] = st_ref[i * N_HEADS + h].T


def _gla_call(mode, src, params, s0, *, layer, dv, tb, chunk, nb, extra_widths=()):
    fused_in = isinstance(src, tuple)
    proj = src[0] if fused_in else src
    bsz, seq, _ = proj.shape
    rows_in = min(tb, seq)
    if rows_in < chunk:
        assert seq == rows_in and bsz % nb == 0
        nt, n_chunks = 1, 1
    else:
        assert seq % tb == 0 and tb % chunk == 0 and nb == 1
        nt, n_chunks = seq // tb, tb // chunk
    kw, vw = N_HEADS * HEAD_DK, N_HEADS * dv

    def act(width, col_block):
        return pl.BlockSpec((nb, rows_in, width), lambda b, t: (b, t, col_block))

    def whole(a):
        return pl.BlockSpec(a.shape, lambda b, t: (0,) * a.ndim)

    if mode == "hgrn":
        act_specs = [act(kw, 0), act(kw, 1), act(vw, 2), act(vw, 3)]
        act_widths = [kw, kw, vw, vw]
    else:
        act_specs = [act(kw, 0), act(kw, 1), act(vw, kw * 2 // vw), act(vw, kw * 2 // vw + 1),
                     act(128, (2 * kw + 2 * vw) // 128)]
        act_widths = [kw, kw, vw, vw, 128]
    if fused_in:
        x, g_mix, w_in = src[0], _whole(src[1]), _whole(src[2])
        in_specs = [pl.BlockSpec((nb, rows_in, x.shape[-1]), lambda b, t: (b, t, 0)), g_mix[1], w_in[1]]
        args = [x, g_mix[0], w_in[0]]
    else:
        in_specs, args = act_specs, [proj] * len(act_specs)
    in_specs = in_specs + [whole(p) for p in params]
    args = args + list(params)
    state_spec = pl.BlockSpec((nb, N_HEADS, HEAD_DK, dv), lambda b, t: (b, 0, 0, 0))
    if s0 is not None:
        in_specs.append(state_spec)
        args.append(s0)
    scratch = [pltpu.VMEM((nb * N_HEADS, dv, HEAD_DK), F32)]
    if rows_in < chunk:
        scratch += [pltpu.VMEM((nb * chunk, w), F32) for w in act_widths]
    return pl.pallas_call(
        functools.partial(_gla_kernel, mode=mode, layer=layer, dv=dv, nb=nb, rows_in=rows_in,
                          chunk=chunk, n_chunks=n_chunks, has_s0=s0 is not None,
                          in_widths=tuple(act_widths) + tuple(extra_widths) if fused_in else None),
        grid=(bsz // nb, nt),
        in_specs=in_specs,
        out_specs=[pl.BlockSpec((nb, rows_in, vw), lambda b, t: (b, t, 0)), state_spec]
        + [pl.BlockSpec((nb, rows_in, w), lambda b, t: (b, t, 0)) for w in extra_widths],
        out_shape=[jax.ShapeDtypeStruct((bsz, seq, vw), BF16),
                   jax.ShapeDtypeStruct((bsz, N_HEADS, HEAD_DK, dv), F32)]
        + [jax.ShapeDtypeStruct((bsz, seq, w), F32) for w in extra_widths],
        scratch_shapes=scratch,
        compiler_params=_cparams("arbitrary", "arbitrary"),
        name="gla_" + mode,
    )(*args)


def _s5_prep_kernel(lre_ref, lim_ref, ls_ref, lre_x_ref, lim_x_ref, ls_x_ref, bre_ref, bim_ref,
                    are_ref, aim_ref, bbre_ref, bbim_ref):
    def disc(lre, lim, ls):
        lr = jnp.minimum(lre, S5_MAX_RE)
        dt = jnp.exp(ls)
        mag = jnp.exp(lr * dt)
        a_re = mag * jnp.cos(lim * dt)
        a_im = mag * jnp.sin(lim * dt)
        den = lr * lr + lim * lim
        z_re = ((a_re - 1.0) * lr + a_im * lim) / den
        z_im = (a_im * lr - (a_re - 1.0) * lim) / den
        return a_re, a_im, z_re, z_im

    a_re, a_im, _, _ = disc(lre_ref[...], lim_ref[...], ls_ref[...])
    are_ref[...] = a_re
    aim_ref[...] = a_im
    _, _, z_re, z_im = disc(lre_x_ref[...], lim_x_ref[...], ls_x_ref[...])
    bbre_ref[...] = z_re * bre_ref[...] - z_im * bim_ref[...]
    bbim_ref[...] = z_re * bim_ref[...] + z_im * bre_ref[...]


def _s5_prep(lam_re, lam_im, log_step, b_re, b_im):
    g, n = lam_re.shape
    p = b_re.shape[-1]
    ls = jnp.broadcast_to(log_step[:, None], (g, n))
    rep = lambda a: jnp.repeat(a, p, axis=1)
    outs = pl.pallas_call(
        _s5_prep_kernel,
        out_shape=[jax.ShapeDtypeStruct((g, n), F32)] * 2 + [jax.ShapeDtypeStruct((g, n * p), F32)] * 2,
        name="s5_prep",
    )(lam_re, lam_im, ls, rep(lam_re), rep(lam_im), rep(ls),
      b_re.reshape(g, n * p), b_im.reshape(g, n * p))
    a_re, a_im, bb_re, bb_im = outs
    return a_re, a_im, bb_re.reshape(g, n, p), bb_im.reshape(g, n, p)


def _s5_kernel(*refs, bg, tc, nt, cw, has_x0):
    n_in = 9 if has_x0 else 7
    u_ref, bm_ref, cm_ref, a_ref, d_ref, wg_ref, bgl_ref = refs[:7]
    o_ref, sre_ref, sim_ref = refs[n_in:n_in + 3]
    utm_ref, xs_ref, st_ref = refs[n_in + 3:]
    t_idx = pl.program_id(1)
    half = xs_ref.shape[1] // 2
    uw = u_ref.shape[-1] // 2

    @pl.when(t_idx == 0)
    def _():
        for hf in range(2):
            if has_x0:
                st_ref[:, hf * 2 * half:hf * 2 * half + half] = refs[7][:, hf * half:(hf + 1) * half]
                st_ref[:, hf * 2 * half + half:(hf + 1) * 2 * half] = refs[8][:, hf * half:(hf + 1) * half]
            else:
                st_ref[...] = jnp.zeros(st_ref.shape, F32)

    for t in range(tc):
        utm_ref[t * bg:(t + 1) * bg, :] = u_ref[:, t, :]
    u = utm_ref[...]
    ys = []
    for hf in range(2):
        xs_ref[...] = jnp.dot(u[:, hf * uw:(hf + 1) * uw].astype(BF16), bm_ref[hf],
                              preferred_element_type=F32)
        base = hf * 2 * half
        for c0 in range(0, half, cw):
            ar = a_ref[0:1, base + c0:base + c0 + cw]
            ai = a_ref[0:1, base + half + c0:base + half + c0 + cw]
            xr = st_ref[:, base + c0:base + c0 + cw]
            xi = st_ref[:, base + half + c0:base + half + c0 + cw]
            for t in range(tc):
                rows = slice(t * bg, (t + 1) * bg)
                nr = ar * xr - ai * xi + xs_ref[rows, c0:c0 + cw]
                ni = ar * xi + ai * xr + xs_ref[rows, half + c0:half + c0 + cw]
                xs_ref[rows, c0:c0 + cw] = nr
                xs_ref[rows, half + c0:half + c0 + cw] = ni
                xr, xi = nr, ni
            st_ref[:, base + c0:base + c0 + cw] = xr
            st_ref[:, base + half + c0:base + half + c0 + cw] = xi
        ys.append(jnp.dot(xs_ref[...].astype(BF16), cm_ref[hf], preferred_element_type=F32))
    y = jnp.concatenate(ys, axis=1) + d_ref[...] * u
    y = 0.5 * y * (1.0 + jnp.tanh(math.sqrt(2.0 / math.pi) * (y + 0.044715 * (y * y * y))))
    gate = jnp.dot(y.astype(BF16), wg_ref[...], preferred_element_type=F32) + bgl_ref[...]
    utm_ref[...] = y * _sigmoid(gate)
    for t in range(tc):
        o_ref[:, t, :] = utm_ref[t * bg:(t + 1) * bg, :]

    @pl.when(t_idx == nt - 1)
    def _():
        for hf in range(2):
            sre_ref[:, hf * half:(hf + 1) * half] = st_ref[:, hf * 2 * half:hf * 2 * half + half]
            sim_ref[:, hf * half:(hf + 1) * half] = st_ref[:, hf * 2 * half + half:(hf + 1) * 2 * half]


def _s5_call(u_src, col_block, mats, x0, *, tc, cw):
    bg, seq, _ = u_src.shape
    assert seq % tc == 0
    nt = seq // tc
    bmat, cmat, a_flat, d_row, w_glu, b_glu = mats
    width = d_row.shape[1]
    nstate = a_flat.shape[1] // 2
    whole = lambda a: pl.BlockSpec(a.shape, lambda g, t: (0,) * a.ndim)
    in_specs = [pl.BlockSpec((bg, tc, width), lambda g, t: (0, t, col_block))]
    in_specs += [whole(m) for m in mats]
    args = [u_src] + list(mats)
    st_spec = pl.BlockSpec((bg, nstate), lambda g, t: (0, 0))
    if x0 is not None:
        in_specs += [st_spec, st_spec]
        args += list(x0)
    scratch = [pltpu.VMEM((bg * tc, width), F32), pltpu.VMEM((bg * tc, nstate), F32),
               pltpu.VMEM((bg, 2 * nstate), F32)]
    return pl.pallas_call(
        functools.partial(_s5_kernel, bg=bg, tc=tc, nt=nt, cw=cw, has_x0=x0 is not None),
        grid=(1, nt),
        in_specs=in_specs,
        out_specs=[pl.BlockSpec((bg, tc, width), lambda g, t: (0, t, 0)), st_spec, st_spec],
        out_shape=[jax.ShapeDtypeStruct((bg, seq, width), F32),
                   jax.ShapeDtypeStruct((bg, nstate), F32),
                   jax.ShapeDtypeStruct((bg, nstate), F32)],
        scratch_shapes=scratch,
        compiler_params=_cparams("arbitrary", "arbitrary"),
        name="s5",
    )(*args)


def _s5_matrices(a_re, a_im, bb_re, bb_im, c_re, c_im, d, w_glu, b_glu):
    g, n, p = bb_re.shape
    gh = g // 2
    eye = jnp.eye(gh, dtype=F32)

    def block_diag(t):
        return (eye[:, None, :, None] * t[:, :, None, :]).reshape(gh * t.shape[1], gh * t.shape[2])

    def in_mat(bb):
        return block_diag(bb.transpose(0, 2, 1))

    def out_mat(cc):
        return block_diag(cc.transpose(0, 2, 1))

    bmat = jnp.stack([jnp.concatenate([in_mat(bb_re[h * gh:(h + 1) * gh]),
                                       in_mat(bb_im[h * gh:(h + 1) * gh])], axis=1)
                      for h in range(2)]).astype(BF16)
    cmat = jnp.stack([jnp.concatenate([out_mat(c_re[h * gh:(h + 1) * gh]),
                                       out_mat(-c_im[h * gh:(h + 1) * gh])], axis=0)
                      for h in range(2)]).astype(BF16)
    a_flat = jnp.concatenate([jnp.concatenate([a_re[h * gh:(h + 1) * gh].reshape(1, gh * n),
                                               a_im[h * gh:(h + 1) * gh].reshape(1, gh * n)], axis=1)
                              for h in range(2)], axis=1)
    return (bmat, cmat, a_flat, d.reshape(1, g * p), w_glu.astype(BF16), b_glu.reshape(1, -1))


def _xattn_kernel(*refs, n_pre):
    x_ref = refs[0]
    g_ref, wq_ref, wo_ref, k_ref, v_ref, y_ref, kt_ref, vt_ref = refs[1 + 2 * n_pre:]

    @pl.when(pl.program_id(1) == 0)
    def _():
        kt_ref[...] = jnp.transpose(k_ref[...], (1, 0, 2)).astype(BF16)
        vt_ref[...] = jnp.transpose(v_ref[...], (1, 0, 2)).astype(BF16)

    x = x_ref[...]
    for a_ref, w_ref in zip(refs[1:1 + n_pre], refs[1 + n_pre:1 + 2 * n_pre]):
        x = x + jnp.dot(a_ref[...].astype(BF16), w_ref[...], preferred_element_type=F32)
    q = jnp.dot(_rms(x, g_ref[...]).astype(BF16), wq_ref[...], preferred_element_type=F32).astype(BF16)
    hd = q.shape[1] // N_HEADS
    outs = []
    for h in range(N_HEADS):
        s = lax.dot_general(q[:, h * hd:(h + 1) * hd], kt_ref[h], _NT,
                            preferred_element_type=F32) * (hd ** -0.5)
        p = jnp.exp(s - jnp.max(s, axis=-1, keepdims=True))
        p = p / jnp.sum(p, axis=-1, keepdims=True)
        outs.append(jnp.dot(p.astype(BF16), vt_ref[h], preferred_element_type=F32).astype(BF16))
    y_ref[...] = x + jnp.dot(jnp.concatenate(outs, axis=1), wo_ref[...], preferred_element_type=F32)


def _xattn_call(x, pre_a, pre_w, g, w_q, w_o, mem_k, mem_v, layer, *, tq):
    bsz, seq, d = x.shape
    n_mem, nh, hd = mem_k.shape[2:]
    assert seq % tq == 0
    kv_spec = pl.BlockSpec((None, None, n_mem, nh, hd), lambda b, t: (layer, b, 0, 0, 0))
    rows = lambda a: pl.BlockSpec((None, tq, a.shape[-1]), lambda b, t: (b, t, 0))
    params = [_whole(w) for w in (*pre_w, g, w_q, w_o)]
    return pl.pallas_call(
        functools.partial(_xattn_kernel, n_pre=len(pre_a)),
        grid=(bsz, seq // tq),
        in_specs=[rows(x)] + [rows(a) for a in pre_a] + [spec for _, spec in params] + [kv_spec, kv_spec],
        out_specs=rows(x),
        out_shape=jax.ShapeDtypeStruct((bsz, seq, d), F32),
        scratch_shapes=[pltpu.VMEM((nh, n_mem, hd), BF16), pltpu.VMEM((nh, n_mem, hd), BF16)],
        compiler_params=_cparams("arbitrary", "arbitrary"),
        name="xattn",
    )(x, *pre_a, *[w for w, _ in params], mem_k, mem_v)


def _attn_rows_kernel(q_ref, k_ref, v_ref, o_ref, pad_ref, *, rows_in):
    nb, rows, d = pad_ref.shape
    nblk = d // 128
    half_blk = nblk // 2
    lanes = k_ref.shape[1]
    lane_blk = lax.broadcasted_iota(jnp.int32, (1, lanes), 1) % nblk
    row_head = lax.broadcasted_iota(jnp.int32, (N_HEADS * rows, 1), 0) // rows
    live = lane_blk == row_head
    scale = (d // N_HEADS) ** -0.5
    for i in range(nb):
        pad_ref[i] = jnp.zeros((rows, d), F32)
        pad_ref[i, 0:rows_in, :] = q_ref[i]
        q = pad_ref[i]
        qx = jnp.concatenate([q[:, j * 128:(j + 1) * 128] for j in range(nblk)], axis=0)
        g = lax.dot_general(qx.astype(BF16), k_ref[i].astype(BF16), _NT, preferred_element_type=F32)
        s = jnp.concatenate(
            [g[2 * h * rows:(2 * h + 1) * rows]
             + pltpu.roll(g[(2 * h + 1) * rows:(2 * h + 2) * rows], lanes - half_blk, axis=1)
             for h in range(N_HEADS)], axis=0) * scale
        s = jnp.where(live, s, -1e30)
        e = jnp.exp(s - jnp.max(s, axis=-1, keepdims=True))
        p = e / jnp.sum(e, axis=-1, keepdims=True)
        px = jnp.concatenate(
            [blk for h in range(N_HEADS)
             for blk in (p[h * rows:(h + 1) * rows], pltpu.roll(p[h * rows:(h + 1) * rows], half_blk, axis=1))],
            axis=0)
        o = jnp.dot(px.astype(BF16), v_ref[i].astype(BF16), preferred_element_type=F32)
        o = jnp.concatenate([o[j * rows:(j + 1) * rows] for j in range(nblk)], axis=1)
        o_ref[i] = o[0:rows_in].astype(o_ref.dtype)


def _attn_rows_call(q, mem_k, mem_v, layer, *, nb):
    bsz, seq, d = q.shape
    depth, _, n_mem, nh, hd = mem_k.shape
    assert bsz % nb == 0 and hd == 256 and nh == N_HEADS and seq <= 16
    as_rows = lambda a: a.reshape(depth, bsz, n_mem, nh, 2, 128).transpose(0, 1, 2, 4, 3, 5).reshape(
        depth, bsz, n_mem * 2 * nh, 128)
    kv_spec = pl.BlockSpec((None, nb, n_mem * 2 * nh, 128), lambda b: (layer, b, 0, 0))
    return pl.pallas_call(
        functools.partial(_attn_rows_kernel, rows_in=seq),
        grid=(bsz // nb,),
        in_specs=[pl.BlockSpec((nb, seq, d), lambda b: (b, 0, 0)), kv_spec, kv_spec],
        out_specs=pl.BlockSpec((nb, seq, d), lambda b: (b, 0, 0)),
        out_shape=jax.ShapeDtypeStruct((bsz, seq, d), BF16),
        scratch_shapes=[pltpu.VMEM((nb, 8 * pl.cdiv(seq, 8), d), F32)],
        compiler_params=_cparams("arbitrary"),
        name="mem_attn_rows",
    )(q, as_rows(mem_k), as_rows(mem_v))


FFN_COLS = 256


def _ffn_kernel(*refs, tm, ts, hs, f_dim, has_hist, has_final):
    n_in = 6 + int(has_hist) + int(has_final)
    x_ref, g_ref, wup_ref, cw_ref, cb_ref, wdn_ref = refs[:6]
    y_ref, state_ref = refs[n_in:n_in + 2]
    gated_ref, hist_ref = refs[n_in + 2:n_in + 4]
    t_idx = pl.program_id(1)

    @pl.when(t_idx == 0)
    def _():
        hist_ref[...] = jnp.zeros(hist_ref.shape, F32)
        if has_hist:
            hist_ref[hs - 2 * ts:hs, :] = refs[6][...]

    x = x_ref[...]
    h = _rms(x, g_ref[...]).astype(BF16)
    row = lax.broadcasted_iota(jnp.int32, (tm, 1), 0)
    for c in range(f_dim // FFN_COLS):
        conv = []
        for part in range(2):
            cols = slice(part * f_dim + c * FFN_COLS, part * f_dim + (c + 1) * FFN_COLS)
            u = jnp.dot(h, wup_ref[:, cols], preferred_element_type=F32)
            if ts == 1:
                prev2, prev1 = hist_ref[hs - 2:hs - 1, cols], hist_ref[hs - 1:hs, cols]
                m1 = jnp.where(row == 0, prev1, pltpu.roll(u, 1, axis=0))
                m2 = jnp.where(row == 0, prev2, jnp.where(row == 1, prev1, pltpu.roll(u, 2, axis=0)))
            else:
                ext = jnp.concatenate([hist_ref[hs - 2 * ts:hs, cols], u], axis=0)
                m2, m1 = ext[0:tm], ext[ts:ts + tm]
            conv.append(cb_ref[:, cols] + cw_ref[0:1, cols] * m2 + cw_ref[1:2, cols] * m1
                        + cw_ref[2:3, cols] * u)
            hist_ref[:, cols] = u[tm - hs:tm]
        gated_ref[:, c * FFN_COLS:(c + 1) * FFN_COLS] = (_silu(conv[0]) * conv[1]).astype(BF16)
    out = x + jnp.dot(gated_ref[...], wdn_ref[...], preferred_element_type=F32)
    if has_final:
        out = _rms(out, refs[n_in - 1][...])
    y_ref[...] = out

    @pl.when(t_idx == pl.num_programs(1) - 1)
    def _():
        state_ref[...] = hist_ref[hs - 2 * ts:hs, :]


def _ffn_call(x, g, w_up, conv_w, conv_b, w_down, hist0, g_final, *, tm, ts):
    ngrp, rows, d = x.shape
    params = [_whole(a) for a in (g, w_up, conv_w, conv_b, w_down)]
    f2 = params[1][0].shape[-1]
    f_dim = f2 // 2
    hs = max(8, 2 * ts)
    assert rows % tm == 0 and tm >= hs and f_dim % FFN_COLS == 0 and (ts == 1 or ts % 8 == 0)
    args = [x] + [a for a, _ in params]
    in_specs = [pl.BlockSpec((None, tm, d), lambda s, t: (s, t, 0))] + [spec for _, spec in params]
    st_spec = pl.BlockSpec((None, 2 * ts, f2), lambda s, t: (s, 0, 0))
    if hist0 is not None:
        in_specs.append(st_spec)
        args.append(hist0)
    if g_final is not None:
        in_specs.append(_whole(g_final)[1])
        args.append(g_final)
    return pl.pallas_call(
        functools.partial(_ffn_kernel, tm=tm, ts=ts, hs=hs, f_dim=f_dim,
                          has_hist=hist0 is not None, has_final=g_final is not None),
        grid=(ngrp, rows // tm),
        in_specs=in_specs,
        out_specs=[pl.BlockSpec((None, tm, d), lambda s, t: (s, t, 0)), st_spec],
        out_shape=[jax.ShapeDtypeStruct((ngrp, rows, d), F32),
                   jax.ShapeDtypeStruct((ngrp, 2 * ts, f2), F32)],
        scratch_shapes=[pltpu.VMEM((tm, f_dim), BF16), pltpu.VMEM((hs, f2), F32)],
        compiler_params=_cparams("arbitrary", "arbitrary"),
        name="conv_ffn",
    )(*args)


def _trunk(x, mem_k, mem_v, states, p, *, prompt):
    bsz, seq, d = x.shape
    depth = p["norm_mix"].shape[0]
    m = bsz * seq
    x2 = x.reshape(m, d)
    new = {"hgrn": [], "s5_re": [], "s5_im": [], "gla": [], "conv": []}
    if prompt:
        gla_tiles = dict(tb=512, chunk=64, nb=1)
        attn_tq = 1024
    else:
        gla_tiles = dict(tb=seq, chunk=16, nb=8)
        attn_tq = seq
    for l in range(depth):
        g_mix = _pick(p["norm_mix"], l)
        if l % 2 == 0:
            e = l // 2
            kw = N_HEADS * HEAD_DK
            x3 = x2.reshape(bsz, seq, d)
            hgrn_par = [p["hgrn_lb"], p["hgrn_gnorm"][e].reshape(1, -1)]
            mats = p["s5_mats"][e]
            if prompt:
                o_a, s_a, u = _gla_call("hgrn", (x3, g_mix, _pick(p["w_in_ab"], e)), hgrn_par, None,
                                        layer=l, dv=kw // N_HEADS, extra_widths=(kw,), **gla_tiles)
                o_b, sr, si = _s5_call(u, 0, mats, None, tc=64, cw=512)
            else:
                proj = _norm_proj(x2, g_mix, _pick(p["w_in_ab"], e)).reshape(bsz, seq, -1)
                o_a, s_a = _gla_call("hgrn", proj, hgrn_par, states["hgrn"][e],
                                     layer=l, dv=kw // N_HEADS, **gla_tiles)
                x0 = (states["s5_re"][e].reshape(bsz, -1), states["s5_im"][e].reshape(bsz, -1))
                o_b, sr, si = _s5_call(proj, 4, mats, x0, tc=seq, cw=128)
            mixed = [o_a, o_b]
            mixed_w = [_pick(p["w_out_ab"], e, rows=kw, row_block=0), _pick(p["w_out_ab"], e, rows=kw, row_block=1)]
            new["hgrn"].append(s_a)
            new["s5_re"].append(sr.reshape(bsz, -1, S5_STATE))
            new["s5_im"].append(si.reshape(bsz, -1, S5_STATE))
        else:
            o_idx = l // 2
            if prompt:
                src_c = (x2.reshape(bsz, seq, d), g_mix, _pick(p["w_in_c"], o_idx))
            else:
                src_c = _norm_proj(x2, g_mix, _pick(p["w_in_c"], o_idx)).reshape(bsz, seq, -1)
            o_c, s_c = _gla_call(
                "gla", src_c,
                [p["gla_w_gate"][o_idx], p["gla_b_gate"][o_idx].reshape(1, -1),
                 p["gla_gnorm"][o_idx].reshape(1, -1)],
                None if states is None else states["gla"][o_idx],
                layer=l, dv=d // N_HEADS, **gla_tiles)
            mixed, mixed_w = [o_c], [_pick(p["w_out_c"], o_idx)]
            new["gla"].append(s_c)
        g_cross, w_q, w_o = _pick(p["norm_cross"], l), _pick(p["xa_w_q"], l), _pick(p["xa_w_o"], l)
        if prompt:
            x2 = _xattn_call(x2.reshape(bsz, seq, d), mixed, mixed_w, g_cross, w_q, w_o,
                             mem_k, mem_v, l, tq=attn_tq).reshape(m, d)
        else:
            x2 = _proj_res(x2, [a.reshape(m, -1) for a in mixed], mixed_w)
            q = _norm_proj(x2, g_cross, w_q)
            o_x = _attn_rows_call(q.reshape(bsz, seq, d), mem_k, mem_v, l, nb=4)
            x2 = _proj_res(x2, [o_x.reshape(m, d)], [w_o])
        g_final = p["norm_final"] if l == depth - 1 else None
        ffn_w = tuple(_pick(p[name], l) for name in
                      ("norm_ffn", "ffn_w_up", "ffn_conv_w", "ffn_conv_b", "ffn_w_down"))
        if prompt:
            y, cst = _ffn_call(x2.reshape(bsz, seq, d), *ffn_w, None, g_final, tm=1024, ts=1)
            x2 = y.reshape(m, d)
        else:
            xt = x2.reshape(bsz, seq, d).transpose(1, 0, 2).reshape(1, m, d)
            hist0 = states["conv"][l].transpose(1, 0, 2).reshape(1, 2 * bsz, -1)
            y, cst = _ffn_call(xt, *ffn_w, hist0, g_final, tm=m, ts=bsz)
            x2 = y.reshape(seq, bsz, d).transpose(1, 0, 2).reshape(m, d)
            cst = cst.reshape(2, bsz, -1).transpose(1, 0, 2)
        new["conv"].append(cst)
    return x2.reshape(bsz, seq, d), new


def kernel(x_prompt, x_sample, mem_prompt, cache_mem_k, cache_mem_v, state_hgrn, state_s5_re, state_s5_im, state_gla, state_ffn_conv, norm_mix, norm_cross, norm_mem, norm_ffn, norm_final, w_in_ab, hgrn_lb, hgrn_gnorm, s5_lam_re, s5_lam_im, s5_log_step, s5_b_re, s5_b_im, s5_c_re, s5_c_im, s5_d, s5_w_glu, s5_b_glu, w_out_ab, w_in_c, gla_w_gate_up, gla_b_gate, gla_gnorm, w_out_c, xa_w_q, xa_w_kv, xa_w_o, ffn_w_up, ffn_conv_w, ffn_conv_b, ffn_w_down):
    depth, d = norm_mix.shape

    gla_cols = w_in_c.shape[2]
    gate_rank = gla_w_gate_up.shape[1]
    pad_c = (-gla_cols) % 128
    w_in_c_p = jnp.pad(w_in_c, ((0, 0), (0, 0), (0, pad_c))).astype(BF16)
    gla_w_gate = jnp.pad(gla_w_gate_up, ((0, 0), (0, 128 - gate_rank), (0, 0))).astype(BF16)

    s5_mats = []
    for e in range(s5_lam_re.shape[0]):
        a_re, a_im, bb_re, bb_im = _s5_prep(s5_lam_re[e], s5_lam_im[e], s5_log_step[e],
                                            s5_b_re[e], s5_b_im[e])
        s5_mats.append(_s5_matrices(a_re, a_im, bb_re, bb_im, s5_c_re[e], s5_c_im[e], s5_d[e],
                                    s5_w_glu[e], s5_b_glu[e]))

    row = lambda a: a.reshape(a.shape[0], 1, a.shape[1])
    p = dict(norm_mix=row(norm_mix), norm_cross=row(norm_cross), norm_ffn=row(norm_ffn),
             norm_final=norm_final.reshape(1, d),
             w_in_ab=w_in_ab.astype(BF16), hgrn_lb=hgrn_lb, hgrn_gnorm=hgrn_gnorm, s5_mats=s5_mats,
             w_out_ab=w_out_ab.astype(BF16), w_in_c=w_in_c_p, gla_w_gate=gla_w_gate,
             gla_b_gate=gla_b_gate, gla_gnorm=gla_gnorm, w_out_c=w_out_c.astype(BF16),
             xa_w_q=xa_w_q.astype(BF16), xa_w_o=xa_w_o.astype(BF16),
             ffn_w_up=ffn_w_up.astype(BF16), ffn_conv_w=ffn_conv_w, ffn_conv_b=row(ffn_conv_b),
             ffn_w_down=ffn_w_down.astype(BF16))

    mem_k_p, mem_v_p = _mem_kv(mem_prompt, norm_mem.reshape(depth, 1, d), xa_w_kv.astype(BF16))
    y_prompt, st_p = _trunk(x_prompt, mem_k_p, mem_v_p, None, p, prompt=True)

    states = dict(hgrn=state_hgrn, s5_re=state_s5_re, s5_im=state_s5_im, gla=state_gla,
                  conv=state_ffn_conv)
    y_sample, st_s = _trunk(x_sample, cache_mem_k, cache_mem_v, states, p, prompt=False)

    stack = lambda xs: xs[0][None] if len(xs) == 1 else jnp.stack(xs)
    return (y_prompt, y_sample,
            stack(st_p["hgrn"]), stack(st_p["s5_re"]), stack(st_p["s5_im"]), stack(st_p["gla"]),
            mem_k_p, mem_v_p, stack(st_p["conv"]),
            stack(st_s["hgrn"]), stack(st_s["s5_re"]), stack(st_s["s5_im"]), stack(st_s["gla"]),
            stack(st_s["conv"]))
```

```python
import functools
import math

import jax
import jax.numpy as jnp
from jax import lax
from jax.experimental import pallas as pl
from jax.experimental.pallas import tpu as pltpu

F32 = jnp.float32
BF16 = jnp.bfloat16

EPS = 1e-6
S5_MAX_RE = -1e-4
GLA_GATE_TAU = 16.0
N_HEADS = 4
HEAD_DK = 128
S5_GROUP = 16
S5_STATE = 64
SUB_BLOCK = 16
GLA_SAFE_DECAY = 64.0
VMEM_LIMIT = 56 * 1024 * 1024

_NT = (((1,), (1,)), ((), ()))
_TN = (((0,), (0,)), ((), ()))


def _cparams(*sem):
    return pltpu.CompilerParams(dimension_semantics=sem, vmem_limit_bytes=VMEM_LIMIT)


def _rms(x, g):
    return x * lax.rsqrt(jnp.mean(x * x, axis=-1, keepdims=True) + EPS) * g


def _sigmoid(x):
    return 1.0 / (1.0 + jnp.exp(-x))


def _silu(x):
    return x * _sigmoid(x)


def _row_tile(rows, want):
    t = min(rows, want)
    assert rows % t == 0, (rows, t)
    return t


def _pick(a, layer, rows=None, row_block=0, cols=None, col_block=0):
    block = (None, rows or a.shape[1], cols or a.shape[2])
    return a, pl.BlockSpec(block, lambda *_: (layer, row_block, col_block))


def _whole(a):
    return a if isinstance(a, tuple) else (a, pl.BlockSpec(a.shape, lambda *_: (0,) * a.ndim))


def _norm_proj_kernel(x_ref, g_ref, w_ref, o_ref):
    h = _rms(x_ref[...], g_ref[...]).astype(BF16)
    n = o_ref.shape[1]
    for c0 in range(0, n, 512):
        cw = min(512, n - c0)
        o_ref[:, c0:c0 + cw] = jnp.dot(h, w_ref[:, c0:c0 + cw], preferred_element_type=F32)


def _norm_proj(x, g, w, *, tm=512):
    m, k = x.shape
    (g, g_spec), (w, w_spec) = _whole(g), _whole(w)
    n = w.shape[-1]
    tm = _row_tile(m, tm)
    return pl.pallas_call(
        _norm_proj_kernel,
        grid=(m // tm,),
        in_specs=[pl.BlockSpec((tm, k), lambda i: (i, 0)), g_spec, w_spec],
        out_specs=pl.BlockSpec((tm, n), lambda i: (i, 0)),
        out_shape=jax.ShapeDtypeStruct((m, n), F32),
        compiler_params=_cparams("arbitrary"),
        name="norm_proj",
    )(x, g, w)


def _mem_kv_kernel(x_ref, g_ref, w_ref, k_ref, v_ref):
    nb, n_mem, d = x_ref.shape
    h = _rms(x_ref[...].reshape(nb * n_mem, d), g_ref[...]).astype(BF16)
    hd = d // N_HEADS
    for o_ref, base in ((k_ref, 0), (v_ref, d)):
        for hh in range(N_HEADS):
            kv = jnp.dot(h, w_ref[:, base + hh * hd:base + (hh + 1) * hd], preferred_element_type=F32)
            for i in range(nb):
                o_ref[i, :, hh, :] = kv[i * n_mem:(i + 1) * n_mem]


def _mem_kv(mem, g, w, *, nb=2):
    bsz, n_mem, d = mem.shape
    depth = w.shape[0]
    assert bsz % nb == 0
    out_spec = pl.BlockSpec((None, nb, n_mem, N_HEADS, d // N_HEADS), lambda l, b: (l, b, 0, 0, 0))
    out_shape = jax.ShapeDtypeStruct((depth, bsz, n_mem, N_HEADS, d // N_HEADS), F32)
    return pl.pallas_call(
        _mem_kv_kernel,
        grid=(depth, bsz // nb),
        in_specs=[pl.BlockSpec((nb, n_mem, d), lambda l, b: (b, 0, 0)),
                  pl.BlockSpec((None, 1, d), lambda l, b: (l, 0, 0)),
                  pl.BlockSpec((None, d, 2 * d), lambda l, b: (l, 0, 0))],
        out_specs=[out_spec, out_spec],
        out_shape=[out_shape, out_shape],
        compiler_params=_cparams("arbitrary", "arbitrary"),
        name="mem_kv",
    )(mem, g, w)


def _proj_res_kernel(res_ref, *refs, n_in):
    acc = res_ref[...]
    for a_ref, w_ref in zip(refs[:n_in], refs[n_in:2 * n_in]):
        acc = acc + jnp.dot(a_ref[...].astype(BF16), w_ref[...], preferred_element_type=F32)
    refs[2 * n_in][...] = acc


def _proj_res(res, a_list, w_list, *, tm=512):
    m, n = res.shape
    tm = _row_tile(m, tm)
    n_in = len(a_list)
    w_list = [_whole(w) for w in w_list]
    in_specs = [pl.BlockSpec((tm, n), lambda i: (i, 0))]
    in_specs += [pl.BlockSpec((tm, a.shape[1]), lambda i: (i, 0)) for a in a_list]
    in_specs += [spec for _, spec in w_list]
    return pl.pallas_call(
        functools.partial(_proj_res_kernel, n_in=n_in),
        grid=(m // tm,),
        in_specs=in_specs,
        out_specs=pl.BlockSpec((tm, n), lambda i: (i, 0)),
        out_shape=jax.ShapeDtypeStruct((m, n), F32),
        compiler_params=_cparams("arbitrary"),
        name="proj_res",
    )(res, *a_list, *[w for w, _ in w_list])


def _cumsum_rows(x, c):
    hi = x.astype(BF16)
    rest = x - hi.astype(F32)
    mid = rest.astype(BF16)
    lo = (rest - mid.astype(F32)).astype(BF16)
    n = max(c, min(x.shape[0], 128))
    r = lax.broadcasted_iota(jnp.int32, (n, n), 0)
    col = lax.broadcasted_iota(jnp.int32, (n, n), 1)
    tri = jnp.where((r >= col) & (r // c == col // c), 1.0, 0.0).astype(BF16)
    tri3 = jnp.concatenate([tri, tri, tri], axis=1)
    out = [jnp.dot(tri3, jnp.concatenate([hi[s:s + n], mid[s:s + n], lo[s:s + n]], axis=0),
                   preferred_element_type=F32) for s in range(0, x.shape[0], n)]
    return out[0] if len(out) == 1 else jnp.concatenate(out, axis=0)


def _gla_head(qh, kh, bh, vh, st_ref, small_decay, live_rows):
    c = qh.shape[0]
    sb = min(SUB_BLOCK, c)
    vb = vh.astype(BF16)
    st = st_ref[...]
    b_last = bh[c - 1:c, :]
    q_in = (qh * jnp.exp(bh)).astype(BF16)
    o_inter = lax.dot_general(q_in, st.astype(BF16), _NT, preferred_element_type=F32)
    if small_decay:
        k_up = kh * jnp.exp(-bh)
        a = lax.dot_general(q_in, k_up.astype(BF16), _NT, preferred_element_type=F32)
        causal = (lax.broadcasted_iota(jnp.int32, (c, c), 0) >= lax.broadcasted_iota(jnp.int32, (c, c), 1))
        kd = (k_up * jnp.exp(b_last)).astype(BF16)
        st_ref[...] = st * jnp.exp(b_last) + lax.dot_general(vb, kd, _TN, preferred_element_type=F32)
        return o_inter + jnp.dot(jnp.where(causal, a, 0.0).astype(BF16), vb, preferred_element_type=F32)
    kd = (kh * jnp.exp(b_last - bh)).astype(BF16)
    st_ref[...] = st * jnp.exp(b_last) + lax.dot_general(vb, kd, _TN, preferred_element_type=F32)
    rows = lax.broadcasted_iota(jnp.int32, (sb, 1), 0)
    parts = []
    for s in range(c // sb):
        r0 = s * sb
        qs, ks, bs, vs = qh[r0:r0 + sb], kh[r0:r0 + sb], bh[r0:r0 + sb], vh[r0:r0 + sb]
        acc = o_inter[r0:r0 + sb]
        if s > 0:
            ref_b = bh[r0 - 1:r0, :]
            qf = (qs * jnp.exp(bs - ref_b)).astype(BF16)
            kf = (kh[0:r0] * jnp.exp(ref_b - bh[0:r0])).astype(BF16)
            a_off = lax.dot_general(qf, kf, _NT, preferred_element_type=F32)
            acc = acc + jnp.dot(a_off.astype(BF16), vb[0:r0], preferred_element_type=F32)
        for j in range(max(0, min(sb, live_rows - r0))):
            w = jnp.exp(jnp.minimum(bs - bs[j:j + 1], 0.0)) * qs * ks[j:j + 1]
            col = jnp.where(rows >= j, jnp.sum(w, axis=-1, keepdims=True), 0.0)
            acc = acc + col * vs[j:j + 1]
        parts.append(acc)
    return parts[0] if len(parts) == 1 else jnp.concatenate(parts, axis=0)


def _gla_kernel(*refs, mode, layer, dv, nb, rows_in, chunk, n_chunks, has_s0, in_widths):
    n_act = 4 if mode == "hgrn" else 5
    n_src = 3 if in_widths else n_act
    n_par = 2 if mode == "hgrn" else 3
    n_in = n_src + n_par + (1 if has_s0 else 0)
    n_out = 2 + (len(in_widths) - n_act if in_widths else 0)
    ins, (o_ref, sout_ref), scr = refs[:n_in], refs[n_in:n_in + 2], refs[n_in + n_out:]
    pars = ins[n_src:n_src + n_par]
    st_ref = scr[0]
    pad_refs = scr[1:]
    t_idx = pl.program_id(1)
    padded = rows_in < chunk
    assert (padded and not in_widths) or nb == 1

    @pl.when(t_idx == 0)
    def _():
        for i in range(nb):
            for h in range(N_HEADS):
                if has_s0:
                    st_ref[i * N_HEADS + h] = ins[-1][i, h].T
                else:
                    st_ref[i * N_HEADS + h] = jnp.zeros(st_ref.shape[1:], F32)

    if in_widths:
        x_ref, gm_ref, win_ref = ins[:n_src]
        hx = _rms(x_ref[0], gm_ref[...]).astype(BF16)
        offs = [sum(in_widths[:i]) for i in range(len(in_widths))]
        ld = [jnp.dot(hx, win_ref[:, o:o + w], preferred_element_type=F32) for o, w in zip(offs, in_widths)]
        for extra_ref, extra in zip(refs[n_in + 2:n_in + n_out], ld[n_act:]):
            extra_ref[0] = extra
        ld = ld[:n_act]
    elif padded:
        for p_ref, a_ref in zip(pad_refs, ins[:n_act]):
            p_ref[...] = jnp.zeros(p_ref.shape, F32)
            for i in range(nb):
                p_ref[i * chunk:i * chunk + rows_in, :] = a_ref[i]
        ld = [a[...] for a in pad_refs]
    else:
        ld = [a[0] for a in ins[:n_act]]
    groups = nb if padded else n_chunks
    span = groups * chunk

    if mode == "hgrn":
        q_raw, f, v, gate = ld
        lb_ref, gn_ref = pars
        lbv = lb_ref[...]
        e = jnp.exp(lbv - jnp.max(lbv, axis=0, keepdims=True))
        lb = jnp.sum(e[0:layer + 1], axis=0, keepdims=True) / jnp.sum(e, axis=0, keepdims=True)
        forget = lb + (1.0 - lb) * _sigmoid(f)
        k = 1.0 - forget
        lg = jnp.log(forget)
        q = _silu(q_raw)
    else:
        q_raw, k, v, gate, gd = ld
        wg_ref, bg_ref, gn_ref = pars
        z = jnp.dot(gd.astype(BF16), wg_ref[...], preferred_element_type=F32) + bg_ref[...]
        lg = (jnp.minimum(z, 0.0) - jnp.log(1.0 + jnp.exp(-jnp.abs(z)))) / GLA_GATE_TAU
        q = q_raw * (HEAD_DK ** -0.5)
    if padded:
        live = lax.broadcasted_iota(jnp.int32, (span, 1), 0) % chunk < rows_in
        lg = jnp.where(live, lg, 0.0)
        k = jnp.where(live, k, 0.0)
    b = _cumsum_rows(lg, chunk)

    def piece(x, i, h, width):
        return x[i * chunk:(i + 1) * chunk, h * width:(h + 1) * width]

    def heads(small_decay):
        rows = []
        for i in range(groups):
            cols = []
            for h in range(N_HEADS):
                cols.append(_gla_head(piece(q, i, h, HEAD_DK), piece(k, i, h, HEAD_DK),
                                      piece(b, i, h, HEAD_DK), piece(v, i, h, dv),
                                      st_ref.at[(i if padded else 0) * N_HEADS + h], small_decay,
                                      rows_in if padded else chunk))
            rows.append(jnp.concatenate(cols, axis=1))
        return rows[0] if groups == 1 else jnp.concatenate(rows, axis=0)

    if chunk <= SUB_BLOCK:
        o_raw = heads(False)
    else:
        o_raw = lax.cond(jnp.min(b) >= -GLA_SAFE_DECAY,
                         functools.partial(heads, True), functools.partial(heads, False))
    o_all = jnp.concatenate(
        [_rms(o_raw[:, h * dv:(h + 1) * dv], gn_ref[...]) * _silu(gate[:, h * dv:(h + 1) * dv])
         for h in range(N_HEADS)], axis=1).astype(o_ref.dtype)
    if padded:
        for i in range(nb):
            o_ref[i] = o_all[i * chunk:i * chunk + rows_in]
    else:
        o_ref[0] = o_all

    @pl.when(t_idx == pl.num_programs(1) - 1)
    def _():
        for i in range(nb):
            for h in range(N_HEADS):
                sout_ref[i, h] = st_ref[i * N_HEADS + h].T


def _gla_call(mode, src, params, s0, *, layer, dv, tb, chunk, nb, extra_widths=()):
    fused_in = isinstance(src, tuple)
    proj = src[0] if fused_in else src
    bsz, seq, _ = proj.shape
    rows_in = min(tb, seq)
    if rows_in < chunk:
        assert seq == rows_in and bsz % nb == 0
        nt, n_chunks = 1, 1
    else:
        assert seq % tb == 0 and tb % chunk == 0 and nb == 1
        nt, n_chunks = seq // tb, tb // chunk
    kw, vw = N_HEADS * HEAD_DK, N_HEADS * dv

    def act(width, col_block):
        return pl.BlockSpec((nb, rows_in, width), lambda b, t: (b, t, col_block))

    def whole(a):
        return pl.BlockSpec(a.shape, lambda b, t: (0,) * a.ndim)

    if mode == "hgrn":
        act_specs = [act(kw, 0), act(kw, 1), act(vw, 2), act(vw, 3)]
        act_widths = [kw, kw, vw, vw]
    else:
        act_specs = [act(kw, 0), act(kw, 1), act(vw, kw * 2 // vw), act(vw, kw * 2 // vw + 1),
                     act(128, (2 * kw + 2 * vw) // 128)]
        act_widths = [kw, kw, vw, vw, 128]
    if fused_in:
        x, g_mix, w_in = src[0], _whole(src[1]), _whole(src[2])
        in_specs = [pl.BlockSpec((nb, rows_in, x.shape[-1]), lambda b, t: (b, t, 0)), g_mix[1], w_in[1]]
        args = [x, g_mix[0], w_in[0]]
    else:
        in_specs, args = act_specs, [proj] * len(act_specs)
    in_specs = in_specs + [whole(p) for p in params]
    args = args + list(params)
    state_spec = pl.BlockSpec((nb, N_HEADS, HEAD_DK, dv), lambda b, t: (b, 0, 0, 0))
    if s0 is not None:
        in_specs.append(state_spec)
        args.append(s0)
    scratch = [pltpu.VMEM((nb * N_HEADS, dv, HEAD_DK), F32)]
    if rows_in < chunk:
        scratch += [pltpu.VMEM((nb * chunk, w), F32) for w in act_widths]
    return pl.pallas_call(
        functools.partial(_gla_kernel, mode=mode, layer=layer, dv=dv, nb=nb, rows_in=rows_in,
                          chunk=chunk, n_chunks=n_chunks, has_s0=s0 is not None,
                          in_widths=tuple(act_widths) + tuple(extra_widths) if fused_in else None),
        grid=(bsz // nb, nt),
        in_specs=in_specs,
        out_specs=[pl.BlockSpec((nb, rows_in, vw), lambda b, t: (b, t, 0)), state_spec]
        + [pl.BlockSpec((nb, rows_in, w), lambda b, t: (b, t, 0)) for w in extra_widths],
        out_shape=[jax.ShapeDtypeStruct((bsz, seq, vw), BF16),
                   jax.ShapeDtypeStruct((bsz, N_HEADS, HEAD_DK, dv), F32)]
        + [jax.ShapeDtypeStruct((bsz, seq, w), F32) for w in extra_widths],
        scratch_shapes=scratch,
        compiler_params=_cparams("arbitrary", "arbitrary"),
        name="gla_" + mode,
    )(*args)


def _s5_prep_kernel(lre_ref, lim_ref, ls_ref, lre_x_ref, lim_x_ref, ls_x_ref, bre_ref, bim_ref,
                    are_ref, aim_ref, bbre_ref, bbim_ref):
    def disc(lre, lim, ls):
        lr = jnp.minimum(lre, S5_MAX_RE)
        dt = jnp.exp(ls)
        mag = jnp.exp(lr * dt)
        a_re = mag * jnp.cos(lim * dt)
        a_im = mag * jnp.sin(lim * dt)
        den = lr * lr + lim * lim
        z_re = ((a_re - 1.0) * lr + a_im * lim) / den
        z_im = (a_im * lr - (a_re - 1.0) * lim) / den
        return a_re, a_im, z_re, z_im

    a_re, a_im, _, _ = disc(lre_ref[...], lim_ref[...], ls_ref[...])
    are_ref[...] = a_re
    aim_ref[...] = a_im
    _, _, z_re, z_im = disc(lre_x_ref[...], lim_x_ref[...], ls_x_ref[...])
    bbre_ref[...] = z_re * bre_ref[...] - z_im * bim_ref[...]
    bbim_ref[...] = z_re * bim_ref[...] + z_im * bre_ref[...]


def _s5_prep(lam_re, lam_im, log_step, b_re, b_im):
    g, n = lam_re.shape
    p = b_re.shape[-1]
    ls = jnp.broadcast_to(log_step[:, None], (g, n))
    rep = lambda a: jnp.repeat(a, p, axis=1)
    outs = pl.pallas_call(
        _s5_prep_kernel,
        out_shape=[jax.ShapeDtypeStruct((g, n), F32)] * 2 + [jax.ShapeDtypeStruct((g, n * p), F32)] * 2,
        name="s5_prep",
    )(lam_re, lam_im, ls, rep(lam_re), rep(lam_im), rep(ls),
      b_re.reshape(g, n * p), b_im.reshape(g, n * p))
    a_re, a_im, bb_re, bb_im = outs
    return a_re, a_im, bb_re.reshape(g, n, p), bb_im.reshape(g, n, p)


def _s5_kernel(*refs, bg, tc, nt, cw, has_x0):
    n_in = 9 if has_x0 else 7
    u_ref, bm_ref, cm_ref, a_ref, d_ref, wg_ref, bgl_ref = refs[:7]
    o_ref, sre_ref, sim_ref = refs[n_in:n_in + 3]
    utm_ref, xs_ref, st_ref = refs[n_in + 3:]
    t_idx = pl.program_id(1)
    half = xs_ref.shape[1] // 2
    uw = u_ref.shape[-1] // 2

    @pl.when(t_idx == 0)
    def _():
        for hf in range(2):
            if has_x0:
                st_ref[:, hf * 2 * half:hf * 2 * half + half] = refs[7][:, hf * half:(hf + 1) * half]
                st_ref[:, hf * 2 * half + half:(hf + 1) * 2 * half] = refs[8][:, hf * half:(hf + 1) * half]
            else:
                st_ref[...] = jnp.zeros(st_ref.shape, F32)

    for t in range(tc):
        utm_ref[t * bg:(t + 1) * bg, :] = u_ref[:, t, :]
    u = utm_ref[...]
    ys = []
    for hf in range(2):
        xs_ref[...] = jnp.dot(u[:, hf * uw:(hf + 1) * uw].astype(BF16), bm_ref[hf],
                              preferred_element_type=F32)
        base = hf * 2 * half
        for c0 in range(0, half, cw):
            ar = a_ref[0:1, base + c0:base + c0 + cw]
            ai = a_ref[0:1, base + half + c0:base + half + c0 + cw]
            xr = st_ref[:, base + c0:base + c0 + cw]
            xi = st_ref[:, base + half + c0:base + half + c0 + cw]
            for t in range(tc):
                rows = slice(t * bg, (t + 1) * bg)
                nr = ar * xr - ai * xi + xs_ref[rows, c0:c0 + cw]
                ni = ar * xi + ai * xr + xs_ref[rows, half + c0:half + c0 + cw]
                xs_ref[rows, c0:c0 + cw] = nr
                xs_ref[rows, half + c0:half + c0 + cw] = ni
                xr, xi = nr, ni
            st_ref[:, base + c0:base + c0 + cw] = xr
            st_ref[:, base + half + c0:base + half + c0 + cw] = xi
        ys.append(jnp.dot(xs_ref[...].astype(BF16), cm_ref[hf], preferred_element_type=F32))
    y = jnp.concatenate(ys, axis=1) + d_ref[...] * u
    y = 0.5 * y * (1.0 + jnp.tanh(math.sqrt(2.0 / math.pi) * (y + 0.044715 * (y * y * y))))
    gate = jnp.dot(y.astype(BF16), wg_ref[...], preferred_element_type=F32) + bgl_ref[...]
    utm_ref[...] = y * _sigmoid(gate)
    for t in range(tc):
        o_ref[:, t, :] = utm_ref[t * bg:(t + 1) * bg, :]

    @pl.when(t_idx == nt - 1)
    def _():
        for hf in range(2):
            sre_ref[:, hf * half:(hf + 1) * half] = st_ref[:, hf * 2 * half:hf * 2 * half + half]
            sim_ref[:, hf * half:(hf + 1) * half] = st_ref[:, hf * 2 * half + half:(hf + 1) * 2 * half]


def _s5_call(u_src, col_block, mats, x0, *, tc, cw):
    bg, seq, _ = u_src.shape
    assert seq % tc == 0
    nt = seq // tc
    bmat, cmat, a_flat, d_row, w_glu, b_glu = mats
    width = d_row.shape[1]
    nstate = a_flat.shape[1] // 2
    whole = lambda a: pl.BlockSpec(a.shape, lambda g, t: (0,) * a.ndim)
    in_specs = [pl.BlockSpec((bg, tc, width), lambda g, t: (0, t, col_block))]
    in_specs += [whole(m) for m in mats]
    args = [u_src] + list(mats)
    st_spec = pl.BlockSpec((bg, nstate), lambda g, t: (0, 0))
    if x0 is not None:
        in_specs += [st_spec, st_spec]
        args += list(x0)
    scratch = [pltpu.VMEM((bg * tc, width), F32), pltpu.VMEM((bg * tc, nstate), F32),
               pltpu.VMEM((bg, 2 * nstate), F32)]
    return pl.pallas_call(
        functools.partial(_s5_kernel, bg=bg, tc=tc, nt=nt, cw=cw, has_x0=x0 is not None),
        grid=(1, nt),
        in_specs=in_specs,
        out_specs=[pl.BlockSpec((bg, tc, width), lambda g, t: (0, t, 0)), st_spec, st_spec],
        out_shape=[jax.ShapeDtypeStruct((bg, seq, width), F32),
                   jax.ShapeDtypeStruct((bg, nstate), F32),
                   jax.ShapeDtypeStruct((bg, nstate), F32)],
        scratch_shapes=scratch,
        compiler_params=_cparams("arbitrary", "arbitrary"),
        name="s5",
    )(*args)


def _s5_matrices(a_re, a_im, bb_re, bb_im, c_re, c_im, d, w_glu, b_glu):
    g, n, p = bb_re.shape
    gh = g // 2
    eye = jnp.eye(gh, dtype=F32)

    def block_diag(t):
        return (eye[:, None, :, None] * t[:, :, None, :]).reshape(gh * t.shape[1], gh * t.shape[2])

    def in_mat(bb):
        return block_diag(bb.transpose(0, 2, 1))

    def out_mat(cc):
        return block_diag(cc.transpose(0, 2, 1))

    bmat = jnp.stack([jnp.concatenate([in_mat(bb_re[h * gh:(h + 1) * gh]),
                                       in_mat(bb_im[h * gh:(h + 1) * gh])], axis=1)
                      for h in range(2)]).astype(BF16)
    cmat = jnp.stack([jnp.concatenate([out_mat(c_re[h * gh:(h + 1) * gh]),
                                       out_mat(-c_im[h * gh:(h + 1) * gh])], axis=0)
                      for h in range(2)]).astype(BF16)
    a_flat = jnp.concatenate([jnp.concatenate([a_re[h * gh:(h + 1) * gh].reshape(1, gh * n),
                                               a_im[h * gh:(h + 1) * gh].reshape(1, gh * n)], axis=1)
                              for h in range(2)], axis=1)
    return (bmat, cmat, a_flat, d.reshape(1, g * p), w_glu.astype(BF16), b_glu.reshape(1, -1))


def _xattn_kernel(*refs, n_pre):
    x_ref = refs[0]
    g_ref, wq_ref, wo_ref, k_ref, v_ref, y_ref, kt_ref, vt_ref = refs[1 + 2 * n_pre:]

    @pl.when(pl.program_id(1) == 0)
    def _():
        kt_ref[...] = jnp.transpose(k_ref[...], (1, 0, 2)).astype(BF16)
        vt_ref[...] = jnp.transpose(v_ref[...], (1, 0, 2)).astype(BF16)

    x = x_ref[...]
    for a_ref, w_ref in zip(refs[1:1 + n_pre], refs[1 + n_pre:1 + 2 * n_pre]):
        x = x + jnp.dot(a_ref[...].astype(BF16), w_ref[...], preferred_element_type=F32)
    q = jnp.dot(_rms(x, g_ref[...]).astype(BF16), wq_ref[...], preferred_element_type=F32).astype(BF16)
    hd = q.shape[1] // N_HEADS
    outs = []
    for h in range(N_HEADS):
        s = lax.dot_general(q[:, h * hd:(h + 1) * hd], kt_ref[h], _NT,
                            preferred_element_type=F32) * (hd ** -0.5)
        p = jnp.exp(s - jnp.max(s, axis=-1, keepdims=True))
        p = p / jnp.sum(p, axis=-1, keepdims=True)
        outs.append(jnp.dot(p.astype(BF16), vt_ref[h], preferred_element_type=F32).astype(BF16))
    y_ref[...] = x + jnp.dot(jnp.concatenate(outs, axis=1), wo_ref[...], preferred_element_type=F32)


def _xattn_call(x, pre_a, pre_w, g, w_q, w_o, mem_k, mem_v, layer, *, tq):
    bsz, seq, d = x.shape
    n_mem, nh, hd = mem_k.shape[2:]
    assert seq % tq == 0
    kv_spec = pl.BlockSpec((None, None, n_mem, nh, hd), lambda b, t: (layer, b, 0, 0, 0))
    rows = lambda a: pl.BlockSpec((None, tq, a.shape[-1]), lambda b, t: (b, t, 0))
    params = [_whole(w) for w in (*pre_w, g, w_q, w_o)]
    return pl.pallas_call(
        functools.partial(_xattn_kernel, n_pre=len(pre_a)),
        grid=(bsz, seq // tq),
        in_specs=[rows(x)] + [rows(a) for a in pre_a] + [spec for _, spec in params] + [kv_spec, kv_spec],
        out_specs=rows(x),
        out_shape=jax.ShapeDtypeStruct((bsz, seq, d), F32),
        scratch_shapes=[pltpu.VMEM((nh, n_mem, hd), BF16), pltpu.VMEM((nh, n_mem, hd), BF16)],
        compiler_params=_cparams("arbitrary", "arbitrary"),
        name="xattn",
    )(x, *pre_a, *[w for w, _ in params], mem_k, mem_v)


def _attn_rows_kernel(q_ref, k_ref, v_ref, o_ref, pad_ref, *, rows_in):
    nb, rows, d = pad_ref.shape
    nblk = d // 128
    half_blk = nblk // 2
    lanes = k_ref.shape[1]
    lane_blk = lax.broadcasted_iota(jnp.int32, (1, lanes), 1) % nblk
    row_head = lax.broadcasted_iota(jnp.int32, (N_HEADS * rows, 1), 0) // rows
    live = lane_blk == row_head
    scale = (d // N_HEADS) ** -0.5
    for i in range(nb):
        pad_ref[i] = jnp.zeros((rows, d), F32)
        pad_ref[i, 0:rows_in, :] = q_ref[i]
        q = pad_ref[i]
        qx = jnp.concatenate([q[:, j * 128:(j + 1) * 128] for j in range(nblk)], axis=0)
        g = lax.dot_general(qx.astype(BF16), k_ref[i].astype(BF16), _NT, preferred_element_type=F32)
        s = jnp.concatenate(
            [g[2 * h * rows:(2 * h + 1) * rows]
             + pltpu.roll(g[(2 * h + 1) * rows:(2 * h + 2) * rows], lanes - half_blk, axis=1)
             for h in range(N_HEADS)], axis=0) * scale
        s = jnp.where(live, s, -1e30)
        e = jnp.exp(s - jnp.max(s, axis=-1, keepdims=True))
        p = e / jnp.sum(e, axis=-1, keepdims=True)
        px = jnp.concatenate(
            [blk for h in range(N_HEADS)
             for blk in (p[h * rows:(h + 1) * rows], pltpu.roll(p[h * rows:(h + 1) * rows], half_blk, axis=1))],
            axis=0)
        o = jnp.dot(px.astype(BF16), v_ref[i].astype(BF16), preferred_element_type=F32)
        o = jnp.concatenate([o[j * rows:(j + 1) * rows] for j in range(nblk)], axis=1)
        o_ref[i] = o[0:rows_in].astype(o_ref.dtype)


def _attn_rows_call(q, mem_k, mem_v, layer, *, nb):
    bsz, seq, d = q.shape
    depth, _, n_mem, nh, hd = mem_k.shape
    assert bsz % nb == 0 and hd == 256 and nh == N_HEADS and seq <= 16
    as_rows = lambda a: a.reshape(depth, bsz, n_mem, nh, 2, 128).transpose(0, 1, 2, 4, 3, 5).reshape(
        depth, bsz, n_mem * 2 * nh, 128)
    kv_spec = pl.BlockSpec((None, nb, n_mem * 2 * nh, 128), lambda b: (layer, b, 0, 0))
    return pl.pallas_call(
        functools.partial(_attn_rows_kernel, rows_in=seq),
        grid=(bsz // nb,),
        in_specs=[pl.BlockSpec((nb, seq, d), lambda b: (b, 0, 0)), kv_spec, kv_spec],
        out_specs=pl.BlockSpec((nb, seq, d), lambda b: (b, 0, 0)),
        out_shape=jax.ShapeDtypeStruct((bsz, seq, d), BF16),
        scratch_shapes=[pltpu.VMEM((nb, 8 * pl.cdiv(seq, 8), d), F32)],
        compiler_params=_cparams("arbitrary"),
        name="mem_attn_rows",
    )(q, as_rows(mem_k), as_rows(mem_v))


FFN_COLS = 256


def _ffn_kernel(*refs, tm, ts, hs, f_dim, has_hist, has_final):
    n_in = 6 + int(has_hist) + int(has_final)
    x_ref, g_ref, wup_ref, cw_ref, cb_ref, wdn_ref = refs[:6]
    y_ref, state_ref = refs[n_in:n_in + 2]
    gated_ref, hist_ref = refs[n_in + 2:n_in + 4]
    t_idx = pl.program_id(1)

    @pl.when(t_idx == 0)
    def _():
        hist_ref[...] = jnp.zeros(hist_ref.shape, F32)
        if has_hist:
            hist_ref[hs - 2 * ts:hs, :] = refs[6][...]

    x = x_ref[...]
    h = _rms(x, g_ref[...]).astype(BF16)
    row = lax.broadcasted_iota(jnp.int32, (tm, 1), 0)
    for c in range(f_dim // FFN_COLS):
        conv = []
        for part in range(2):
            cols = slice(part * f_dim + c * FFN_COLS, part * f_dim + (c + 1) * FFN_COLS)
            u = jnp.dot(h, wup_ref[:, cols], preferred_element_type=F32)
            if ts == 1:
                prev2, prev1 = hist_ref[hs - 2:hs - 1, cols], hist_ref[hs - 1:hs, cols]
                m1 = jnp.where(row == 0, prev1, pltpu.roll(u, 1, axis=0))
                m2 = jnp.where(row == 0, prev2, jnp.where(row == 1, prev1, pltpu.roll(u, 2, axis=0)))
            else:
                ext = jnp.concatenate([hist_ref[hs - 2 * ts:hs, cols], u], axis=0)
                m2, m1 = ext[0:tm], ext[ts:ts + tm]
            conv.append(cb_ref[:, cols] + cw_ref[0:1, cols] * m2 + cw_ref[1:2, cols] * m1
                        + cw_ref[2:3, cols] * u)
            hist_ref[:, cols] = u[tm - hs:tm]
        gated_ref[:, c * FFN_COLS:(c + 1) * FFN_COLS] = (_silu(conv[0]) * conv[1]).astype(BF16)
    out = x + jnp.dot(gated_ref[...], wdn_ref[...], preferred_element_type=F32)
    if has_final:
        out = _rms(out, refs[n_in - 1][...])
    y_ref[...] = out

    @pl.when(t_idx == pl.num_programs(1) - 1)
    def _():
        state_ref[...] = hist_ref[hs - 2 * ts:hs, :]


def _ffn_call(x, g, w_up, conv_w, conv_b, w_down, hist0, g_final, *, tm, ts):
    ngrp, rows, d = x.shape
    params = [_whole(a) for a in (g, w_up, conv_w, conv_b, w_down)]
    f2 = params[1][0].shape[-1]
    f_dim = f2 // 2
    hs = max(8, 2 * ts)
    assert rows % tm == 0 and tm >= hs and f_dim % FFN_COLS == 0 and (ts == 1 or ts % 8 == 0)
    args = [x] + [a for a, _ in params]
    in_specs = [pl.BlockSpec((None, tm, d), lambda s, t: (s, t, 0))] + [spec for _, spec in params]
    st_spec = pl.BlockSpec((None, 2 * ts, f2), lambda s, t: (s, 0, 0))
    if hist0 is not None:
        in_specs.append(st_spec)
        args.append(hist0)
    if g_final is not None:
        in_specs.append(_whole(g_final)[1])
        args.append(g_final)
    return pl.pallas_call(
        functools.partial(_ffn_kernel, tm=tm, ts=ts, hs=hs, f_dim=f_dim,
                          has_hist=hist0 is not None, has_final=g_final is not None),
        grid=(ngrp, rows // tm),
        in_specs=in_specs,
        out_specs=[pl.BlockSpec((None, tm, d), lambda s, t: (s, t, 0)), st_spec],
        out_shape=[jax.ShapeDtypeStruct((ngrp, rows, d), F32),
                   jax.ShapeDtypeStruct((ngrp, 2 * ts, f2), F32)],
        scratch_shapes=[pltpu.VMEM((tm, f_dim), BF16), pltpu.VMEM((hs, f2), F32)],
        compiler_params=_cparams("arbitrary", "arbitrary"),
        name="conv_ffn",
    )(*args)


def _trunk(x, mem_k, mem_v, states, p, *, prompt):
    bsz, seq, d = x.shape
    depth = p["norm_mix"].shape[0]
    m = bsz * seq
    x2 = x.reshape(m, d)
    new = {"hgrn": [], "s5_re": [], "s5_im": [], "gla": [], "conv": []}
    if prompt:
        gla_tiles = dict(tb=512, chunk=64, nb=1)
        attn_tq = 1024
    else:
        gla_tiles = dict(tb=seq, chunk=16, nb=8)
        attn_tq = seq
    for l in range(depth):
        g_mix = _pick(p["norm_mix"], l)
        if l % 2 == 0:
            e = l // 2
            kw = N_HEADS * HEAD_DK
            x3 = x2.reshape(bsz, seq, d)
            hgrn_par = [p["hgrn_lb"], p["hgrn_gnorm"][e].reshape(1, -1)]
            mats = p["s5_mats"][e]
            if prompt:
                o_a, s_a, u = _gla_call("hgrn", (x3, g_mix, _pick(p["w_in_ab"], e)), hgrn_par, None,
                                        layer=l, dv=kw // N_HEADS, extra_widths=(kw,), **gla_tiles)
                o_b, sr, si = _s5_call(u, 0, mats, None, tc=64, cw=512)
            else:
                proj = _norm_proj(x2, g_mix, _pick(p["w_in_ab"], e)).reshape(bsz, seq, -1)
                o_a, s_a = _gla_call("hgrn", proj, hgrn_par, states["hgrn"][e],
                                     layer=l, dv=kw // N_HEADS, **gla_tiles)
                x0 = (states["s5_re"][e].reshape(bsz, -1), states["s5_im"][e].reshape(bsz, -1))
                o_b, sr, si = _s5_call(proj, 4, mats, x0, tc=seq, cw=128)
            mixed = [o_a, o_b]
            mixed_w = [_pick(p["w_out_ab"], e, rows=kw, row_block=0), _pick(p["w_out_ab"], e, rows=kw, row_block=1)]
            new["hgrn"].append(s_a)
            new["s5_re"].append(sr.reshape(bsz, -1, S5_STATE))
            new["s5_im"].append(si.reshape(bsz, -1, S5_STATE))
        else:
            o_idx = l // 2
            if prompt:
                src_c = (x2.reshape(bsz, seq, d), g_mix, _pick(p["w_in_c"], o_idx))
            else:
                src_c = _norm_proj(x2, g_mix, _pick(p["w_in_c"], o_idx)).reshape(bsz, seq, -1)
            o_c, s_c = _gla_call(
                "gla", src_c,
                [p["gla_w_gate"][o_idx], p["gla_b_gate"][o_idx].reshape(1, -1),
                 p["gla_gnorm"][o_idx].reshape(1, -1)],
                None if states is None else states["gla"][o_idx],
                layer=l, dv=d // N_HEADS, **{**gla_tiles, **(dict(chunk=128) if prompt else {})})
            mixed, mixed_w = [o_c], [_pick(p["w_out_c"], o_idx)]
            new["gla"].append(s_c)
        g_cross, w_q, w_o = _pick(p["norm_cross"], l), _pick(p["xa_w_q"], l), _pick(p["xa_w_o"], l)
        if prompt:
            x2 = _xattn_call(x2.reshape(bsz, seq, d), mixed, mixed_w, g_cross, w_q, w_o,
                             mem_k, mem_v, l, tq=attn_tq).reshape(m, d)
        else:
            x2 = _proj_res(x2, [a.reshape(m, -1) for a in mixed], mixed_w)
            q = _norm_proj(x2, g_cross, w_q)
            o_x = _attn_rows_call(q.reshape(bsz, seq, d), mem_k, mem_v, l, nb=4)
            x2 = _proj_res(x2, [o_x.reshape(m, d)], [w_o])
        g_final = p["norm_final"] if l == depth - 1 else None
        ffn_w = tuple(_pick(p[name], l) for name in
                      ("norm_ffn", "ffn_w_up", "ffn_conv_w", "ffn_conv_b", "ffn_w_down"))
        if prompt:
            y, cst = _ffn_call(x2.reshape(bsz, seq, d), *ffn_w, None, g_final, tm=1024, ts=1)
            x2 = y.reshape(m, d)
        else:
            xt = x2.reshape(bsz, seq, d).transpose(1, 0, 2).reshape(1, m, d)
            hist0 = states["conv"][l].transpose(1, 0, 2).reshape(1, 2 * bsz, -1)
            y, cst = _ffn_call(xt, *ffn_w, hist0, g_final, tm=m, ts=bsz)
            x2 = y.reshape(seq, bsz, d).transpose(1, 0, 2).reshape(m, d)
            cst = cst.reshape(2, bsz, -1).transpose(1, 0, 2)
        new["conv"].append(cst)
    return x2.reshape(bsz, seq, d), new


def kernel(x_prompt, x_sample, mem_prompt, cache_mem_k, cache_mem_v, state_hgrn, state_s5_re, state_s5_im, state_gla, state_ffn_conv, norm_mix, norm_cross, norm_mem, norm_ffn, norm_final, w_in_ab, hgrn_lb, hgrn_gnorm, s5_lam_re, s5_lam_im, s5_log_step, s5_b_re, s5_b_im, s5_c_re, s5_c_im, s5_d, s5_w_glu, s5_b_glu, w_out_ab, w_in_c, gla_w_gate_up, gla_b_gate, gla_gnorm, w_out_c, xa_w_q, xa_w_kv, xa_w_o, ffn_w_up, ffn_conv_w, ffn_conv_b, ffn_w_down):
    depth, d = norm_mix.shape

    gla_cols = w_in_c.shape[2]
    gate_rank = gla_w_gate_up.shape[1]
    pad_c = (-gla_cols) % 128
    w_in_c_p = jnp.pad(w_in_c, ((0, 0), (0, 0), (0, pad_c))).astype(BF16)
    gla_w_gate = jnp.pad(gla_w_gate_up, ((0, 0), (0, 128 - gate_rank), (0, 0))).astype(BF16)

    s5_mats = []
    for e in range(s5_lam_re.shape[0]):
        a_re, a_im, bb_re, bb_im = _s5_prep(s5_lam_re[e], s5_lam_im[e], s5_log_step[e],
                                            s5_b_re[e], s5_b_im[e])
        s5_mats.append(_s5_matrices(a_re, a_im, bb_re, bb_im, s5_c_re[e], s5_c_im[e], s5_d[e],
                                    s5_w_glu[e], s5_b_glu[e]))

    row = lambda a: a.reshape(a.shape[0], 1, a.shape[1])
    p = dict(norm_mix=row(norm_mix), norm_cross=row(norm_cross), norm_ffn=row(norm_ffn),
             norm_final=norm_final.reshape(1, d),
             w_in_ab=w_in_ab.astype(BF16), hgrn_lb=hgrn_lb, hgrn_gnorm=hgrn_gnorm, s5_mats=s5_mats,
             w_out_ab=w_out_ab.astype(BF16), w_in_c=w_in_c_p, gla_w_gate=gla_w_gate,
             gla_b_gate=gla_b_gate, gla_gnorm=gla_gnorm, w_out_c=w_out_c.astype(BF16),
             xa_w_q=xa_w_q.astype(BF16), xa_w_o=xa_w_o.astype(BF16),
             ffn_w_up=ffn_w_up.astype(BF16), ffn_conv_w=ffn_conv_w, ffn_conv_b=row(ffn_conv_b),
             ffn_w_down=ffn_w_down.astype(BF16))

    mem_k_p, mem_v_p = _mem_kv(mem_prompt, norm_mem.reshape(depth, 1, d), xa_w_kv.astype(BF16))
    y_prompt, st_p = _trunk(x_prompt, mem_k_p, mem_v_p, None, p, prompt=True)

    states = dict(hgrn=state_hgrn, s5_re=state_s5_re, s5_im=state_s5_im, gla=state_gla,
                  conv=state_ffn_conv)
    y_sample, st_s = _trunk(x_sample, cache_mem_k, cache_mem_v, states, p, prompt=False)

    stack = lambda xs: xs[0][None] if len(xs) == 1 else jnp.stack(xs)
    return (y_prompt, y_sample,
            stack(st_p["hgrn"]), stack(st_p["s5_re"]), stack(st_p["s5_im"]), stack(st_p["gla"]),
            mem_k_p, mem_v_p, stack(st_p["conv"]),
            stack(st_s["hgrn"]), stack(st_s["s5_re"]), stack(st_s["s5_im"]), stack(st_s["gla"]),
            stack(st_s["conv"]))
```

```python
import functools
import math

import jax
import jax.numpy as jnp
from jax import lax
from jax.experimental import pallas as pl
from jax.experimental.pallas import tpu as pltpu

F32 = jnp.float32
BF16 = jnp.bfloat16

EPS = 1e-6
S5_MAX_RE = -1e-4
GLA_GATE_TAU = 16.0
N_HEADS = 4
HEAD_DK = 128
S5_GROUP = 16
S5_STATE = 64
SUB_BLOCK = 16
GLA_SAFE_DECAY = 64.0
VMEM_LIMIT = 56 * 1024 * 1024

_NT = (((1,), (1,)), ((), ()))
_TN = (((0,), (0,)), ((), ()))


def _cparams(*sem):
    return pltpu.CompilerParams(dimension_semantics=sem, vmem_limit_bytes=VMEM_LIMIT)


def _rms(x, g):
    return x * lax.rsqrt(jnp.mean(x * x, axis=-1, keepdims=True) + EPS) * g


def _sigmoid(x):
    return 1.0 / (1.0 + jnp.exp(-x))


def _silu(x):
    return x * _sigmoid(x)


def _row_tile(rows, want):
    t = min(rows, want)
    assert rows % t == 0, (rows, t)
    return t


def _pick(a, layer, rows=None, row_block=0, cols=None, col_block=0):
    block = (None, rows or a.shape[1], cols or a.shape[2])
    return a, pl.BlockSpec(block, lambda *_: (layer, row_block, col_block))


def _whole(a):
    return a if isinstance(a, tuple) else (a, pl.BlockSpec(a.shape, lambda *_: (0,) * a.ndim))


def _norm_proj_kernel(x_ref, g_ref, w_ref, o_ref):
    h = _rms(x_ref[...], g_ref[...]).astype(BF16)
    n = o_ref.shape[1]
    for c0 in range(0, n, 512):
        cw = min(512, n - c0)
        o_ref[:, c0:c0 + cw] = jnp.dot(h, w_ref[:, c0:c0 + cw], preferred_element_type=F32)


def _norm_proj(x, g, w, *, tm=512):
    m, k = x.shape
    (g, g_spec), (w, w_spec) = _whole(g), _whole(w)
    n = w.shape[-1]
    tm = _row_tile(m, tm)
    return pl.pallas_call(
        _norm_proj_kernel,
        grid=(m // tm,),
        in_specs=[pl.BlockSpec((tm, k), lambda i: (i, 0)), g_spec, w_spec],
        out_specs=pl.BlockSpec((tm, n), lambda i: (i, 0)),
        out_shape=jax.ShapeDtypeStruct((m, n), F32),
        compiler_params=_cparams("arbitrary"),
        name="norm_proj",
    )(x, g, w)


def _mem_kv_kernel(x_ref, g_ref, w_ref, k_ref, v_ref):
    nb, n_mem, d = x_ref.shape
    h = _rms(x_ref[...].reshape(nb * n_mem, d), g_ref[...]).astype(BF16)
    hd = d // N_HEADS
    for o_ref, base in ((k_ref, 0), (v_ref, d)):
        for hh in range(N_HEADS):
            kv = jnp.dot(h, w_ref[:, base + hh * hd:base + (hh + 1) * hd], preferred_element_type=F32)
            for i in range(nb):
                o_ref[i, :, hh, :] = kv[i * n_mem:(i + 1) * n_mem]


def _mem_kv(mem, g, w, *, nb=2):
    bsz, n_mem, d = mem.shape
    depth = w.shape[0]
    assert bsz % nb == 0
    out_spec = pl.BlockSpec((None, nb, n_mem, N_HEADS, d // N_HEADS), lambda l, b: (l, b, 0, 0, 0))
    out_shape = jax.ShapeDtypeStruct((depth, bsz, n_mem, N_HEADS, d // N_HEADS), F32)
    return pl.pallas_call(
        _mem_kv_kernel,
        grid=(depth, bsz // nb),
        in_specs=[pl.BlockSpec((nb, n_mem, d), lambda l, b: (b, 0, 0)),
                  pl.BlockSpec((None, 1, d), lambda l, b: (l, 0, 0)),
                  pl.BlockSpec((None, d, 2 * d), lambda l, b: (l, 0, 0))],
        out_specs=[out_spec, out_spec],
        out_shape=[out_shape, out_shape],
        compiler_params=_cparams("arbitrary", "arbitrary"),
        name="mem_kv",
    )(mem, g, w)


def _proj_res_kernel(res_ref, *refs, n_in):
    acc = res_ref[...]
    for a_ref, w_ref in zip(refs[:n_in], refs[n_in:2 * n_in]):
        acc = acc + jnp.dot(a_ref[...].astype(BF16), w_ref[...], preferred_element_type=F32)
    refs[2 * n_in][...] = acc


def _proj_res(res, a_list, w_list, *, tm=512):
    m, n = res.shape
    tm = _row_tile(m, tm)
    n_in = len(a_list)
    w_list = [_whole(w) for w in w_list]
    in_specs = [pl.BlockSpec((tm, n), lambda i: (i, 0))]
    in_specs += [pl.BlockSpec((tm, a.shape[1]), lambda i: (i, 0)) for a in a_list]
    in_specs += [spec for _, spec in w_list]
    return pl.pallas_call(
        functools.partial(_proj_res_kernel, n_in=n_in),
        grid=(m // tm,),
        in_specs=in_specs,
        out_specs=pl.BlockSpec((tm, n), lambda i: (i, 0)),
        out_shape=jax.ShapeDtypeStruct((m, n), F32),
        compiler_params=_cparams("arbitrary"),
        name="proj_res",
    )(res, *a_list, *[w for w, _ in w_list])


def _cumsum_rows(x, c):
    hi = x.astype(BF16)
    rest = x - hi.astype(F32)
    mid = rest.astype(BF16)
    lo = (rest - mid.astype(F32)).astype(BF16)
    n = max(c, min(x.shape[0], 128))
    r = lax.broadcasted_iota(jnp.int32, (n, n), 0)
    col = lax.broadcasted_iota(jnp.int32, (n, n), 1)
    tri = jnp.where((r >= col) & (r // c == col // c), 1.0, 0.0).astype(BF16)
    tri3 = jnp.concatenate([tri, tri, tri], axis=1)
    out = [jnp.dot(tri3, jnp.concatenate([hi[s:s + n], mid[s:s + n], lo[s:s + n]], axis=0),
                   preferred_element_type=F32) for s in range(0, x.shape[0], n)]
    return out[0] if len(out) == 1 else jnp.concatenate(out, axis=0)


def _gla_head(qh, kh, bh, vh, st_ref, small_decay, live_rows):
    c = qh.shape[0]
    sb = min(SUB_BLOCK, c)
    vb = vh.astype(BF16)
    st = st_ref[...]
    b_last = bh[c - 1:c, :]
    q_in = (qh * jnp.exp(bh)).astype(BF16)
    o_inter = lax.dot_general(q_in, st.astype(BF16), _NT, preferred_element_type=F32)
    if small_decay:
        k_up = kh * jnp.exp(-bh)
        a = lax.dot_general(q_in, k_up.astype(BF16), _NT, preferred_element_type=F32)
        causal = (lax.broadcasted_iota(jnp.int32, (c, c), 0) >= lax.broadcasted_iota(jnp.int32, (c, c), 1))
        kd = (k_up * jnp.exp(b_last)).astype(BF16)
        st_ref[...] = st * jnp.exp(b_last) + lax.dot_general(vb, kd, _TN, preferred_element_type=F32)
        return o_inter + jnp.dot(jnp.where(causal, a, 0.0).astype(BF16), vb, preferred_element_type=F32)
    kd = (kh * jnp.exp(b_last - bh)).astype(BF16)
    st_ref[...] = st * jnp.exp(b_last) + lax.dot_general(vb, kd, _TN, preferred_element_type=F32)
    rows = lax.broadcasted_iota(jnp.int32, (sb, 1), 0)
    parts = []
    for s in range(c // sb):
        r0 = s * sb
        qs, ks, bs, vs = qh[r0:r0 + sb], kh[r0:r0 + sb], bh[r0:r0 + sb], vh[r0:r0 + sb]
        acc = o_inter[r0:r0 + sb]
        if s > 0:
            ref_b = bh[r0 - 1:r0, :]
            qf = (qs * jnp.exp(bs - ref_b)).astype(BF16)
            kf = (kh[0:r0] * jnp.exp(ref_b - bh[0:r0])).astype(BF16)
            a_off = lax.dot_general(qf, kf, _NT, preferred_element_type=F32)
            acc = acc + jnp.dot(a_off.astype(BF16), vb[0:r0], preferred_element_type=F32)
        for j in range(max(0, min(sb, live_rows - r0))):
            w = jnp.exp(jnp.minimum(bs - bs[j:j + 1], 0.0)) * qs * ks[j:j + 1]
            col = jnp.where(rows >= j, jnp.sum(w, axis=-1, keepdims=True), 0.0)
            acc = acc + col * vs[j:j + 1]
        parts.append(acc)
    return parts[0] if len(parts) == 1 else jnp.concatenate(parts, axis=0)


def _gla_kernel(*refs, mode, layer, dv, nb, rows_in, chunk, n_chunks, has_s0, in_widths):
    n_act = 4 if mode == "hgrn" else 5
    n_src = 3 if in_widths else n_act
    n_par = 2 if mode == "hgrn" else 3
    n_in = n_src + n_par + (1 if has_s0 else 0)
    n_out = 2 + (len(in_widths) - n_act if in_widths else 0)
    ins, (o_ref, sout_ref), scr = refs[:n_in], refs[n_in:n_in + 2], refs[n_in + n_out:]
    pars = ins[n_src:n_src + n_par]
    st_ref = scr[0]
    pad_refs = scr[1:]
    t_idx = pl.program_id(1)
    padded = rows_in < chunk
    assert (padded and not in_widths) or nb == 1

    @pl.when(t_idx == 0)
    def _():
        for i in range(nb):
            for h in range(N_HEADS):
                if has_s0:
                    st_ref[i * N_HEADS + h] = ins[-1][i, h].T
                else:
                    st_ref[i * N_HEADS + h] = jnp.zeros(st_ref.shape[1:], F32)

    if in_widths:
        x_ref, gm_ref, win_ref = ins[:n_src]
        hx = _rms(x_ref[0], gm_ref[...]).astype(BF16)
        offs = [sum(in_widths[:i]) for i in range(len(in_widths))]
        ld = [jnp.dot(hx, win_ref[:, o:o + w], preferred_element_type=F32) for o, w in zip(offs, in_widths)]
        for extra_ref, extra in zip(refs[n_in + 2:n_in + n_out], ld[n_act:]):
            extra_ref[0] = extra
        ld = ld[:n_act]
    elif padded:
        for p_ref, a_ref in zip(pad_refs, ins[:n_act]):
            p_ref[...] = jnp.zeros(p_ref.shape, F32)
            for i in range(nb):
                p_ref[i * chunk:i * chunk + rows_in, :] = a_ref[i]
        ld = [a[...] for a in pad_refs]
    else:
        ld = [a[0] for a in ins[:n_act]]
    groups = nb if padded else n_chunks
    span = groups * chunk

    if mode == "hgrn":
        q_raw, f, v, gate = ld
        lb_ref, gn_ref = pars
        lbv = lb_ref[...]
        e = jnp.exp(lbv - jnp.max(lbv, axis=0, keepdims=True))
        lb = jnp.sum(e[0:layer + 1], axis=0, keepdims=True) / jnp.sum(e, axis=0, keepdims=True)
        forget = lb + (1.0 - lb) * _sigmoid(f)
        k = 1.0 - forget
        lg = jnp.log(forget)
        q = _silu(q_raw)
    else:
        q_raw, k, v, gate, gd = ld
        wg_ref, bg_ref, gn_ref = pars
        z = jnp.dot(gd.astype(BF16), wg_ref[...], preferred_element_type=F32) + bg_ref[...]
        lg = (jnp.minimum(z, 0.0) - jnp.log(1.0 + jnp.exp(-jnp.abs(z)))) / GLA_GATE_TAU
        q = q_raw * (HEAD_DK ** -0.5)
    if padded:
        live = lax.broadcasted_iota(jnp.int32, (span, 1), 0) % chunk < rows_in
        lg = jnp.where(live, lg, 0.0)
        k = jnp.where(live, k, 0.0)
    b = _cumsum_rows(lg, chunk)

    def piece(x, i, h, width):
        return x[i * chunk:(i + 1) * chunk, h * width:(h + 1) * width]

    def heads(small_decay):
        rows = []
        for i in range(groups):
            cols = []
            for h in range(N_HEADS):
                cols.append(_gla_head(piece(q, i, h, HEAD_DK), piece(k, i, h, HEAD_DK),
                                      piece(b, i, h, HEAD_DK), piece(v, i, h, dv),
                                      st_ref.at[(i if padded else 0) * N_HEADS + h], small_decay,
                                      rows_in if padded else chunk))
            rows.append(jnp.concatenate(cols, axis=1))
        return rows[0] if groups == 1 else jnp.concatenate(rows, axis=0)

    if chunk <= SUB_BLOCK:
        o_raw = heads(False)
    else:
        o_raw = lax.cond(jnp.min(b) >= -GLA_SAFE_DECAY,
                         functools.partial(heads, True), functools.partial(heads, False))
    o_all = jnp.concatenate(
        [_rms(o_raw[:, h * dv:(h + 1) * dv], gn_ref[...]) * _silu(gate[:, h * dv:(h + 1) * dv])
         for h in range(N_HEADS)], axis=1).astype(o_ref.dtype)
    if padded:
        for i in range(nb):
            o_ref[i] = o_all[i * chunk:i * chunk + rows_in]
    else:
        o_ref[0] = o_all

    @pl.when(t_idx == pl.num_programs(1) - 1)
    def _():
        for i in range(nb):
            for h in range(N_HEADS):
                sout_ref[i, h] = st_ref[i * N_HEADS + h].T


def _gla_call(mode, src, params, s0, *, layer, dv, tb, chunk, nb, extra_widths=()):
    fused_in = isinstance(src, tuple)
    proj = src[0] if fused_in else src
    bsz, seq, _ = proj.shape
    rows_in = min(tb, seq)
    if rows_in < chunk:
        assert seq == rows_in and bsz % nb == 0
        nt, n_chunks = 1, 1
    else:
        assert seq % tb == 0 and tb % chunk == 0 and nb == 1
        nt, n_chunks = seq // tb, tb // chunk
    kw, vw = N_HEADS * HEAD_DK, N_HEADS * dv

    def act(width, col_block):
        return pl.BlockSpec((nb, rows_in, width), lambda b, t: (b, t, col_block))

    def whole(a):
        return pl.BlockSpec(a.shape, lambda b, t: (0,) * a.ndim)

    if mode == "hgrn":
        act_specs = [act(kw, 0), act(kw, 1), act(vw, 2), act(vw, 3)]
        act_widths = [kw, kw, vw, vw]
    else:
        act_specs = [act(kw, 0), act(kw, 1), act(vw, kw * 2 // vw), act(vw, kw * 2 // vw + 1),
                     act(128, (2 * kw + 2 * vw) // 128)]
        act_widths = [kw, kw, vw, vw, 128]
    if fused_in:
        x, g_mix, w_in = src[0], _whole(src[1]), _whole(src[2])
        in_specs = [pl.BlockSpec((nb, rows_in, x.shape[-1]), lambda b, t: (b, t, 0)), g_mix[1], w_in[1]]
        args = [x, g_mix[0], w_in[0]]
    else:
        in_specs, args = act_specs, [proj] * len(act_specs)
    in_specs = in_specs + [whole(p) for p in params]
    args = args + list(params)
    state_spec = pl.BlockSpec((nb, N_HEADS, HEAD_DK, dv), lambda b, t: (b, 0, 0, 0))
    if s0 is not None:
        in_specs.append(state_spec)
        args.append(s0)
    scratch = [pltpu.VMEM((nb * N_HEADS, dv, HEAD_DK), F32)]
    if rows_in < chunk:
        scratch += [pltpu.VMEM((nb * chunk, w), F32) for w in act_widths]
    return pl.pallas_call(
        functools.partial(_gla_kernel, mode=mode, layer=layer, dv=dv, nb=nb, rows_in=rows_in,
                          chunk=chunk, n_chunks=n_chunks, has_s0=s0 is not None,
                          in_widths=tuple(act_widths) + tuple(extra_widths) if fused_in else None),
        grid=(bsz // nb, nt),
        in_specs=in_specs,
        out_specs=[pl.BlockSpec((nb, rows_in, vw), lambda b, t: (b, t, 0)), state_spec]
        + [pl.BlockSpec((nb, rows_in, w), lambda b, t: (b, t, 0)) for w in extra_widths],
        out_shape=[jax.ShapeDtypeStruct((bsz, seq, vw), BF16),
                   jax.ShapeDtypeStruct((bsz, N_HEADS, HEAD_DK, dv), F32)]
        + [jax.ShapeDtypeStruct((bsz, seq, w), F32) for w in extra_widths],
        scratch_shapes=scratch,
        compiler_params=_cparams("arbitrary", "arbitrary"),
        name="gla_" + mode,
    )(*args)


def _s5_prep_kernel(lre_ref, lim_ref, ls_ref, lre_x_ref, lim_x_ref, ls_x_ref, bre_ref, bim_ref,
                    are_ref, aim_ref, bbre_ref, bbim_ref):
    def disc(lre, lim, ls):
        lr = jnp.minimum(lre, S5_MAX_RE)
        dt = jnp.exp(ls)
        mag = jnp.exp(lr * dt)
        a_re = mag * jnp.cos(lim * dt)
        a_im = mag * jnp.sin(lim * dt)
        den = lr * lr + lim * lim
        z_re = ((a_re - 1.0) * lr + a_im * lim) / den
        z_im = (a_im * lr - (a_re - 1.0) * lim) / den
        return a_re, a_im, z_re, z_im

    a_re, a_im, _, _ = disc(lre_ref[...], lim_ref[...], ls_ref[...])
    are_ref[...] = a_re
    aim_ref[...] = a_im
    _, _, z_re, z_im = disc(lre_x_ref[...], lim_x_ref[...], ls_x_ref[...])
    bbre_ref[...] = z_re * bre_ref[...] - z_im * bim_ref[...]
    bbim_ref[...] = z_re * bim_ref[...] + z_im * bre_ref[...]


def _s5_prep(lam_re, lam_im, log_step, b_re, b_im):
    g, n = lam_re.shape
    p = b_re.shape[-1]
    ls = jnp.broadcast_to(log_step[:, None], (g, n))
    rep = lambda a: jnp.repeat(a, p, axis=1)
    outs = pl.pallas_call(
        _s5_prep_kernel,
        out_shape=[jax.ShapeDtypeStruct((g, n), F32)] * 2 + [jax.ShapeDtypeStruct((g, n * p), F32)] * 2,
        name="s5_prep",
    )(lam_re, lam_im, ls, rep(lam_re), rep(lam_im), rep(ls),
      b_re.reshape(g, n * p), b_im.reshape(g, n * p))
    a_re, a_im, bb_re, bb_im = outs
    return a_re, a_im, bb_re.reshape(g, n, p), bb_im.reshape(g, n, p)


def _s5_kernel(*refs, bg, tc, nt, cw, has_x0):
    n_in = 9 if has_x0 else 7
    u_ref, bm_ref, cm_ref, a_ref, d_ref, wg_ref, bgl_ref = refs[:7]
    o_ref, sre_ref, sim_ref = refs[n_in:n_in + 3]
    utm_ref, xs_ref, st_ref = refs[n_in + 3:]
    t_idx = pl.program_id(1)
    half = xs_ref.shape[1] // 2
    uw = u_ref.shape[-1] // 2

    @pl.when(t_idx == 0)
    def _():
        for hf in range(2):
            if has_x0:
                st_ref[:, hf * 2 * half:hf * 2 * half + half] = refs[7][:, hf * half:(hf + 1) * half]
                st_ref[:, hf * 2 * half + half:(hf + 1) * 2 * half] = refs[8][:, hf * half:(hf + 1) * half]
            else:
                st_ref[...] = jnp.zeros(st_ref.shape, F32)

    for t in range(tc):
        utm_ref[t * bg:(t + 1) * bg, :] = u_ref[:, t, :]
    u = utm_ref[...]
    ys = []
    for hf in range(2):
        xs_ref[...] = jnp.dot(u[:, hf * uw:(hf + 1) * uw].astype(BF16), bm_ref[hf],
                              preferred_element_type=F32)
        base = hf * 2 * half
        for c0 in range(0, half, cw):
            ar = a_ref[0:1, base + c0:base + c0 + cw]
            ai = a_ref[0:1, base + half + c0:base + half + c0 + cw]
            xr = st_ref[:, base + c0:base + c0 + cw]
            xi = st_ref[:, base + half + c0:base + half + c0 + cw]
            for t in range(tc):
                rows = slice(t * bg, (t + 1) * bg)
                nr = ar * xr - ai * xi + xs_ref[rows, c0:c0 + cw]
                ni = ar * xi + ai * xr + xs_ref[rows, half + c0:half + c0 + cw]
                xs_ref[rows, c0:c0 + cw] = nr
                xs_ref[rows, half + c0:half + c0 + cw] = ni
                xr, xi = nr, ni
            st_ref[:, base + c0:base + c0 + cw] = xr
            st_ref[:, base + half + c0:base + half + c0 + cw] = xi
        ys.append(jnp.dot(xs_ref[...].astype(BF16), cm_ref[hf], preferred_element_type=F32))
    y = jnp.concatenate(ys, axis=1) + d_ref[...] * u
    y = 0.5 * y * (1.0 + jnp.tanh(math.sqrt(2.0 / math.pi) * (y + 0.044715 * (y * y * y))))
    gate = jnp.dot(y.astype(BF16), wg_ref[...], preferred_element_type=F32) + bgl_ref[...]
    utm_ref[...] = y * _sigmoid(gate)
    for t in range(tc):
        o_ref[:, t, :] = utm_ref[t * bg:(t + 1) * bg, :]

    @pl.when(t_idx == nt - 1)
    def _():
        for hf in range(2):
            sre_ref[:, hf * half:(hf + 1) * half] = st_ref[:, hf * 2 * half:hf * 2 * half + half]
            sim_ref[:, hf * half:(hf + 1) * half] = st_ref[:, hf * 2 * half + half:(hf + 1) * 2 * half]


def _s5_call(u_src, col_block, mats, x0, *, tc, cw):
    bg, seq, _ = u_src.shape
    assert seq % tc == 0
    nt = seq // tc
    bmat, cmat, a_flat, d_row, w_glu, b_glu = mats
    width = d_row.shape[1]
    nstate = a_flat.shape[1] // 2
    whole = lambda a: pl.BlockSpec(a.shape, lambda g, t: (0,) * a.ndim)
    in_specs = [pl.BlockSpec((bg, tc, width), lambda g, t: (0, t, col_block))]
    in_specs += [whole(m) for m in mats]
    args = [u_src] + list(mats)
    st_spec = pl.BlockSpec((bg, nstate), lambda g, t: (0, 0))
    if x0 is not None:
        in_specs += [st_spec, st_spec]
        args += list(x0)
    scratch = [pltpu.VMEM((bg * tc, width), F32), pltpu.VMEM((bg * tc, nstate), F32),
               pltpu.VMEM((bg, 2 * nstate), F32)]
    return pl.pallas_call(
        functools.partial(_s5_kernel, bg=bg, tc=tc, nt=nt, cw=cw, has_x0=x0 is not None),
        grid=(1, nt),
        in_specs=in_specs,
        out_specs=[pl.BlockSpec((bg, tc, width), lambda g, t: (0, t, 0)), st_spec, st_spec],
        out_shape=[jax.ShapeDtypeStruct((bg, seq, width), F32),
                   jax.ShapeDtypeStruct((bg, nstate), F32),
                   jax.ShapeDtypeStruct((bg, nstate), F32)],
        scratch_shapes=scratch,
        compiler_params=_cparams("arbitrary", "arbitrary"),
        name="s5",
    )(*args)


def _s5_matrices(a_re, a_im, bb_re, bb_im, c_re, c_im, d, w_glu, b_glu):
    g, n, p = bb_re.shape
    gh = g // 2
    eye = jnp.eye(gh, dtype=F32)

    def block_diag(t):
        return (eye[:, None, :, None] * t[:, :, None, :]).reshape(gh * t.shape[1], gh * t.shape[2])

    def in_mat(bb):
        return block_diag(bb.transpose(0, 2, 1))

    def out_mat(cc):
        return block_diag(cc.transpose(0, 2, 1))

    bmat = jnp.stack([jnp.concatenate([in_mat(bb_re[h * gh:(h + 1) * gh]),
                                       in_mat(bb_im[h * gh:(h + 1) * gh])], axis=1)
                      for h in range(2)]).astype(BF16)
    cmat = jnp.stack([jnp.concatenate([out_mat(c_re[h * gh:(h + 1) * gh]),
                                       out_mat(-c_im[h * gh:(h + 1) * gh])], axis=0)
                      for h in range(2)]).astype(BF16)
    a_flat = jnp.concatenate([jnp.concatenate([a_re[h * gh:(h + 1) * gh].reshape(1, gh * n),
                                               a_im[h * gh:(h + 1) * gh].reshape(1, gh * n)], axis=1)
                              for h in range(2)], axis=1)
    return (bmat, cmat, a_flat, d.reshape(1, g * p), w_glu.astype(BF16), b_glu.reshape(1, -1))


def _xattn_kernel(*refs, n_pre):
    x_ref = refs[0]
    g_ref, wq_ref, wo_ref, k_ref, v_ref, y_ref, kt_ref, vt_ref = refs[1 + 2 * n_pre:]

    @pl.when(pl.program_id(1) == 0)
    def _():
        kt_ref[...] = jnp.transpose(k_ref[...], (1, 0, 2)).astype(BF16)
        vt_ref[...] = jnp.transpose(v_ref[...], (1, 0, 2)).astype(BF16)

    x = x_ref[...]
    for a_ref, w_ref in zip(refs[1:1 + n_pre], refs[1 + n_pre:1 + 2 * n_pre]):
        x = x + jnp.dot(a_ref[...].astype(BF16), w_ref[...], preferred_element_type=F32)
    q = jnp.dot(_rms(x, g_ref[...]).astype(BF16), wq_ref[...], preferred_element_type=F32).astype(BF16)
    hd = q.shape[1] // N_HEADS
    outs = []
    for h in range(N_HEADS):
        s = lax.dot_general(q[:, h * hd:(h + 1) * hd], kt_ref[h], _NT,
                            preferred_element_type=F32) * (hd ** -0.5)
        p = jnp.exp(s - jnp.max(s, axis=-1, keepdims=True))
        p = p / jnp.sum(p, axis=-1, keepdims=True)
        outs.append(jnp.dot(p.astype(BF16), vt_ref[h], preferred_element_type=F32).astype(BF16))
    y_ref[...] = x + jnp.dot(jnp.concatenate(outs, axis=1), wo_ref[...], preferred_element_type=F32)


def _xattn_call(x, pre_a, pre_w, g, w_q, w_o, mem_k, mem_v, layer, *, tq):
    bsz, seq, d = x.shape
    n_mem, nh, hd = mem_k.shape[2:]
    assert seq % tq == 0
    kv_spec = pl.BlockSpec((None, None, n_mem, nh, hd), lambda b, t: (layer, b, 0, 0, 0))
    rows = lambda a: pl.BlockSpec((None, tq, a.shape[-1]), lambda b, t: (b, t, 0))
    params = [_whole(w) for w in (*pre_w, g, w_q, w_o)]
    return pl.pallas_call(
        functools.partial(_xattn_kernel, n_pre=len(pre_a)),
        grid=(bsz, seq // tq),
        in_specs=[rows(x)] + [rows(a) for a in pre_a] + [spec for _, spec in params] + [kv_spec, kv_spec],
        out_specs=rows(x),
        out_shape=jax.ShapeDtypeStruct((bsz, seq, d), F32),
        scratch_shapes=[pltpu.VMEM((nh, n_mem, hd), BF16), pltpu.VMEM((nh, n_mem, hd), BF16)],
        compiler_params=_cparams("arbitrary", "arbitrary"),
        name="xattn",
    )(x, *pre_a, *[w for w, _ in params], mem_k, mem_v)


def _attn_rows_kernel(q_ref, k_ref, v_ref, o_ref, pad_ref, *, rows_in):
    nb, rows, d = pad_ref.shape
    nblk = d // 128
    half_blk = nblk // 2
    lanes = k_ref.shape[1]
    lane_blk = lax.broadcasted_iota(jnp.int32, (1, lanes), 1) % nblk
    row_head = lax.broadcasted_iota(jnp.int32, (N_HEADS * rows, 1), 0) // rows
    live = lane_blk == row_head
    scale = (d // N_HEADS) ** -0.5
    for i in range(nb):
        pad_ref[i] = jnp.zeros((rows, d), F32)
        pad_ref[i, 0:rows_in, :] = q_ref[i]
        q = pad_ref[i]
        qx = jnp.concatenate([q[:, j * 128:(j + 1) * 128] for j in range(nblk)], axis=0)
        g = lax.dot_general(qx.astype(BF16), k_ref[i].astype(BF16), _NT, preferred_element_type=F32)
        s = jnp.concatenate(
            [g[2 * h * rows:(2 * h + 1) * rows]
             + pltpu.roll(g[(2 * h + 1) * rows:(2 * h + 2) * rows], lanes - half_blk, axis=1)
             for h in range(N_HEADS)], axis=0) * scale
        s = jnp.where(live, s, -1e30)
        e = jnp.exp(s - jnp.max(s, axis=-1, keepdims=True))
        p = e / jnp.sum(e, axis=-1, keepdims=True)
        px = jnp.concatenate(
            [blk for h in range(N_HEADS)
             for blk in (p[h * rows:(h + 1) * rows], pltpu.roll(p[h * rows:(h + 1) * rows], half_blk, axis=1))],
            axis=0)
        o = jnp.dot(px.astype(BF16), v_ref[i].astype(BF16), preferred_element_type=F32)
        o = jnp.concatenate([o[j * rows:(j + 1) * rows] for j in range(nblk)], axis=1)
        o_ref[i] = o[0:rows_in].astype(o_ref.dtype)


def _attn_rows_call(q, mem_k, mem_v, layer, *, nb):
    bsz, seq, d = q.shape
    depth, _, n_mem, nh, hd = mem_k.shape
    assert bsz % nb == 0 and hd == 256 and nh == N_HEADS and seq <= 16
    as_rows = lambda a: a.reshape(depth, bsz, n_mem, nh, 2, 128).transpose(0, 1, 2, 4, 3, 5).reshape(
        depth, bsz, n_mem * 2 * nh, 128)
    kv_spec = pl.BlockSpec((None, nb, n_mem * 2 * nh, 128), lambda b: (layer, b, 0, 0))
    return pl.pallas_call(
        functools.partial(_attn_rows_kernel, rows_in=seq),
        grid=(bsz // nb,),
        in_specs=[pl.BlockSpec((nb, seq, d), lambda b: (b, 0, 0)), kv_spec, kv_spec],
        out_specs=pl.BlockSpec((nb, seq, d), lambda b: (b, 0, 0)),
        out_shape=jax.ShapeDtypeStruct((bsz, seq, d), BF16),
        scratch_shapes=[pltpu.VMEM((nb, 8 * pl.cdiv(seq, 8), d), F32)],
        compiler_params=_cparams("arbitrary"),
        name="mem_attn_rows",
    )(q, as_rows(mem_k), as_rows(mem_v))


FFN_COLS = 256


def _ffn_kernel(*refs, tm, ts, hs, f_dim, has_hist, has_final):
    n_in = 6 + int(has_hist) + int(has_final)
    x_ref, g_ref, wup_ref, cw_ref, cb_ref, wdn_ref = refs[:6]
    y_ref, state_ref = refs[n_in:n_in + 2]
    gated_ref, hist_ref = refs[n_in + 2:n_in + 4]
    t_idx = pl.program_id(1)

    @pl.when(t_idx == 0)
    def _():
        hist_ref[...] = jnp.zeros(hist_ref.shape, F32)
        if has_hist:
            hist_ref[hs - 2 * ts:hs, :] = refs[6][...]

    x = x_ref[...]
    h = _rms(x, g_ref[...]).astype(BF16)
    row = lax.broadcasted_iota(jnp.int32, (tm, 1), 0)
    for c in range(f_dim // FFN_COLS):
        conv = []
        for part in range(2):
            cols = slice(part * f_dim + c * FFN_COLS, part * f_dim + (c + 1) * FFN_COLS)
            u = jnp.dot(h, wup_ref[:, cols], preferred_element_type=F32)
            if ts == 1:
                prev2, prev1 = hist_ref[hs - 2:hs - 1, cols], hist_ref[hs - 1:hs, cols]
                m1 = jnp.where(row == 0, prev1, pltpu.roll(u, 1, axis=0))
                m2 = jnp.where(row == 0, prev2, jnp.where(row == 1, prev1, pltpu.roll(u, 2, axis=0)))
            else:
                ext = jnp.concatenate([hist_ref[hs - 2 * ts:hs, cols], u], axis=0)
                m2, m1 = ext[0:tm], ext[ts:ts + tm]
            conv.append(cb_ref[:, cols] + cw_ref[0:1, cols] * m2 + cw_ref[1:2, cols] * m1
                        + cw_ref[2:3, cols] * u)
            hist_ref[:, cols] = u[tm - hs:tm]
        gated_ref[:, c * FFN_COLS:(c + 1) * FFN_COLS] = (_silu(conv[0]) * conv[1]).astype(BF16)
    out = x + jnp.dot(gated_ref[...], wdn_ref[...], preferred_element_type=F32)
    if has_final:
        out = _rms(out, refs[n_in - 1][...])
    y_ref[...] = out

    @pl.when(t_idx == pl.num_programs(1) - 1)
    def _():
        state_ref[...] = hist_ref[hs - 2 * ts:hs, :]


def _ffn_call(x, g, w_up, conv_w, conv_b, w_down, hist0, g_final, *, tm, ts):
    ngrp, rows, d = x.shape
    params = [_whole(a) for a in (g, w_up, conv_w, conv_b, w_down)]
    f2 = params[1][0].shape[-1]
    f_dim = f2 // 2
    hs = max(8, 2 * ts)
    assert rows % tm == 0 and tm >= hs and f_dim % FFN_COLS == 0 and (ts == 1 or ts % 8 == 0)
    args = [x] + [a for a, _ in params]
    in_specs = [pl.BlockSpec((None, tm, d), lambda s, t: (s, t, 0))] + [spec for _, spec in params]
    st_spec = pl.BlockSpec((None, 2 * ts, f2), lambda s, t: (s, 0, 0))
    if hist0 is not None:
        in_specs.append(st_spec)
        args.append(hist0)
    if g_final is not None:
        in_specs.append(_whole(g_final)[1])
        args.append(g_final)
    return pl.pallas_call(
        functools.partial(_ffn_kernel, tm=tm, ts=ts, hs=hs, f_dim=f_dim,
                          has_hist=hist0 is not None, has_final=g_final is not None),
        grid=(ngrp, rows // tm),
        in_specs=in_specs,
        out_specs=[pl.BlockSpec((None, tm, d), lambda s, t: (s, t, 0)), st_spec],
        out_shape=[jax.ShapeDtypeStruct((ngrp, rows, d), F32),
                   jax.ShapeDtypeStruct((ngrp, 2 * ts, f2), F32)],
        scratch_shapes=[pltpu.VMEM((tm, f_dim), BF16), pltpu.VMEM((hs, f2), F32)],
        compiler_params=_cparams("arbitrary", "arbitrary"),
        name="conv_ffn",
    )(*args)


def _trunk(x, mem_k, mem_v, states, p, *, prompt):
    bsz, seq, d = x.shape
    depth = p["norm_mix"].shape[0]
    m = bsz * seq
    x2 = x.reshape(m, d)
    new = {"hgrn": [], "s5_re": [], "s5_im": [], "gla": [], "conv": []}
    if prompt:
        gla_tiles = dict(tb=1024, chunk=64, nb=1)
        attn_tq = 1024
    else:
        gla_tiles = dict(tb=seq, chunk=16, nb=8)
        attn_tq = seq
    for l in range(depth):
        g_mix = _pick(p["norm_mix"], l)
        if l % 2 == 0:
            e = l // 2
            kw = N_HEADS * HEAD_DK
            x3 = x2.reshape(bsz, seq, d)
            hgrn_par = [p["hgrn_lb"], p["hgrn_gnorm"][e].reshape(1, -1)]
            mats = p["s5_mats"][e]
            if prompt:
                o_a, s_a, u = _gla_call("hgrn", (x3, g_mix, _pick(p["w_in_ab"], e)), hgrn_par, None,
                                        layer=l, dv=kw // N_HEADS, extra_widths=(kw,), **gla_tiles)
                o_b, sr, si = _s5_call(u, 0, mats, None, tc=64, cw=512)
            else:
                proj = _norm_proj(x2, g_mix, _pick(p["w_in_ab"], e)).reshape(bsz, seq, -1)
                o_a, s_a = _gla_call("hgrn", proj, hgrn_par, states["hgrn"][e],
                                     layer=l, dv=kw // N_HEADS, **gla_tiles)
                x0 = (states["s5_re"][e].reshape(bsz, -1), states["s5_im"][e].reshape(bsz, -1))
                o_b, sr, si = _s5_call(proj, 4, mats, x0, tc=seq, cw=128)
            mixed = [o_a, o_b]
            mixed_w = [_pick(p["w_out_ab"], e, rows=kw, row_block=0), _pick(p["w_out_ab"], e, rows=kw, row_block=1)]
            new["hgrn"].append(s_a)
            new["s5_re"].append(sr.reshape(bsz, -1, S5_STATE))
            new["s5_im"].append(si.reshape(bsz, -1, S5_STATE))
        else:
            o_idx = l // 2
            if prompt:
                src_c = (x2.reshape(bsz, seq, d), g_mix, _pick(p["w_in_c"], o_idx))
            else:
                src_c = _norm_proj(x2, g_mix, _pick(p["w_in_c"], o_idx)).reshape(bsz, seq, -1)
            o_c, s_c = _gla_call(
                "gla", src_c,
                [p["gla_w_gate"][o_idx], p["gla_b_gate"][o_idx].reshape(1, -1),
                 p["gla_gnorm"][o_idx].reshape(1, -1)],
                None if states is None else states["gla"][o_idx],
                layer=l, dv=d // N_HEADS, **{**gla_tiles, **(dict(chunk=128) if prompt else {})})
            mixed, mixed_w = [o_c], [_pick(p["w_out_c"], o_idx)]
            new["gla"].append(s_c)
        g_cross, w_q, w_o = _pick(p["norm_cross"], l), _pick(p["xa_w_q"], l), _pick(p["xa_w_o"], l)
        if prompt:
            x2 = _xattn_call(x2.reshape(bsz, seq, d), mixed, mixed_w, g_cross, w_q, w_o,
                             mem_k, mem_v, l, tq=attn_tq).reshape(m, d)
        else:
            x2 = _proj_res(x2, [a.reshape(m, -1) for a in mixed], mixed_w)
            q = _norm_proj(x2, g_cross, w_q)
            o_x = _attn_rows_call(q.reshape(bsz, seq, d), mem_k, mem_v, l, nb=8)
            x2 = _proj_res(x2, [o_x.reshape(m, d)], [w_o])
        g_final = p["norm_final"] if l == depth - 1 else None
        ffn_w = tuple(_pick(p[name], l) for name in
                      ("norm_ffn", "ffn_w_up", "ffn_conv_w", "ffn_conv_b", "ffn_w_down"))
        if prompt:
            y, cst = _ffn_call(x2.reshape(bsz, seq, d), *ffn_w, None, g_final, tm=1024, ts=1)
            x2 = y.reshape(m, d)
        else:
            xt = x2.reshape(bsz, seq, d).transpose(1, 0, 2).reshape(1, m, d)
            hist0 = states["conv"][l].transpose(1, 0, 2).reshape(1, 2 * bsz, -1)
            y, cst = _ffn_call(xt, *ffn_w, hist0, g_final, tm=m, ts=bsz)
            x2 = y.reshape(seq, bsz, d).transpose(1, 0, 2).reshape(m, d)
            cst = cst.reshape(2, bsz, -1).transpose(1, 0, 2)
        new["conv"].append(cst)
    return x2.reshape(bsz, seq, d), new


def kernel(x_prompt, x_sample, mem_prompt, cache_mem_k, cache_mem_v, state_hgrn, state_s5_re, state_s5_im, state_gla, state_ffn_conv, norm_mix, norm_cross, norm_mem, norm_ffn, norm_final, w_in_ab, hgrn_lb, hgrn_gnorm, s5_lam_re, s5_lam_im, s5_log_step, s5_b_re, s5_b_im, s5_c_re, s5_c_im, s5_d, s5_w_glu, s5_b_glu, w_out_ab, w_in_c, gla_w_gate_up, gla_b_gate, gla_gnorm, w_out_c, xa_w_q, xa_w_kv, xa_w_o, ffn_w_up, ffn_conv_w, ffn_conv_b, ffn_w_down):
    depth, d = norm_mix.shape

    gla_cols = w_in_c.shape[2]
    gate_rank = gla_w_gate_up.shape[1]
    pad_c = (-gla_cols) % 128
    w_in_c_p = jnp.pad(w_in_c, ((0, 0), (0, 0), (0, pad_c))).astype(BF16)
    gla_w_gate = jnp.pad(gla_w_gate_up, ((0, 0), (0, 128 - gate_rank), (0, 0))).astype(BF16)

    s5_mats = []
    for e in range(s5_lam_re.shape[0]):
        a_re, a_im, bb_re, bb_im = _s5_prep(s5_lam_re[e], s5_lam_im[e], s5_log_step[e],
                                            s5_b_re[e], s5_b_im[e])
        s5_mats.append(_s5_matrices(a_re, a_im, bb_re, bb_im, s5_c_re[e], s5_c_im[e], s5_d[e],
                                    s5_w_glu[e], s5_b_glu[e]))

    row = lambda a: a.reshape(a.shape[0], 1, a.shape[1])
    p = dict(norm_mix=row(norm_mix), norm_cross=row(norm_cross), norm_ffn=row(norm_ffn),
             norm_final=norm_final.reshape(1, d),
             w_in_ab=w_in_ab.astype(BF16), hgrn_lb=hgrn_lb, hgrn_gnorm=hgrn_gnorm, s5_mats=s5_mats,
             w_out_ab=w_out_ab.astype(BF16), w_in_c=w_in_c_p, gla_w_gate=gla_w_gate,
             gla_b_gate=gla_b_gate, gla_gnorm=gla_gnorm, w_out_c=w_out_c.astype(BF16),
             xa_w_q=xa_w_q.astype(BF16), xa_w_o=xa_w_o.astype(BF16),
             ffn_w_up=ffn_w_up.astype(BF16), ffn_conv_w=ffn_conv_w, ffn_conv_b=row(ffn_conv_b),
             ffn_w_down=ffn_w_down.astype(BF16))

    mem_k_p, mem_v_p = _mem_kv(mem_prompt, norm_mem.reshape(depth, 1, d), xa_w_kv.astype(BF16))
    y_prompt, st_p = _trunk(x_prompt, mem_k_p, mem_v_p, None, p, prompt=True)

    states = dict(hgrn=state_hgrn, s5_re=state_s5_re, s5_im=state_s5_im, gla=state_gla,
                  conv=state_ffn_conv)
    y_sample, st_s = _trunk(x_sample, cache_mem_k, cache_mem_v, states, p, prompt=False)

    stack = lambda xs: xs[0][None] if len(xs) == 1 else jnp.stack(xs)
    return (y_prompt, y_sample,
            stack(st_p["hgrn"]), stack(st_p["s5_re"]), stack(st_p["s5_im"]), stack(st_p["gla"]),
            mem_k_p, mem_v_p, stack(st_p["conv"]),
            stack(st_s["hgrn"]), stack(st_s["s5_re"]), stack(st_s["s5_im"]), stack(st_s["gla"]),
            stack(st_s["conv"]))
```

```python
import functools
import math

import jax
import jax.numpy as jnp
from jax import lax
from jax.experimental import pallas as pl
from jax.experimental.pallas import tpu as pltpu

F32 = jnp.float32
BF16 = jnp.bfloat16

EPS = 1e-6
S5_MAX_RE = -1e-4
GLA_GATE_TAU = 16.0
N_HEADS = 4
HEAD_DK = 128
S5_GROUP = 16
S5_STATE = 64
SUB_BLOCK = 16
GLA_SAFE_DECAY = 64.0
VMEM_LIMIT = 56 * 1024 * 1024

_NT = (((1,), (1,)), ((), ()))
_TN = (((0,), (0,)), ((), ()))


def _cparams(*sem):
    return pltpu.CompilerParams(dimension_semantics=sem, vmem_limit_bytes=VMEM_LIMIT)


def _rms(x, g):
    return x * lax.rsqrt(jnp.mean(x * x, axis=-1, keepdims=True) + EPS) * g


def _sigmoid(x):
    return 1.0 / (1.0 + jnp.exp(-x))


def _silu(x):
    return x * _sigmoid(x)


def _row_tile(rows, want):
    t = min(rows, want)
    assert rows % t == 0, (rows, t)
    return t


def _pick(a, layer, rows=None, row_block=0, cols=None, col_block=0):
    block = (None, rows or a.shape[1], cols or a.shape[2])
    return a, pl.BlockSpec(block, lambda *_: (layer, row_block, col_block))


def _whole(a):
    return a if isinstance(a, tuple) else (a, pl.BlockSpec(a.shape, lambda *_: (0,) * a.ndim))


def _norm_proj_kernel(x_ref, g_ref, w_ref, o_ref):
    h = _rms(x_ref[...], g_ref[...]).astype(BF16)
    n = o_ref.shape[1]
    for c0 in range(0, n, 512):
        cw = min(512, n - c0)
        o_ref[:, c0:c0 + cw] = jnp.dot(h, w_ref[:, c0:c0 + cw], preferred_element_type=F32)


def _norm_proj(x, g, w, *, tm=512):
    m, k = x.shape
    (g, g_spec), (w, w_spec) = _whole(g), _whole(w)
    n = w.shape[-1]
    tm = _row_tile(m, tm)
    return pl.pallas_call(
        _norm_proj_kernel,
        grid=(m // tm,),
        in_specs=[pl.BlockSpec((tm, k), lambda i: (i, 0)), g_spec, w_spec],
        out_specs=pl.BlockSpec((tm, n), lambda i: (i, 0)),
        out_shape=jax.ShapeDtypeStruct((m, n), F32),
        compiler_params=_cparams("arbitrary"),
        name="norm_proj",
    )(x, g, w)


def _mem_kv_kernel(x_ref, g_ref, w_ref, k_ref, v_ref):
    nb, n_mem, d = x_ref.shape
    h = _rms(x_ref[...].reshape(nb * n_mem, d), g_ref[...]).astype(BF16)
    hd = d // N_HEADS
    for o_ref, base in ((k_ref, 0), (v_ref, d)):
        for hh in range(N_HEADS):
            kv = jnp.dot(h, w_ref[:, base + hh * hd:base + (hh + 1) * hd], preferred_element_type=F32)
            for i in range(nb):
                o_ref[i, :, hh, :] = kv[i * n_mem:(i + 1) * n_mem]


def _mem_kv(mem, g, w, *, nb=2):
    bsz, n_mem, d = mem.shape
    depth = w.shape[0]
    assert bsz % nb == 0
    out_spec = pl.BlockSpec((None, nb, n_mem, N_HEADS, d // N_HEADS), lambda l, b: (l, b, 0, 0, 0))
    out_shape = jax.ShapeDtypeStruct((depth, bsz, n_mem, N_HEADS, d // N_HEADS), F32)
    return pl.pallas_call(
        _mem_kv_kernel,
        grid=(depth, bsz // nb),
        in_specs=[pl.BlockSpec((nb, n_mem, d), lambda l, b: (b, 0, 0)),
                  pl.BlockSpec((None, 1, d), lambda l, b: (l, 0, 0)),
                  pl.BlockSpec((None, d, 2 * d), lambda l, b: (l, 0, 0))],
        out_specs=[out_spec, out_spec],
        out_shape=[out_shape, out_shape],
        compiler_params=_cparams("arbitrary", "arbitrary"),
        name="mem_kv",
    )(mem, g, w)


def _proj_res_kernel(res_ref, *refs, n_in):
    acc = res_ref[...]
    for a_ref, w_ref in zip(refs[:n_in], refs[n_in:2 * n_in]):
        acc = acc + jnp.dot(a_ref[...].astype(BF16), w_ref[...], preferred_element_type=F32)
    refs[2 * n_in][...] = acc


def _proj_res(res, a_list, w_list, *, tm=512):
    m, n = res.shape
    tm = _row_tile(m, tm)
    n_in = len(a_list)
    w_list = [_whole(w) for w in w_list]
    in_specs = [pl.BlockSpec((tm, n), lambda i: (i, 0))]
    in_specs += [pl.BlockSpec((tm, a.shape[1]), lambda i: (i, 0)) for a in a_list]
    in_specs += [spec for _, spec in w_list]
    return pl.pallas_call(
        functools.partial(_proj_res_kernel, n_in=n_in),
        grid=(m // tm,),
        in_specs=in_specs,
        out_specs=pl.BlockSpec((tm, n), lambda i: (i, 0)),
        out_shape=jax.ShapeDtypeStruct((m, n), F32),
        compiler_params=_cparams("arbitrary"),
        name="proj_res",
    )(res, *a_list, *[w for w, _ in w_list])


def _cumsum_rows(x, c):
    hi = x.astype(BF16)
    rest = x - hi.astype(F32)
    mid = rest.astype(BF16)
    lo = (rest - mid.astype(F32)).astype(BF16)
    n = max(c, min(x.shape[0], 128))
    r = lax.broadcasted_iota(jnp.int32, (n, n), 0)
    col = lax.broadcasted_iota(jnp.int32, (n, n), 1)
    tri = jnp.where((r >= col) & (r // c == col // c), 1.0, 0.0).astype(BF16)
    tri3 = jnp.concatenate([tri, tri, tri], axis=1)
    out = [jnp.dot(tri3, jnp.concatenate([hi[s:s + n], mid[s:s + n], lo[s:s + n]], axis=0),
                   preferred_element_type=F32) for s in range(0, x.shape[0], n)]
    return out[0] if len(out) == 1 else jnp.concatenate(out, axis=0)


def _gla_head(qh, kh, bh, vh, st_ref, small_decay, live_rows):
    c = qh.shape[0]
    sb = min(SUB_BLOCK, c)
    vb = vh.astype(BF16)
    st = st_ref[...]
    b_last = bh[c - 1:c, :]
    q_in = (qh * jnp.exp(bh)).astype(BF16)
    o_inter = lax.dot_general(q_in, st.astype(BF16), _NT, preferred_element_type=F32)
    if small_decay:
        k_up = kh * jnp.exp(-bh)
        a = lax.dot_general(q_in, k_up.astype(BF16), _NT, preferred_element_type=F32)
        causal = (lax.broadcasted_iota(jnp.int32, (c, c), 0) >= lax.broadcasted_iota(jnp.int32, (c, c), 1))
        kd = (k_up * jnp.exp(b_last)).astype(BF16)
        st_ref[...] = st * jnp.exp(b_last) + lax.dot_general(vb, kd, _TN, preferred_element_type=F32)
        return o_inter + jnp.dot(jnp.where(causal, a, 0.0).astype(BF16), vb, preferred_element_type=F32)
    kd = (kh * jnp.exp(b_last - bh)).astype(BF16)
    st_ref[...] = st * jnp.exp(b_last) + lax.dot_general(vb, kd, _TN, preferred_element_type=F32)
    rows = lax.broadcasted_iota(jnp.int32, (sb, 1), 0)
    parts = []
    for s in range(c // sb):
        r0 = s * sb
        qs, ks, bs, vs = qh[r0:r0 + sb], kh[r0:r0 + sb], bh[r0:r0 + sb], vh[r0:r0 + sb]
        acc = o_inter[r0:r0 + sb]
        if s > 0:
            ref_b = bh[r0 - 1:r0, :]
            qf = (qs * jnp.exp(bs - ref_b)).astype(BF16)
            kf = (kh[0:r0] * jnp.exp(ref_b - bh[0:r0])).astype(BF16)
            a_off = lax.dot_general(qf, kf, _NT, preferred_element_type=F32)
            acc = acc + jnp.dot(a_off.astype(BF16), vb[0:r0], preferred_element_type=F32)
        for j in range(max(0, min(sb, live_rows - r0))):
            w = jnp.exp(jnp.minimum(bs - bs[j:j + 1], 0.0)) * qs * ks[j:j + 1]
            col = jnp.where(rows >= j, jnp.sum(w, axis=-1, keepdims=True), 0.0)
            acc = acc + col * vs[j:j + 1]
        parts.append(acc)
    return parts[0] if len(parts) == 1 else jnp.concatenate(parts, axis=0)


def _gla_kernel(*refs, mode, layer, dv, nb, rows_in, chunk, n_chunks, has_s0, in_widths):
    n_act = 4 if mode == "hgrn" else 5
    n_src = 3 if in_widths else n_act
    n_par = 2 if mode == "hgrn" else 3
    n_in = n_src + n_par + (1 if has_s0 else 0)
    n_out = 2 + (len(in_widths) - n_act if in_widths else 0)
    ins, (o_ref, sout_ref), scr = refs[:n_in], refs[n_in:n_in + 2], refs[n_in + n_out:]
    pars = ins[n_src:n_src + n_par]
    st_ref = scr[0]
    pad_refs = scr[1:]
    t_idx = pl.program_id(1)
    padded = rows_in < chunk
    assert (padded and not in_widths) or nb == 1

    @pl.when(t_idx == 0)
    def _():
        for i in range(nb):
            for h in range(N_HEADS):
                if has_s0:
                    st_ref[i * N_HEADS + h] = ins[-1][i, h].T
                else:
                    st_ref[i * N_HEADS + h] = jnp.zeros(st_ref.shape[1:], F32)

    if in_widths:
        x_ref, gm_ref, win_ref = ins[:n_src]
        hx = _rms(x_ref[0], gm_ref[...]).astype(BF16)
        offs = [sum(in_widths[:i]) for i in range(len(in_widths))]
        ld = [jnp.dot(hx, win_ref[:, o:o + w], preferred_element_type=F32) for o, w in zip(offs, in_widths)]
        for extra_ref, extra in zip(refs[n_in + 2:n_in + n_out], ld[n_act:]):
            extra_ref[0] = extra
        ld = ld[:n_act]
    elif padded:
        for p_ref, a_ref in zip(pad_refs, ins[:n_act]):
            p_ref[...] = jnp.zeros(p_ref.shape, F32)
            for i in range(nb):
                p_ref[i * chunk:i * chunk + rows_in, :] = a_ref[i]
        ld = [a[...] for a in pad_refs]
    else:
        ld = [a[0] for a in ins[:n_act]]
    groups = nb if padded else n_chunks
    span = groups * chunk

    if mode == "hgrn":
        q_raw, f, v, gate = ld
        lb_ref, gn_ref = pars
        lbv = lb_ref[...]
        e = jnp.exp(lbv - jnp.max(lbv, axis=0, keepdims=True))
        lb = jnp.sum(e[0:layer + 1], axis=0, keepdims=True) / jnp.sum(e, axis=0, keepdims=True)
        forget = lb + (1.0 - lb) * _sigmoid(f)
        k = 1.0 - forget
        lg = jnp.log(forget)
        q = _silu(q_raw)
    else:
        q_raw, k, v, gate, gd = ld
        wg_ref, bg_ref, gn_ref = pars
        z = jnp.dot(gd.astype(BF16), wg_ref[...], preferred_element_type=F32) + bg_ref[...]
        lg = (jnp.minimum(z, 0.0) - jnp.log(1.0 + jnp.exp(-jnp.abs(z)))) / GLA_GATE_TAU
        q = q_raw * (HEAD_DK ** -0.5)
    if padded:
        live = lax.broadcasted_iota(jnp.int32, (span, 1), 0) % chunk < rows_in
        lg = jnp.where(live, lg, 0.0)
        k = jnp.where(live, k, 0.0)
    b = _cumsum_rows(lg, chunk)

    def piece(x, i, h, width):
        return x[i * chunk:(i + 1) * chunk, h * width:(h + 1) * width]

    def heads(small_decay):
        rows = []
        for i in range(groups):
            cols = []
            for h in range(N_HEADS):
                cols.append(_gla_head(piece(q, i, h, HEAD_DK), piece(k, i, h, HEAD_DK),
                                      piece(b, i, h, HEAD_DK), piece(v, i, h, dv),
                                      st_ref.at[(i if padded else 0) * N_HEADS + h], small_decay,
                                      rows_in if padded else chunk))
            rows.append(jnp.concatenate(cols, axis=1))
        return rows[0] if groups == 1 else jnp.concatenate(rows, axis=0)

    if chunk <= SUB_BLOCK:
        o_raw = heads(False)
    else:
        o_raw = lax.cond(jnp.min(b) >= -GLA_SAFE_DECAY,
                         functools.partial(heads, True), functools.partial(heads, False))
    o_all = jnp.concatenate(
        [_rms(o_raw[:, h * dv:(h + 1) * dv], gn_ref[...]) * _silu(gate[:, h * dv:(h + 1) * dv])
         for h in range(N_HEADS)], axis=1).astype(o_ref.dtype)
    if padded:
        for i in range(nb):
            o_ref[i] = o_all[i * chunk:i * chunk + rows_in]
    else:
        o_ref[0] = o_all

    @pl.when(t_idx == pl.num_programs(1) - 1)
    def _():
        for i in range(nb):
            for h in range(N_HEADS):
                sout_ref[i, h] = st_ref[i * N_HEADS + h].T


def _gla_call(mode, src, params, s0, *, layer, dv, tb, chunk, nb, extra_widths=()):
    fused_in = isinstance(src, tuple)
    proj = src[0] if fused_in else src
    bsz, seq, _ = proj.shape
    rows_in = min(tb, seq)
    if rows_in < chunk:
        assert seq == rows_in and bsz % nb == 0
        nt, n_chunks = 1, 1
    else:
        assert seq % tb == 0 and tb % chunk == 0 and nb == 1
        nt, n_chunks = seq // tb, tb // chunk
    kw, vw = N_HEADS * HEAD_DK, N_HEADS * dv

    def act(width, col_block):
        return pl.BlockSpec((nb, rows_in, width), lambda b, t: (b, t, col_block))

    def whole(a):
        return pl.BlockSpec(a.shape, lambda b, t: (0,) * a.ndim)

    if mode == "hgrn":
        act_specs = [act(kw, 0), act(kw, 1), act(vw, 2), act(vw, 3)]
        act_widths = [kw, kw, vw, vw]
    else:
        act_specs = [act(kw, 0), act(kw, 1), act(vw, kw * 2 // vw), act(vw, kw * 2 // vw + 1),
                     act(128, (2 * kw + 2 * vw) // 128)]
        act_widths = [kw, kw, vw, vw, 128]
    if fused_in:
        x, g_mix, w_in = src[0], _whole(src[1]), _whole(src[2])
        in_specs = [pl.BlockSpec((nb, rows_in, x.shape[-1]), lambda b, t: (b, t, 0)), g_mix[1], w_in[1]]
        args = [x, g_mix[0], w_in[0]]
    else:
        in_specs, args = act_specs, [proj] * len(act_specs)
    in_specs = in_specs + [whole(p) for p in params]
    args = args + list(params)
    state_spec = pl.BlockSpec((nb, N_HEADS, HEAD_DK, dv), lambda b, t: (b, 0, 0, 0))
    if s0 is not None:
        in_specs.append(state_spec)
        args.append(s0)
    scratch = [pltpu.VMEM((nb * N_HEADS, dv, HEAD_DK), F32)]
    if rows_in < chunk:
        scratch += [pltpu.VMEM((nb * chunk, w), F32) for w in act_widths]
    return pl.pallas_call(
        functools.partial(_gla_kernel, mode=mode, layer=layer, dv=dv, nb=nb, rows_in=rows_in,
                          chunk=chunk, n_chunks=n_chunks, has_s0=s0 is not None,
                          in_widths=tuple(act_widths) + tuple(extra_widths) if fused_in else None),
        grid=(bsz // nb, nt),
        in_specs=in_specs,
        out_specs=[pl.BlockSpec((nb, rows_in, vw), lambda b, t: (b, t, 0)), state_spec]
        + [pl.BlockSpec((nb, rows_in, w), lambda b, t: (b, t, 0)) for w in extra_widths],
        out_shape=[jax.ShapeDtypeStruct((bsz, seq, vw), BF16),
                   jax.ShapeDtypeStruct((bsz, N_HEADS, HEAD_DK, dv), F32)]
        + [jax.ShapeDtypeStruct((bsz, seq, w), F32) for w in extra_widths],
        scratch_shapes=scratch,
        compiler_params=_cparams("arbitrary", "arbitrary"),
        name="gla_" + mode,
    )(*args)


def _s5_prep_kernel(lre_ref, lim_ref, ls_ref, lre_x_ref, lim_x_ref, ls_x_ref, bre_ref, bim_ref,
                    are_ref, aim_ref, bbre_ref, bbim_ref):
    def disc(lre, lim, ls):
        lr = jnp.minimum(lre, S5_MAX_RE)
        dt = jnp.exp(ls)
        mag = jnp.exp(lr * dt)
        a_re = mag * jnp.cos(lim * dt)
        a_im = mag * jnp.sin(lim * dt)
        den = lr * lr + lim * lim
        z_re = ((a_re - 1.0) * lr + a_im * lim) / den
        z_im = (a_im * lr - (a_re - 1.0) * lim) / den
        return a_re, a_im, z_re, z_im

    a_re, a_im, _, _ = disc(lre_ref[...], lim_ref[...], ls_ref[...])
    are_ref[...] = a_re
    aim_ref[...] = a_im
    _, _, z_re, z_im = disc(lre_x_ref[...], lim_x_ref[...], ls_x_ref[...])
    bbre_ref[...] = z_re * bre_ref[...] - z_im * bim_ref[...]
    bbim_ref[...] = z_re * bim_ref[...] + z_im * bre_ref[...]


def _s5_prep(lam_re, lam_im, log_step, b_re, b_im):
    g, n = lam_re.shape
    p = b_re.shape[-1]
    ls = jnp.broadcast_to(log_step[:, None], (g, n))
    rep = lambda a: jnp.repeat(a, p, axis=1)
    outs = pl.pallas_call(
        _s5_prep_kernel,
        out_shape=[jax.ShapeDtypeStruct((g, n), F32)] * 2 + [jax.ShapeDtypeStruct((g, n * p), F32)] * 2,
        name="s5_prep",
    )(lam_re, lam_im, ls, rep(lam_re), rep(lam_im), rep(ls),
      b_re.reshape(g, n * p), b_im.reshape(g, n * p))
    a_re, a_im, bb_re, bb_im = outs
    return a_re, a_im, bb_re.reshape(g, n, p), bb_im.reshape(g, n, p)


def _s5_kernel(*refs, bg, tc, nt, cw, has_x0):
    n_in = 9 if has_x0 else 7
    u_ref, bm_ref, cm_ref, a_ref, d_ref, wg_ref, bgl_ref = refs[:7]
    o_ref, sre_ref, sim_ref = refs[n_in:n_in + 3]
    utm_ref, xs_ref, st_ref = refs[n_in + 3:]
    t_idx = pl.program_id(1)
    half = xs_ref.shape[1] // 2
    uw = u_ref.shape[-1] // 2

    @pl.when(t_idx == 0)
    def _():
        for hf in range(2):
            if has_x0:
                st_ref[:, hf * 2 * half:hf * 2 * half + half] = refs[7][:, hf * half:(hf + 1) * half]
                st_ref[:, hf * 2 * half + half:(hf + 1) * 2 * half] = refs[8][:, hf * half:(hf + 1) * half]
            else:
                st_ref[...] = jnp.zeros(st_ref.shape, F32)

    for t in range(tc):
        utm_ref[t * bg:(t + 1) * bg, :] = u_ref[:, t, :]
    u = utm_ref[...]
    ys = []
    for hf in range(2):
        xs_ref[...] = jnp.dot(u[:, hf * uw:(hf + 1) * uw].astype(BF16), bm_ref[hf],
                              preferred_element_type=F32)
        base = hf * 2 * half
        for c0 in range(0, half, cw):
            ar = a_ref[0:1, base + c0:base + c0 + cw]
            ai = a_ref[0:1, base + half + c0:base + half + c0 + cw]
            xr = st_ref[:, base + c0:base + c0 + cw]
            xi = st_ref[:, base + half + c0:base + half + c0 + cw]
            for t in range(tc):
                rows = slice(t * bg, (t + 1) * bg)
                nr = ar * xr - ai * xi + xs_ref[rows, c0:c0 + cw]
                ni = ar * xi + ai * xr + xs_ref[rows, half + c0:half + c0 + cw]
                xs_ref[rows, c0:c0 + cw] = nr
                xs_ref[rows, half + c0:half + c0 + cw] = ni
                xr, xi = nr, ni
            st_ref[:, base + c0:base + c0 + cw] = xr
            st_ref[:, base + half + c0:base + half + c0 + cw] = xi
        ys.append(jnp.dot(xs_ref[...].astype(BF16), cm_ref[hf], preferred_element_type=F32))
    y = jnp.concatenate(ys, axis=1) + d_ref[...] * u
    y = 0.5 * y * (1.0 + jnp.tanh(math.sqrt(2.0 / math.pi) * (y + 0.044715 * (y * y * y))))
    gate = jnp.dot(y.astype(BF16), wg_ref[...], preferred_element_type=F32) + bgl_ref[...]
    utm_ref[...] = y * _sigmoid(gate)
    for t in range(tc):
        o_ref[:, t, :] = utm_ref[t * bg:(t + 1) * bg, :]

    @pl.when(t_idx == nt - 1)
    def _():
        for hf in range(2):
            sre_ref[:, hf * half:(hf + 1) * half] = st_ref[:, hf * 2 * half:hf * 2 * half + half]
            sim_ref[:, hf * half:(hf + 1) * half] = st_ref[:, hf * 2 * half + half:(hf + 1) * 2 * half]


def _s5_call(u_src, col_block, mats, x0, *, tc, cw):
    bg, seq, _ = u_src.shape
    assert seq % tc == 0
    nt = seq // tc
    bmat, cmat, a_flat, d_row, w_glu, b_glu = mats
    width = d_row.shape[1]
    nstate = a_flat.shape[1] // 2
    whole = lambda a: pl.BlockSpec(a.shape, lambda g, t: (0,) * a.ndim)
    in_specs = [pl.BlockSpec((bg, tc, width), lambda g, t: (0, t, col_block))]
    in_specs += [whole(m) for m in mats]
    args = [u_src] + list(mats)
    st_spec = pl.BlockSpec((bg, nstate), lambda g, t: (0, 0))
    if x0 is not None:
        in_specs += [st_spec, st_spec]
        args += list(x0)
    scratch = [pltpu.VMEM((bg * tc, width), F32), pltpu.VMEM((bg * tc, nstate), F32),
               pltpu.VMEM((bg, 2 * nstate), F32)]
    return pl.pallas_call(
        functools.partial(_s5_kernel, bg=bg, tc=tc, nt=nt, cw=cw, has_x0=x0 is not None),
        grid=(1, nt),
        in_specs=in_specs,
        out_specs=[pl.BlockSpec((bg, tc, width), lambda g, t: (0, t, 0)), st_spec, st_spec],
        out_shape=[jax.ShapeDtypeStruct((bg, seq, width), F32),
                   jax.ShapeDtypeStruct((bg, nstate), F32),
                   jax.ShapeDtypeStruct((bg, nstate), F32)],
        scratch_shapes=scratch,
        compiler_params=_cparams("arbitrary", "arbitrary"),
        name="s5",
    )(*args)


def _s5_matrices(a_re, a_im, bb_re, bb_im, c_re, c_im, d, w_glu, b_glu):
    g, n, p = bb_re.shape
    gh = g // 2
    eye = jnp.eye(gh, dtype=F32)

    def block_diag(t):
        return (eye[:, None, :, None] * t[:, :, None, :]).reshape(gh * t.shape[1], gh * t.shape[2])

    def in_mat(bb):
        return block_diag(bb.transpose(0, 2, 1))

    def out_mat(cc):
        return block_diag(cc.transpose(0, 2, 1))

    bmat = jnp.stack([jnp.concatenate([in_mat(bb_re[h * gh:(h + 1) * gh]),
                                       in_mat(bb_im[h * gh:(h + 1) * gh])], axis=1)
                      for h in range(2)]).astype(BF16)
    cmat = jnp.stack([jnp.concatenate([out_mat(c_re[h * gh:(h + 1) * gh]),
                                       out_mat(-c_im[h * gh:(h + 1) * gh])], axis=0)
                      for h in range(2)]).astype(BF16)
    a_flat = jnp.concatenate([jnp.concatenate([a_re[h * gh:(h + 1) * gh].reshape(1, gh * n),
                                               a_im[h * gh:(h + 1) * gh].reshape(1, gh * n)], axis=1)
                              for h in range(2)], axis=1)
    return (bmat, cmat, a_flat, d.reshape(1, g * p), w_glu.astype(BF16), b_glu.reshape(1, -1))


def _xattn_kernel(*refs, n_pre):
    x_ref = refs[0]
    g_ref, wq_ref, wo_ref, k_ref, v_ref, y_ref, kt_ref, vt_ref = refs[1 + 2 * n_pre:]

    @pl.when(pl.program_id(1) == 0)
    def _():
        kt_ref[...] = jnp.transpose(k_ref[...], (1, 0, 2)).astype(BF16)
        vt_ref[...] = jnp.transpose(v_ref[...], (1, 0, 2)).astype(BF16)

    x = x_ref[...]
    for a_ref, w_ref in zip(refs[1:1 + n_pre], refs[1 + n_pre:1 + 2 * n_pre]):
        x = x + jnp.dot(a_ref[...].astype(BF16), w_ref[...], preferred_element_type=F32)
    q = jnp.dot(_rms(x, g_ref[...]).astype(BF16), wq_ref[...], preferred_element_type=F32).astype(BF16)
    hd = q.shape[1] // N_HEADS
    outs = []
    for h in range(N_HEADS):
        s = lax.dot_general(q[:, h * hd:(h + 1) * hd], kt_ref[h], _NT,
                            preferred_element_type=F32) * (hd ** -0.5)
        p = jnp.exp(s - jnp.max(s, axis=-1, keepdims=True))
        p = p / jnp.sum(p, axis=-1, keepdims=True)
        outs.append(jnp.dot(p.astype(BF16), vt_ref[h], preferred_element_type=F32).astype(BF16))
    y_ref[...] = x + jnp.dot(jnp.concatenate(outs, axis=1), wo_ref[...], preferred_element_type=F32)


def _xattn_call(x, pre_a, pre_w, g, w_q, w_o, mem_k, mem_v, layer, *, tq):
    bsz, seq, d = x.shape
    n_mem, nh, hd = mem_k.shape[2:]
    assert seq % tq == 0
    kv_spec = pl.BlockSpec((None, None, n_mem, nh, hd), lambda b, t: (layer, b, 0, 0, 0))
    rows = lambda a: pl.BlockSpec((None, tq, a.shape[-1]), lambda b, t: (b, t, 0))
    params = [_whole(w) for w in (*pre_w, g, w_q, w_o)]
    return pl.pallas_call(
        functools.partial(_xattn_kernel, n_pre=len(pre_a)),
        grid=(bsz, seq // tq),
        in_specs=[rows(x)] + [rows(a) for a in pre_a] + [spec for _, spec in params] + [kv_spec, kv_spec],
        out_specs=rows(x),
        out_shape=jax.ShapeDtypeStruct((bsz, seq, d), F32),
        scratch_shapes=[pltpu.VMEM((nh, n_mem, hd), BF16), pltpu.VMEM((nh, n_mem, hd), BF16)],
        compiler_params=_cparams("arbitrary", "arbitrary"),
        name="xattn",
    )(x, *pre_a, *[w for w, _ in params], mem_k, mem_v)


def _attn_rows_kernel(q_ref, k_ref, v_ref, o_ref, pad_ref, *, rows_in):
    nb, rows, d = pad_ref.shape
    nblk = d // 128
    half_blk = nblk // 2
    lanes = k_ref.shape[1]
    lane_blk = lax.broadcasted_iota(jnp.int32, (1, lanes), 1) % nblk
    row_head = lax.broadcasted_iota(jnp.int32, (N_HEADS * rows, 1), 0) // rows
    live = lane_blk == row_head
    scale = (d // N_HEADS) ** -0.5
    for i in range(nb):
        pad_ref[i] = jnp.zeros((rows, d), F32)
        pad_ref[i, 0:rows_in, :] = q_ref[i]
        q = pad_ref[i]
        qx = jnp.concatenate([q[:, j * 128:(j + 1) * 128] for j in range(nblk)], axis=0)
        g = lax.dot_general(qx.astype(BF16), k_ref[i].astype(BF16), _NT, preferred_element_type=F32)
        s = jnp.concatenate(
            [g[2 * h * rows:(2 * h + 1) * rows]
             + pltpu.roll(g[(2 * h + 1) * rows:(2 * h + 2) * rows], lanes - half_blk, axis=1)
             for h in range(N_HEADS)], axis=0) * scale
        s = jnp.where(live, s, -1e30)
        e = jnp.exp(s - jnp.max(s, axis=-1, keepdims=True))
        p = e / jnp.sum(e, axis=-1, keepdims=True)
        px = jnp.concatenate(
            [blk for h in range(N_HEADS)
             for blk in (p[h * rows:(h + 1) * rows], pltpu.roll(p[h * rows:(h + 1) * rows], half_blk, axis=1))],
            axis=0)
        o = jnp.dot(px.astype(BF16), v_ref[i].astype(BF16), preferred_element_type=F32)
        o = jnp.concatenate([o[j * rows:(j + 1) * rows] for j in range(nblk)], axis=1)
        o_ref[i] = o[0:rows_in].astype(o_ref.dtype)


def _attn_rows_call(q, mem_k, mem_v, layer, *, nb):
    bsz, seq, d = q.shape
    depth, _, n_mem, nh, hd = mem_k.shape
    assert bsz % nb == 0 and hd == 256 and nh == N_HEADS and seq <= 16
    as_rows = lambda a: a.reshape(depth, bsz, n_mem, nh, 2, 128).transpose(0, 1, 2, 4, 3, 5).reshape(
        depth, bsz, n_mem * 2 * nh, 128)
    kv_spec = pl.BlockSpec((None, nb, n_mem * 2 * nh, 128), lambda b: (layer, b, 0, 0))
    return pl.pallas_call(
        functools.partial(_attn_rows_kernel, rows_in=seq),
        grid=(bsz // nb,),
        in_specs=[pl.BlockSpec((nb, seq, d), lambda b: (b, 0, 0)), kv_spec, kv_spec],
        out_specs=pl.BlockSpec((nb, seq, d), lambda b: (b, 0, 0)),
        out_shape=jax.ShapeDtypeStruct((bsz, seq, d), BF16),
        scratch_shapes=[pltpu.VMEM((nb, 8 * pl.cdiv(seq, 8), d), F32)],
        compiler_params=_cparams("arbitrary"),
        name="mem_attn_rows",
    )(q, as_rows(mem_k), as_rows(mem_v))


FFN_COLS = 256


def _ffn_kernel(*refs, tm, ts, hs, f_dim, has_hist, has_final):
    n_in = 6 + int(has_hist) + int(has_final)
    x_ref, g_ref, wup_ref, cw_ref, cb_ref, wdn_ref = refs[:6]
    y_ref, state_ref = refs[n_in:n_in + 2]
    gated_ref, hist_ref = refs[n_in + 2:n_in + 4]
    t_idx = pl.program_id(1)

    @pl.when(t_idx == 0)
    def _():
        hist_ref[...] = jnp.zeros(hist_ref.shape, F32)
        if has_hist:
            hist_ref[hs - 2 * ts:hs, :] = refs[6][...]

    x = x_ref[...]
    h = _rms(x, g_ref[...]).astype(BF16)
    row = lax.broadcasted_iota(jnp.int32, (tm, 1), 0)
    for c in range(f_dim // FFN_COLS):
        conv = []
        for part in range(2):
            cols = slice(part * f_dim + c * FFN_COLS, part * f_dim + (c + 1) * FFN_COLS)
            u = jnp.dot(h, wup_ref[:, cols], preferred_element_type=F32)
            if ts == 1:
                prev2, prev1 = hist_ref[hs - 2:hs - 1, cols], hist_ref[hs - 1:hs, cols]
                m1 = jnp.where(row == 0, prev1, pltpu.roll(u, 1, axis=0))
                m2 = jnp.where(row == 0, prev2, jnp.where(row == 1, prev1, pltpu.roll(u, 2, axis=0)))
            else:
                ext = jnp.concatenate([hist_ref[hs - 2 * ts:hs, cols], u], axis=0)
                m2, m1 = ext[0:tm], ext[ts:ts + tm]
            conv.append(cb_ref[:, cols] + cw_ref[0:1, cols] * m2 + cw_ref[1:2, cols] * m1
                        + cw_ref[2:3, cols] * u)
            hist_ref[:, cols] = u[tm - hs:tm]
        gated_ref[:, c * FFN_COLS:(c + 1) * FFN_COLS] = (_silu(conv[0]) * conv[1]).astype(BF16)
    out = x + jnp.dot(gated_ref[...], wdn_ref[...], preferred_element_type=F32)
    if has_final:
        out = _rms(out, refs[n_in - 1][...])
    y_ref[...] = out

    @pl.when(t_idx == pl.num_programs(1) - 1)
    def _():
        state_ref[...] = hist_ref[hs - 2 * ts:hs, :]


def _ffn_call(x, g, w_up, conv_w, conv_b, w_down, hist0, g_final, *, tm, ts):
    ngrp, rows, d = x.shape
    params = [_whole(a) for a in (g, w_up, conv_w, conv_b, w_down)]
    f2 = params[1][0].shape[-1]
    f_dim = f2 // 2
    hs = max(8, 2 * ts)
    assert rows % tm == 0 and tm >= hs and f_dim % FFN_COLS == 0 and (ts == 1 or ts % 8 == 0)
    args = [x] + [a for a, _ in params]
    in_specs = [pl.BlockSpec((None, tm, d), lambda s, t: (s, t, 0))] + [spec for _, spec in params]
    st_spec = pl.BlockSpec((None, 2 * ts, f2), lambda s, t: (s, 0, 0))
    if hist0 is not None:
        in_specs.append(st_spec)
        args.append(hist0)
    if g_final is not None:
        in_specs.append(_whole(g_final)[1])
        args.append(g_final)
    return pl.pallas_call(
        functools.partial(_ffn_kernel, tm=tm, ts=ts, hs=hs, f_dim=f_dim,
                          has_hist=hist0 is not None, has_final=g_final is not None),
        grid=(ngrp, rows // tm),
        in_specs=in_specs,
        out_specs=[pl.BlockSpec((None, tm, d), lambda s, t: (s, t, 0)), st_spec],
        out_shape=[jax.ShapeDtypeStruct((ngrp, rows, d), F32),
                   jax.ShapeDtypeStruct((ngrp, 2 * ts, f2), F32)],
        scratch_shapes=[pltpu.VMEM((tm, f_dim), BF16), pltpu.VMEM((hs, f2), F32)],
        compiler_params=_cparams("arbitrary", "arbitrary"),
        name="conv_ffn",
    )(*args)


def _trunk(x, mem_k, mem_v, states, p, *, prompt):
    bsz, seq, d = x.shape
    depth = p["norm_mix"].shape[0]
    m = bsz * seq
    x2 = x.reshape(m, d)
    new = {"hgrn": [], "s5_re": [], "s5_im": [], "gla": [], "conv": []}
    if prompt:
        gla_tiles = dict(tb=1024, chunk=64, nb=1)
        attn_tq = 1024
    else:
        gla_tiles = dict(tb=seq, chunk=16, nb=8)
        attn_tq = seq
    for l in range(depth):
        g_mix = _pick(p["norm_mix"], l)
        if l % 2 == 0:
            e = l // 2
            kw = N_HEADS * HEAD_DK
            x3 = x2.reshape(bsz, seq, d)
            hgrn_par = [p["hgrn_lb"], p["hgrn_gnorm"][e].reshape(1, -1)]
            mats = p["s5_mats"][e]
            if prompt:
                o_a, s_a, u = _gla_call("hgrn", (x3, g_mix, _pick(p["w_in_ab"], e)), hgrn_par, None,
                                        layer=l, dv=kw // N_HEADS, extra_widths=(kw,), **gla_tiles)
                o_b, sr, si = _s5_call(u, 0, mats, None, tc=128, cw=512)
            else:
                proj = _norm_proj(x2, g_mix, _pick(p["w_in_ab"], e)).reshape(bsz, seq, -1)
                o_a, s_a = _gla_call("hgrn", proj, hgrn_par, states["hgrn"][e],
                                     layer=l, dv=kw // N_HEADS, **gla_tiles)
                x0 = (states["s5_re"][e].reshape(bsz, -1), states["s5_im"][e].reshape(bsz, -1))
                o_b, sr, si = _s5_call(proj, 4, mats, x0, tc=seq, cw=128)
            mixed = [o_a, o_b]
            mixed_w = [_pick(p["w_out_ab"], e, rows=kw, row_block=0), _pick(p["w_out_ab"], e, rows=kw, row_block=1)]
            new["hgrn"].append(s_a)
            new["s5_re"].append(sr.reshape(bsz, -1, S5_STATE))
            new["s5_im"].append(si.reshape(bsz, -1, S5_STATE))
        else:
            o_idx = l // 2
            if prompt:
                src_c = (x2.reshape(bsz, seq, d), g_mix, _pick(p["w_in_c"], o_idx))
            else:
                src_c = _norm_proj(x2, g_mix, _pick(p["w_in_c"], o_idx)).reshape(bsz, seq, -1)
            o_c, s_c = _gla_call(
                "gla", src_c,
                [p["gla_w_gate"][o_idx], p["gla_b_gate"][o_idx].reshape(1, -1),
                 p["gla_gnorm"][o_idx].reshape(1, -1)],
                None if states is None else states["gla"][o_idx],
                layer=l, dv=d // N_HEADS, **{**gla_tiles, **(dict(chunk=256) if prompt else {})})
            mixed, mixed_w = [o_c], [_pick(p["w_out_c"], o_idx)]
            new["gla"].append(s_c)
        g_cross, w_q, w_o = _pick(p["norm_cross"], l), _pick(p["xa_w_q"], l), _pick(p["xa_w_o"], l)
        if prompt:
            x2 = _xattn_call(x2.reshape(bsz, seq, d), mixed, mixed_w, g_cross, w_q, w_o,
                             mem_k, mem_v, l, tq=attn_tq).reshape(m, d)
        else:
            x2 = _proj_res(x2, [a.reshape(m, -1) for a in mixed], mixed_w)
            q = _norm_proj(x2, g_cross, w_q)
            o_x = _attn_rows_call(q.reshape(bsz, seq, d), mem_k, mem_v, l, nb=8)
            x2 = _proj_res(x2, [o_x.reshape(m, d)], [w_o])
        g_final = p["norm_final"] if l == depth - 1 else None
        ffn_w = tuple(_pick(p[name], l) for name in
                      ("norm_ffn", "ffn_w_up", "ffn_conv_w", "ffn_conv_b", "ffn_w_down"))
        if prompt:
            y, cst = _ffn_call(x2.reshape(bsz, seq, d), *ffn_w, None, g_final, tm=1024, ts=1)
            x2 = y.reshape(m, d)
        else:
            xt = x2.reshape(bsz, seq, d).transpose(1, 0, 2).reshape(1, m, d)
            hist0 = states["conv"][l].transpose(1, 0, 2).reshape(1, 2 * bsz, -1)
            y, cst = _ffn_call(xt, *ffn_w, hist0, g_final, tm=m, ts=bsz)
            x2 = y.reshape(seq, bsz, d).transpose(1, 0, 2).reshape(m, d)
            cst = cst.reshape(2, bsz, -1).transpose(1, 0, 2)
        new["conv"].append(cst)
    return x2.reshape(bsz, seq, d), new


def kernel(x_prompt, x_sample, mem_prompt, cache_mem_k, cache_mem_v, state_hgrn, state_s5_re, state_s5_im, state_gla, state_ffn_conv, norm_mix, norm_cross, norm_mem, norm_ffn, norm_final, w_in_ab, hgrn_lb, hgrn_gnorm, s5_lam_re, s5_lam_im, s5_log_step, s5_b_re, s5_b_im, s5_c_re, s5_c_im, s5_d, s5_w_glu, s5_b_glu, w_out_ab, w_in_c, gla_w_gate_up, gla_b_gate, gla_gnorm, w_out_c, xa_w_q, xa_w_kv, xa_w_o, ffn_w_up, ffn_conv_w, ffn_conv_b, ffn_w_down):
    depth, d = norm_mix.shape

    gla_cols = w_in_c.shape[2]
    gate_rank = gla_w_gate_up.shape[1]
    pad_c = (-gla_cols) % 128
    w_in_c_p = jnp.pad(w_in_c, ((0, 0), (0, 0), (0, pad_c))).astype(BF16)
    gla_w_gate = jnp.pad(gla_w_gate_up, ((0, 0), (0, 128 - gate_rank), (0, 0))).astype(BF16)

    s5_mats = []
    for e in range(s5_lam_re.shape[0]):
        a_re, a_im, bb_re, bb_im = _s5_prep(s5_lam_re[e], s5_lam_im[e], s5_log_step[e],
                                            s5_b_re[e], s5_b_im[e])
        s5_mats.append(_s5_matrices(a_re, a_im, bb_re, bb_im, s5_c_re[e], s5_c_im[e], s5_d[e],
                                    s5_w_glu[e], s5_b_glu[e]))

    row = lambda a: a.reshape(a.shape[0], 1, a.shape[1])
    p = dict(norm_mix=row(norm_mix), norm_cross=row(norm_cross), norm_ffn=row(norm_ffn),
             norm_final=norm_final.reshape(1, d),
             w_in_ab=w_in_ab.astype(BF16), hgrn_lb=hgrn_lb, hgrn_gnorm=hgrn_gnorm, s5_mats=s5_mats,
             w_out_ab=w_out_ab.astype(BF16), w_in_c=w_in_c_p, gla_w_gate=gla_w_gate,
             gla_b_gate=gla_b_gate, gla_gnorm=gla_gnorm, w_out_c=w_out_c.astype(BF16),
             xa_w_q=xa_w_q.astype(BF16), xa_w_o=xa_w_o.astype(BF16),
             ffn_w_up=ffn_w_up.astype(BF16), ffn_conv_w=ffn_conv_w, ffn_conv_b=row(ffn_conv_b),
             ffn_w_down=ffn_w_down.astype(BF16))

    mem_k_p, mem_v_p = _mem_kv(mem_prompt, norm_mem.reshape(depth, 1, d), xa_w_kv.astype(BF16))
    y_prompt, st_p = _trunk(x_prompt, mem_k_p, mem_v_p, None, p, prompt=True)

    states = dict(hgrn=state_hgrn, s5_re=state_s5_re, s5_im=state_s5_im, gla=state_gla,
                  conv=state_ffn_conv)
    y_sample, st_s = _trunk(x_sample, cache_mem_k, cache_mem_v, states, p, prompt=False)

    stack = lambda xs: xs[0][None] if len(xs) == 1 else jnp.stack(xs)
    return (y_prompt, y_sample,
            stack(st_p["hgrn"]), stack(st_p["s5_re"]), stack(st_p["s5_im"]), stack(st_p["gla"]),
            mem_k_p, mem_v_p, stack(st_p["conv"]),
            stack(st_s["hgrn"]), stack(st_s["s5_re"]), stack(st_s["s5_im"]), stack(st_s["gla"]),
            stack(st_s["conv"]))
```

```python
import functools
import math

import jax
import jax.numpy as jnp
from jax import lax
from jax.experimental import pallas as pl
from jax.experimental.pallas import tpu as pltpu

F32 = jnp.float32
BF16 = jnp.bfloat16

EPS = 1e-6
S5_MAX_RE = -1e-4
GLA_GATE_TAU = 16.0
N_HEADS = 4
HEAD_DK = 128
S5_GROUP = 16
S5_STATE = 64
SUB_BLOCK = 16
GLA_SAFE_DECAY = 64.0
VMEM_LIMIT = 56 * 1024 * 1024

_NT = (((1,), (1,)), ((), ()))
_TN = (((0,), (0,)), ((), ()))


def _cparams(*sem):
    return pltpu.CompilerParams(dimension_semantics=sem, vmem_limit_bytes=VMEM_LIMIT)


def _rms(x, g):
    return x * lax.rsqrt(jnp.mean(x * x, axis=-1, keepdims=True) + EPS) * g


def _sigmoid(x):
    return 1.0 / (1.0 + jnp.exp(-x))


def _silu(x):
    return x * _sigmoid(x)


def _row_tile(rows, want):
    t = min(rows, want)
    assert rows % t == 0, (rows, t)
    return t


def _pick(a, layer, rows=None, row_block=0, cols=None, col_block=0):
    block = (None, rows or a.shape[1], cols or a.shape[2])
    return a, pl.BlockSpec(block, lambda *_: (layer, row_block, col_block))


def _whole(a):
    return a if isinstance(a, tuple) else (a, pl.BlockSpec(a.shape, lambda *_: (0,) * a.ndim))


def _norm_proj_kernel(x_ref, g_ref, w_ref, o_ref):
    h = _rms(x_ref[...], g_ref[...]).astype(BF16)
    n = o_ref.shape[1]
    for c0 in range(0, n, 512):
        cw = min(512, n - c0)
        o_ref[:, c0:c0 + cw] = jnp.dot(h, w_ref[:, c0:c0 + cw], preferred_element_type=F32)


def _norm_proj(x, g, w, *, tm=512):
    m, k = x.shape
    (g, g_spec), (w, w_spec) = _whole(g), _whole(w)
    n = w.shape[-1]
    tm = _row_tile(m, tm)
    return pl.pallas_call(
        _norm_proj_kernel,
        grid=(m // tm,),
        in_specs=[pl.BlockSpec((tm, k), lambda i: (i, 0)), g_spec, w_spec],
        out_specs=pl.BlockSpec((tm, n), lambda i: (i, 0)),
        out_shape=jax.ShapeDtypeStruct((m, n), F32),
        compiler_params=_cparams("arbitrary"),
        name="norm_proj",
    )(x, g, w)


def _mem_kv_kernel(x_ref, g_ref, w_ref, k_ref, v_ref):
    nb, n_mem, d = x_ref.shape
    h = _rms(x_ref[...].reshape(nb * n_mem, d), g_ref[...]).astype(BF16)
    hd = d // N_HEADS
    for o_ref, base in ((k_ref, 0), (v_ref, d)):
        for hh in range(N_HEADS):
            kv = jnp.dot(h, w_ref[:, base + hh * hd:base + (hh + 1) * hd], preferred_element_type=F32)
            for i in range(nb):
                o_ref[i, :, hh, :] = kv[i * n_mem:(i + 1) * n_mem]


def _mem_kv(mem, g, w, *, nb=2):
    bsz, n_mem, d = mem.shape
    depth = w.shape[0]
    assert bsz % nb == 0
    out_spec = pl.BlockSpec((None, nb, n_mem, N_HEADS, d // N_HEADS), lambda l, b: (l, b, 0, 0, 0))
    out_shape = jax.ShapeDtypeStruct((depth, bsz, n_mem, N_HEADS, d // N_HEADS), F32)
    return pl.pallas_call(
        _mem_kv_kernel,
        grid=(depth, bsz // nb),
        in_specs=[pl.BlockSpec((nb, n_mem, d), lambda l, b: (b, 0, 0)),
                  pl.BlockSpec((None, 1, d), lambda l, b: (l, 0, 0)),
                  pl.BlockSpec((None, d, 2 * d), lambda l, b: (l, 0, 0))],
        out_specs=[out_spec, out_spec],
        out_shape=[out_shape, out_shape],
        compiler_params=_cparams("arbitrary", "arbitrary"),
        name="mem_kv",
    )(mem, g, w)


def _proj_res_kernel(res_ref, *refs, n_in):
    acc = res_ref[...]
    for a_ref, w_ref in zip(refs[:n_in], refs[n_in:2 * n_in]):
        acc = acc + jnp.dot(a_ref[...].astype(BF16), w_ref[...], preferred_element_type=F32)
    refs[2 * n_in][...] = acc


def _proj_res(res, a_list, w_list, *, tm=512):
    m, n = res.shape
    tm = _row_tile(m, tm)
    n_in = len(a_list)
    w_list = [_whole(w) for w in w_list]
    in_specs = [pl.BlockSpec((tm, n), lambda i: (i, 0))]
    in_specs += [pl.BlockSpec((tm, a.shape[1]), lambda i: (i, 0)) for a in a_list]
    in_specs += [spec for _, spec in w_list]
    return pl.pallas_call(
        functools.partial(_proj_res_kernel, n_in=n_in),
        grid=(m // tm,),
        in_specs=in_specs,
        out_specs=pl.BlockSpec((tm, n), lambda i: (i, 0)),
        out_shape=jax.ShapeDtypeStruct((m, n), F32),
        compiler_params=_cparams("arbitrary"),
        name="proj_res",
    )(res, *a_list, *[w for w, _ in w_list])


def _cumsum_rows(x, c):
    hi = x.astype(BF16)
    rest = x - hi.astype(F32)
    mid = rest.astype(BF16)
    lo = (rest - mid.astype(F32)).astype(BF16)
    n = max(c, min(x.shape[0], 128))
    r = lax.broadcasted_iota(jnp.int32, (n, n), 0)
    col = lax.broadcasted_iota(jnp.int32, (n, n), 1)
    tri = jnp.where((r >= col) & (r // c == col // c), 1.0, 0.0).astype(BF16)
    tri3 = jnp.concatenate([tri, tri, tri], axis=1)
    out = [jnp.dot(tri3, jnp.concatenate([hi[s:s + n], mid[s:s + n], lo[s:s + n]], axis=0),
                   preferred_element_type=F32) for s in range(0, x.shape[0], n)]
    return out[0] if len(out) == 1 else jnp.concatenate(out, axis=0)


def _gla_head(qh, kh, bh, vh, st_ref, small_decay, live_rows):
    c, dk = qh.shape
    dv = vh.shape[1]
    sb = min(SUB_BLOCK, c)
    vb = vh.astype(BF16)
    st = st_ref[...]
    b_last = bh[c - 1:c, :]
    q_in = (qh * jnp.exp(bh)).astype(BF16)
    o_inter = jnp.dot(q_in, st.astype(BF16), preferred_element_type=F32)
    decay = jnp.transpose(jnp.broadcast_to(jnp.exp(b_last), (dk, dk)))
    decay = decay if dv == dk else jnp.concatenate([decay] * (dv // dk), axis=1)
    if small_decay:
        k_up = kh * jnp.exp(-bh)
        a = lax.dot_general(q_in, k_up.astype(BF16), _NT, preferred_element_type=F32)
        causal = (lax.broadcasted_iota(jnp.int32, (c, c), 0) >= lax.broadcasted_iota(jnp.int32, (c, c), 1))
        kd = (k_up * jnp.exp(b_last)).astype(BF16)
        st_ref[...] = st * decay + lax.dot_general(kd, vb, _TN, preferred_element_type=F32)
        return o_inter + jnp.dot(jnp.where(causal, a, 0.0).astype(BF16), vb, preferred_element_type=F32)
    kd = (kh * jnp.exp(b_last - bh)).astype(BF16)
    st_ref[...] = st * decay + lax.dot_general(kd, vb, _TN, preferred_element_type=F32)
    rows = lax.broadcasted_iota(jnp.int32, (sb, 1), 0)
    parts = []
    for s in range(c // sb):
        r0 = s * sb
        qs, ks, bs, vs = qh[r0:r0 + sb], kh[r0:r0 + sb], bh[r0:r0 + sb], vh[r0:r0 + sb]
        acc = o_inter[r0:r0 + sb]
        if s > 0:
            ref_b = bh[r0 - 1:r0, :]
            qf = (qs * jnp.exp(bs - ref_b)).astype(BF16)
            kf = (kh[0:r0] * jnp.exp(ref_b - bh[0:r0])).astype(BF16)
            a_off = lax.dot_general(qf, kf, _NT, preferred_element_type=F32)
            acc = acc + jnp.dot(a_off.astype(BF16), vb[0:r0], preferred_element_type=F32)
        for j in range(max(0, min(sb, live_rows - r0))):
            w = jnp.exp(jnp.minimum(bs - bs[j:j + 1], 0.0)) * qs * ks[j:j + 1]
            col = jnp.where(rows >= j, jnp.sum(w, axis=-1, keepdims=True), 0.0)
            acc = acc + col * vs[j:j + 1]
        parts.append(acc)
    return parts[0] if len(parts) == 1 else jnp.concatenate(parts, axis=0)


def _gla_kernel(*refs, mode, layer, dv, nb, rows_in, chunk, n_chunks, has_s0, in_widths):
    n_act = 4 if mode == "hgrn" else 5
    n_src = 3 if in_widths else n_act
    n_par = 2 if mode == "hgrn" else 3
    n_in = n_src + n_par + (1 if has_s0 else 0)
    n_out = 2 + (len(in_widths) - n_act if in_widths else 0)
    ins, (o_ref, sout_ref), scr = refs[:n_in], refs[n_in:n_in + 2], refs[n_in + n_out:]
    pars = ins[n_src:n_src + n_par]
    st_ref = scr[0]
    pad_refs = scr[1:]
    t_idx = pl.program_id(1)
    padded = rows_in < chunk
    assert (padded and not in_widths) or nb == 1

    @pl.when(t_idx == 0)
    def _():
        for i in range(nb):
            for h in range(N_HEADS):
                if has_s0:
                    st_ref[i * N_HEADS + h] = ins[-1][i, h]
                else:
                    st_ref[i * N_HEADS + h] = jnp.zeros(st_ref.shape[1:], F32)

    if in_widths:
        x_ref, gm_ref, win_ref = ins[:n_src]
        hx = _rms(x_ref[0], gm_ref[...]).astype(BF16)
        offs = [sum(in_widths[:i]) for i in range(len(in_widths))]
        ld = [jnp.dot(hx, win_ref[:, o:o + w], preferred_element_type=F32) for o, w in zip(offs, in_widths)]
        for extra_ref, extra in zip(refs[n_in + 2:n_in + n_out], ld[n_act:]):
            extra_ref[0] = extra
        ld = ld[:n_act]
    elif padded:
        for p_ref, a_ref in zip(pad_refs, ins[:n_act]):
            p_ref[...] = jnp.zeros(p_ref.shape, F32)
            for i in range(nb):
                p_ref[i * chunk:i * chunk + rows_in, :] = a_ref[i]
        ld = [a[...] for a in pad_refs]
    else:
        ld = [a[0] for a in ins[:n_act]]
    groups = nb if padded else n_chunks
    span = groups * chunk

    if mode == "hgrn":
        q_raw, f, v, gate = ld
        lb_ref, gn_ref = pars
        lbv = lb_ref[...]
        e = jnp.exp(lbv - jnp.max(lbv, axis=0, keepdims=True))
        lb = jnp.sum(e[0:layer + 1], axis=0, keepdims=True) / jnp.sum(e, axis=0, keepdims=True)
        forget = lb + (1.0 - lb) * _sigmoid(f)
        k = 1.0 - forget
        lg = jnp.log(forget)
        q = _silu(q_raw)
    else:
        q_raw, k, v, gate, gd = ld
        wg_ref, bg_ref, gn_ref = pars
        z = jnp.dot(gd.astype(BF16), wg_ref[...], preferred_element_type=F32) + bg_ref[...]
        lg = (jnp.minimum(z, 0.0) - jnp.log(1.0 + jnp.exp(-jnp.abs(z)))) / GLA_GATE_TAU
        q = q_raw * (HEAD_DK ** -0.5)
    if padded:
        live = lax.broadcasted_iota(jnp.int32, (span, 1), 0) % chunk < rows_in
        lg = jnp.where(live, lg, 0.0)
        k = jnp.where(live, k, 0.0)
    b = _cumsum_rows(lg, chunk)

    def piece(x, i, h, width):
        return x[i * chunk:(i + 1) * chunk, h * width:(h + 1) * width]

    def heads(small_decay):
        rows = []
        for i in range(groups):
            cols = []
            for h in range(N_HEADS):
                cols.append(_gla_head(piece(q, i, h, HEAD_DK), piece(k, i, h, HEAD_DK),
                                      piece(b, i, h, HEAD_DK), piece(v, i, h, dv),
                                      st_ref.at[(i if padded else 0) * N_HEADS + h], small_decay,
                                      rows_in if padded else chunk))
            rows.append(jnp.concatenate(cols, axis=1))
        return rows[0] if groups == 1 else jnp.concatenate(rows, axis=0)

    if chunk <= SUB_BLOCK:
        o_raw = heads(False)
    else:
        o_raw = lax.cond(jnp.min(b) >= -GLA_SAFE_DECAY,
                         functools.partial(heads, True), functools.partial(heads, False))
    o_all = jnp.concatenate(
        [_rms(o_raw[:, h * dv:(h + 1) * dv], gn_ref[...]) * _silu(gate[:, h * dv:(h + 1) * dv])
         for h in range(N_HEADS)], axis=1).astype(o_ref.dtype)
    if padded:
        for i in range(nb):
            o_ref[i] = o_all[i * chunk:i * chunk + rows_in]
    else:
        o_ref[0] = o_all

    @pl.when(t_idx == pl.num_programs(1) - 1)
    def _():
        for i in range(nb):
            for h in range(N_HEADS):
                sout_ref[i, h] = st_ref[i * N_HEADS + h]


def _gla_call(mode, src, params, s0, *, layer, dv, tb, chunk, nb, extra_widths=()):
    fused_in = isinstance(src, tuple)
    proj = src[0] if fused_in else src
    bsz, seq, _ = proj.shape
    rows_in = min(tb, seq)
    if rows_in < chunk:
        assert seq == rows_in and bsz % nb == 0
        nt, n_chunks = 1, 1
    else:
        assert seq % tb == 0 and tb % chunk == 0 and nb == 1
        nt, n_chunks = seq // tb, tb // chunk
    kw, vw = N_HEADS * HEAD_DK, N_HEADS * dv

    def act(width, col_block):
        return pl.BlockSpec((nb, rows_in, width), lambda b, t: (b, t, col_block))

    def whole(a):
        return pl.BlockSpec(a.shape, lambda b, t: (0,) * a.ndim)

    if mode == "hgrn":
        act_specs = [act(kw, 0), act(kw, 1), act(vw, 2), act(vw, 3)]
        act_widths = [kw, kw, vw, vw]
    else:
        act_specs = [act(kw, 0), act(kw, 1), act(vw, kw * 2 // vw), act(vw, kw * 2 // vw + 1),
                     act(128, (2 * kw + 2 * vw) // 128)]
        act_widths = [kw, kw, vw, vw, 128]
    if fused_in:
        x, g_mix, w_in = src[0], _whole(src[1]), _whole(src[2])
        in_specs = [pl.BlockSpec((nb, rows_in, x.shape[-1]), lambda b, t: (b, t, 0)), g_mix[1], w_in[1]]
        args = [x, g_mix[0], w_in[0]]
    else:
        in_specs, args = act_specs, [proj] * len(act_specs)
    in_specs = in_specs + [whole(p) for p in params]
    args = args + list(params)
    state_spec = pl.BlockSpec((nb, N_HEADS, HEAD_DK, dv), lambda b, t: (b, 0, 0, 0))
    if s0 is not None:
        in_specs.append(state_spec)
        args.append(s0)
    scratch = [pltpu.VMEM((nb * N_HEADS, HEAD_DK, dv), F32)]
    if rows_in < chunk:
        scratch += [pltpu.VMEM((nb * chunk, w), F32) for w in act_widths]
    return pl.pallas_call(
        functools.partial(_gla_kernel, mode=mode, layer=layer, dv=dv, nb=nb, rows_in=rows_in,
                          chunk=chunk, n_chunks=n_chunks, has_s0=s0 is not None,
                          in_widths=tuple(act_widths) + tuple(extra_widths) if fused_in else None),
        grid=(bsz // nb, nt),
        in_specs=in_specs,
        out_specs=[pl.BlockSpec((nb, rows_in, vw), lambda b, t: (b, t, 0)), state_spec]
        + [pl.BlockSpec((nb, rows_in, w), lambda b, t: (b, t, 0)) for w in extra_widths],
        out_shape=[jax.ShapeDtypeStruct((bsz, seq, vw), BF16),
                   jax.ShapeDtypeStruct((bsz, N_HEADS, HEAD_DK, dv), F32)]
        + [jax.ShapeDtypeStruct((bsz, seq, w), F32) for w in extra_widths],
        scratch_shapes=scratch,
        compiler_params=_cparams("arbitrary", "arbitrary"),
        name="gla_" + mode,
    )(*args)


def _s5_prep_kernel(lre_ref, lim_ref, ls_ref, lre_x_ref, lim_x_ref, ls_x_ref, bre_ref, bim_ref,
                    are_ref, aim_ref, bbre_ref, bbim_ref):
    def disc(lre, lim, ls):
        lr = jnp.minimum(lre, S5_MAX_RE)
        dt = jnp.exp(ls)
        mag = jnp.exp(lr * dt)
        a_re = mag * jnp.cos(lim * dt)
        a_im = mag * jnp.sin(lim * dt)
        den = lr * lr + lim * lim
        z_re = ((a_re - 1.0) * lr + a_im * lim) / den
        z_im = (a_im * lr - (a_re - 1.0) * lim) / den
        return a_re, a_im, z_re, z_im

    a_re, a_im, _, _ = disc(lre_ref[...], lim_ref[...], ls_ref[...])
    are_ref[...] = a_re
    aim_ref[...] = a_im
    _, _, z_re, z_im = disc(lre_x_ref[...], lim_x_ref[...], ls_x_ref[...])
    bbre_ref[...] = z_re * bre_ref[...] - z_im * bim_ref[...]
    bbim_ref[...] = z_re * bim_ref[...] + z_im * bre_ref[...]


def _s5_prep(lam_re, lam_im, log_step, b_re, b_im):
    g, n = lam_re.shape
    p = b_re.shape[-1]
    ls = jnp.broadcast_to(log_step[:, None], (g, n))
    rep = lambda a: jnp.repeat(a, p, axis=1)
    outs = pl.pallas_call(
        _s5_prep_kernel,
        out_shape=[jax.ShapeDtypeStruct((g, n), F32)] * 2 + [jax.ShapeDtypeStruct((g, n * p), F32)] * 2,
        name="s5_prep",
    )(lam_re, lam_im, ls, rep(lam_re), rep(lam_im), rep(ls),
      b_re.reshape(g, n * p), b_im.reshape(g, n * p))
    a_re, a_im, bb_re, bb_im = outs
    return a_re, a_im, bb_re.reshape(g, n, p), bb_im.reshape(g, n, p)


def _s5_kernel(*refs, bg, tc, nt, cw, has_x0):
    n_in = 9 if has_x0 else 7
    u_ref, bm_ref, cm_ref, a_ref, d_ref, wg_ref, bgl_ref = refs[:7]
    o_ref, sre_ref, sim_ref = refs[n_in:n_in + 3]
    utm_ref, xs_ref, st_ref = refs[n_in + 3:]
    t_idx = pl.program_id(1)
    half = xs_ref.shape[1] // 2
    uw = u_ref.shape[-1] // 2

    @pl.when(t_idx == 0)
    def _():
        for hf in range(2):
            if has_x0:
                st_ref[:, hf * 2 * half:hf * 2 * half + half] = refs[7][:, hf * half:(hf + 1) * half]
                st_ref[:, hf * 2 * half + half:(hf + 1) * 2 * half] = refs[8][:, hf * half:(hf + 1) * half]
            else:
                st_ref[...] = jnp.zeros(st_ref.shape, F32)

    for t in range(tc):
        utm_ref[t * bg:(t + 1) * bg, :] = u_ref[:, t, :]
    u = utm_ref[...]
    ys = []
    for hf in range(2):
        xs_ref[...] = jnp.dot(u[:, hf * uw:(hf + 1) * uw].astype(BF16), bm_ref[hf],
                              preferred_element_type=F32)
        base = hf * 2 * half
        for c0 in range(0, half, cw):
            ar = a_ref[0:1, base + c0:base + c0 + cw]
            ai = a_ref[0:1, base + half + c0:base + half + c0 + cw]
            xr = st_ref[:, base + c0:base + c0 + cw]
            xi = st_ref[:, base + half + c0:base + half + c0 + cw]
            for t in range(tc):
                rows = slice(t * bg, (t + 1) * bg)
                nr = ar * xr - ai * xi + xs_ref[rows, c0:c0 + cw]
                ni = ar * xi + ai * xr + xs_ref[rows, half + c0:half + c0 + cw]
                xs_ref[rows, c0:c0 + cw] = nr
                xs_ref[rows, half + c0:half + c0 + cw] = ni
                xr, xi = nr, ni
            st_ref[:, base + c0:base + c0 + cw] = xr
            st_ref[:, base + half + c0:base + half + c0 + cw] = xi
        ys.append(jnp.dot(xs_ref[...].astype(BF16), cm_ref[hf], preferred_element_type=F32))
    y = jnp.concatenate(ys, axis=1) + d_ref[...] * u
    y = 0.5 * y * (1.0 + jnp.tanh(math.sqrt(2.0 / math.pi) * (y + 0.044715 * (y * y * y))))
    gate = jnp.dot(y.astype(BF16), wg_ref[...], preferred_element_type=F32) + bgl_ref[...]
    utm_ref[...] = y * _sigmoid(gate)
    for t in range(tc):
        o_ref[:, t, :] = utm_ref[t * bg:(t + 1) * bg, :]

    @pl.when(t_idx == nt - 1)
    def _():
        for hf in range(2):
            sre_ref[:, hf * half:(hf + 1) * half] = st_ref[:, hf * 2 * half:hf * 2 * half + half]
            sim_ref[:, hf * half:(hf + 1) * half] = st_ref[:, hf * 2 * half + half:(hf + 1) * 2 * half]


def _s5_call(u_src, col_block, mats, x0, *, tc, cw):
    bg, seq, _ = u_src.shape
    assert seq % tc == 0
    nt = seq // tc
    bmat, cmat, a_flat, d_row, w_glu, b_glu = mats
    width = d_row.shape[1]
    nstate = a_flat.shape[1] // 2
    whole = lambda a: pl.BlockSpec(a.shape, lambda g, t: (0,) * a.ndim)
    in_specs = [pl.BlockSpec((bg, tc, width), lambda g, t: (0, t, col_block))]
    in_specs += [whole(m) for m in mats]
    args = [u_src] + list(mats)
    st_spec = pl.BlockSpec((bg, nstate), lambda g, t: (0, 0))
    if x0 is not None:
        in_specs += [st_spec, st_spec]
        args += list(x0)
    scratch = [pltpu.VMEM((bg * tc, width), F32), pltpu.VMEM((bg * tc, nstate), F32),
               pltpu.VMEM((bg, 2 * nstate), F32)]
    return pl.pallas_call(
        functools.partial(_s5_kernel, bg=bg, tc=tc, nt=nt, cw=cw, has_x0=x0 is not None),
        grid=(1, nt),
        in_specs=in_specs,
        out_specs=[pl.BlockSpec((bg, tc, width), lambda g, t: (0, t, 0)), st_spec, st_spec],
        out_shape=[jax.ShapeDtypeStruct((bg, seq, width), F32),
                   jax.ShapeDtypeStruct((bg, nstate), F32),
                   jax.ShapeDtypeStruct((bg, nstate), F32)],
        scratch_shapes=scratch,
        compiler_params=_cparams("arbitrary", "arbitrary"),
        name="s5",
    )(*args)


def _s5_matrices(a_re, a_im, bb_re, bb_im, c_re, c_im, d, w_glu, b_glu):
    g, n, p = bb_re.shape
    gh = g // 2
    eye = jnp.eye(gh, dtype=F32)

    def block_diag(t):
        return (eye[:, None, :, None] * t[:, :, None, :]).reshape(gh * t.shape[1], gh * t.shape[2])

    def in_mat(bb):
        return block_diag(bb.transpose(0, 2, 1))

    def out_mat(cc):
        return block_diag(cc.transpose(0, 2, 1))

    bmat = jnp.stack([jnp.concatenate([in_mat(bb_re[h * gh:(h + 1) * gh]),
                                       in_mat(bb_im[h * gh:(h + 1) * gh])], axis=1)
                      for h in range(2)]).astype(BF16)
    cmat = jnp.stack([jnp.concatenate([out_mat(c_re[h * gh:(h + 1) * gh]),
                                       out_mat(-c_im[h * gh:(h + 1) * gh])], axis=0)
                      for h in range(2)]).astype(BF16)
    a_flat = jnp.concatenate([jnp.concatenate([a_re[h * gh:(h + 1) * gh].reshape(1, gh * n),
                                               a_im[h * gh:(h + 1) * gh].reshape(1, gh * n)], axis=1)
                              for h in range(2)], axis=1)
    return (bmat, cmat, a_flat, d.reshape(1, g * p), w_glu.astype(BF16), b_glu.reshape(1, -1))


def _xattn_kernel(*refs, n_pre):
    x_ref = refs[0]
    g_ref, wq_ref, wo_ref, k_ref, v_ref, y_ref, kt_ref, vt_ref = refs[1 + 2 * n_pre:]

    @pl.when(pl.program_id(1) == 0)
    def _():
        kt_ref[...] = jnp.transpose(k_ref[...], (1, 0, 2)).astype(BF16)
        vt_ref[...] = jnp.transpose(v_ref[...], (1, 0, 2)).astype(BF16)

    x = x_ref[...]
    for a_ref, w_ref in zip(refs[1:1 + n_pre], refs[1 + n_pre:1 + 2 * n_pre]):
        x = x + jnp.dot(a_ref[...].astype(BF16), w_ref[...], preferred_element_type=F32)
    q = jnp.dot(_rms(x, g_ref[...]).astype(BF16), wq_ref[...], preferred_element_type=F32).astype(BF16)
    hd = q.shape[1] // N_HEADS
    outs = []
    for h in range(N_HEADS):
        s = lax.dot_general(q[:, h * hd:(h + 1) * hd], kt_ref[h], _NT,
                            preferred_element_type=F32) * (hd ** -0.5)
        p = jnp.exp(s - jnp.max(s, axis=-1, keepdims=True))
        p = p / jnp.sum(p, axis=-1, keepdims=True)
        outs.append(jnp.dot(p.astype(BF16), vt_ref[h], preferred_element_type=F32).astype(BF16))
    y_ref[...] = x + jnp.dot(jnp.concatenate(outs, axis=1), wo_ref[...], preferred_element_type=F32)


def _xattn_call(x, pre_a, pre_w, g, w_q, w_o, mem_k, mem_v, layer, *, tq):
    bsz, seq, d = x.shape
    n_mem, nh, hd = mem_k.shape[2:]
    assert seq % tq == 0
    kv_spec = pl.BlockSpec((None, None, n_mem, nh, hd), lambda b, t: (layer, b, 0, 0, 0))
    rows = lambda a: pl.BlockSpec((None, tq, a.shape[-1]), lambda b, t: (b, t, 0))
    params = [_whole(w) for w in (*pre_w, g, w_q, w_o)]
    return pl.pallas_call(
        functools.partial(_xattn_kernel, n_pre=len(pre_a)),
        grid=(bsz, seq // tq),
        in_specs=[rows(x)] + [rows(a) for a in pre_a] + [spec for _, spec in params] + [kv_spec, kv_spec],
        out_specs=rows(x),
        out_shape=jax.ShapeDtypeStruct((bsz, seq, d), F32),
        scratch_shapes=[pltpu.VMEM((nh, n_mem, hd), BF16), pltpu.VMEM((nh, n_mem, hd), BF16)],
        compiler_params=_cparams("arbitrary", "arbitrary"),
        name="xattn",
    )(x, *pre_a, *[w for w, _ in params], mem_k, mem_v)


def _attn_rows_kernel(q_ref, k_ref, v_ref, o_ref, pad_ref, *, rows_in):
    nb, rows, d = pad_ref.shape
    nblk = d // 128
    half_blk = nblk // 2
    lanes = k_ref.shape[1]
    lane_blk = lax.broadcasted_iota(jnp.int32, (1, lanes), 1) % nblk
    row_head = lax.broadcasted_iota(jnp.int32, (N_HEADS * rows, 1), 0) // rows
    live = lane_blk == row_head
    scale = (d // N_HEADS) ** -0.5
    for i in range(nb):
        pad_ref[i] = jnp.zeros((rows, d), F32)
        pad_ref[i, 0:rows_in, :] = q_ref[i]
        q = pad_ref[i]
        qx = jnp.concatenate([q[:, j * 128:(j + 1) * 128] for j in range(nblk)], axis=0)
        g = lax.dot_general(qx.astype(BF16), k_ref[i].astype(BF16), _NT, preferred_element_type=F32)
        s = jnp.concatenate(
            [g[2 * h * rows:(2 * h + 1) * rows]
             + pltpu.roll(g[(2 * h + 1) * rows:(2 * h + 2) * rows], lanes - half_blk, axis=1)
             for h in range(N_HEADS)], axis=0) * scale
        s = jnp.where(live, s, -1e30)
        e = jnp.exp(s - jnp.max(s, axis=-1, keepdims=True))
        p = e / jnp.sum(e, axis=-1, keepdims=True)
        px = jnp.concatenate(
            [blk for h in range(N_HEADS)
             for blk in (p[h * rows:(h + 1) * rows], pltpu.roll(p[h * rows:(h + 1) * rows], half_blk, axis=1))],
            axis=0)
        o = jnp.dot(px.astype(BF16), v_ref[i].astype(BF16), preferred_element_type=F32)
        o = jnp.concatenate([o[j * rows:(j + 1) * rows] for j in range(nblk)], axis=1)
        o_ref[i] = o[0:rows_in].astype(o_ref.dtype)


def _attn_rows_call(q, mem_k, mem_v, layer, *, nb):
    bsz, seq, d = q.shape
    depth, _, n_mem, nh, hd = mem_k.shape
    assert bsz % nb == 0 and hd == 256 and nh == N_HEADS and seq <= 16
    as_rows = lambda a: a.reshape(depth, bsz, n_mem, nh, 2, 128).transpose(0, 1, 2, 4, 3, 5).reshape(
        depth, bsz, n_mem * 2 * nh, 128)
    kv_spec = pl.BlockSpec((None, nb, n_mem * 2 * nh, 128), lambda b: (layer, b, 0, 0))
    return pl.pallas_call(
        functools.partial(_attn_rows_kernel, rows_in=seq),
        grid=(bsz // nb,),
        in_specs=[pl.BlockSpec((nb, seq, d), lambda b: (b, 0, 0)), kv_spec, kv_spec],
        out_specs=pl.BlockSpec((nb, seq, d), lambda b: (b, 0, 0)),
        out_shape=jax.ShapeDtypeStruct((bsz, seq, d), BF16),
        scratch_shapes=[pltpu.VMEM((nb, 8 * pl.cdiv(seq, 8), d), F32)],
        compiler_params=_cparams("arbitrary"),
        name="mem_attn_rows",
    )(q, as_rows(mem_k), as_rows(mem_v))


FFN_COLS = 256


def _ffn_kernel(*refs, tm, ts, hs, f_dim, has_hist, has_final):
    n_in = 6 + int(has_hist) + int(has_final)
    x_ref, g_ref, wup_ref, cw_ref, cb_ref, wdn_ref = refs[:6]
    y_ref, state_ref = refs[n_in:n_in + 2]
    gated_ref, hist_ref = refs[n_in + 2:n_in + 4]
    t_idx = pl.program_id(1)

    @pl.when(t_idx == 0)
    def _():
        hist_ref[...] = jnp.zeros(hist_ref.shape, F32)
        if has_hist:
            hist_ref[hs - 2 * ts:hs, :] = refs[6][...]

    x = x_ref[...]
    h = _rms(x, g_ref[...]).astype(BF16)
    row = lax.broadcasted_iota(jnp.int32, (tm, 1), 0)
    for c in range(f_dim // FFN_COLS):
        conv = []
        for part in range(2):
            cols = slice(part * f_dim + c * FFN_COLS, part * f_dim + (c + 1) * FFN_COLS)
            u = jnp.dot(h, wup_ref[:, cols], preferred_element_type=F32)
            if ts == 1:
                prev2, prev1 = hist_ref[hs - 2:hs - 1, cols], hist_ref[hs - 1:hs, cols]
                m1 = jnp.where(row == 0, prev1, pltpu.roll(u, 1, axis=0))
                m2 = jnp.where(row == 0, prev2, jnp.where(row == 1, prev1, pltpu.roll(u, 2, axis=0)))
            else:
                ext = jnp.concatenate([hist_ref[hs - 2 * ts:hs, cols], u], axis=0)
                m2, m1 = ext[0:tm], ext[ts:ts + tm]
            conv.append(cb_ref[:, cols] + cw_ref[0:1, cols] * m2 + cw_ref[1:2, cols] * m1
                        + cw_ref[2:3, cols] * u)
            hist_ref[:, cols] = u[tm - hs:tm]
        gated_ref[:, c * FFN_COLS:(c + 1) * FFN_COLS] = (_silu(conv[0]) * conv[1]).astype(BF16)
    out = x + jnp.dot(gated_ref[...], wdn_ref[...], preferred_element_type=F32)
    if has_final:
        out = _rms(out, refs[n_in - 1][...])
    y_ref[...] = out

    @pl.when(t_idx == pl.num_programs(1) - 1)
    def _():
        state_ref[...] = hist_ref[hs - 2 * ts:hs, :]


def _ffn_call(x, g, w_up, conv_w, conv_b, w_down, hist0, g_final, *, tm, ts):
    ngrp, rows, d = x.shape
    params = [_whole(a) for a in (g, w_up, conv_w, conv_b, w_down)]
    f2 = params[1][0].shape[-1]
    f_dim = f2 // 2
    hs = max(8, 2 * ts)
    assert rows % tm == 0 and tm >= hs and f_dim % FFN_COLS == 0 and (ts == 1 or ts % 8 == 0)
    args = [x] + [a for a, _ in params]
    in_specs = [pl.BlockSpec((None, tm, d), lambda s, t: (s, t, 0))] + [spec for _, spec in params]
    st_spec = pl.BlockSpec((None, 2 * ts, f2), lambda s, t: (s, 0, 0))
    if hist0 is not None:
        in_specs.append(st_spec)
        args.append(hist0)
    if g_final is not None:
        in_specs.append(_whole(g_final)[1])
        args.append(g_final)
    return pl.pallas_call(
        functools.partial(_ffn_kernel, tm=tm, ts=ts, hs=hs, f_dim=f_dim,
                          has_hist=hist0 is not None, has_final=g_final is not None),
        grid=(ngrp, rows // tm),
        in_specs=in_specs,
        out_specs=[pl.BlockSpec((None, tm, d), lambda s, t: (s, t, 0)), st_spec],
        out_shape=[jax.ShapeDtypeStruct((ngrp, rows, d), F32),
                   jax.ShapeDtypeStruct((ngrp, 2 * ts, f2), F32)],
        scratch_shapes=[pltpu.VMEM((tm, f_dim), BF16), pltpu.VMEM((hs, f2), F32)],
        compiler_params=_cparams("arbitrary", "arbitrary"),
        name="conv_ffn",
    )(*args)


def _trunk(x, mem_k, mem_v, states, p, *, prompt):
    bsz, seq, d = x.shape
    depth = p["norm_mix"].shape[0]
    m = bsz * seq
    x2 = x.reshape(m, d)
    new = {"hgrn": [], "s5_re": [], "s5_im": [], "gla": [], "conv": []}
    if prompt:
        gla_tiles = dict(tb=1024, chunk=64, nb=1)
        attn_tq = 1024
    else:
        gla_tiles = dict(tb=seq, chunk=16, nb=8)
        attn_tq = seq
    for l in range(depth):
        g_mix = _pick(p["norm_mix"], l)
        if l % 2 == 0:
            e = l // 2
            kw = N_HEADS * HEAD_DK
            x3 = x2.reshape(bsz, seq, d)
            hgrn_par = [p["hgrn_lb"], p["hgrn_gnorm"][e].reshape(1, -1)]
            mats = p["s5_mats"][e]
            if prompt:
                o_a, s_a, u = _gla_call("hgrn", (x3, g_mix, _pick(p["w_in_ab"], e)), hgrn_par, None,
                                        layer=l, dv=kw // N_HEADS, extra_widths=(kw,), **gla_tiles)
                o_b, sr, si = _s5_call(u, 0, mats, None, tc=128, cw=512)
            else:
                proj = _norm_proj(x2, g_mix, _pick(p["w_in_ab"], e)).reshape(bsz, seq, -1)
                o_a, s_a = _gla_call("hgrn", proj, hgrn_par, states["hgrn"][e],
                                     layer=l, dv=kw // N_HEADS, **gla_tiles)
                x0 = (states["s5_re"][e].reshape(bsz, -1), states["s5_im"][e].reshape(bsz, -1))
                o_b, sr, si = _s5_call(proj, 4, mats, x0, tc=seq, cw=128)
            mixed = [o_a, o_b]
            mixed_w = [_pick(p["w_out_ab"], e, rows=kw, row_block=0), _pick(p["w_out_ab"], e, rows=kw, row_block=1)]
            new["hgrn"].append(s_a)
            new["s5_re"].append(sr.reshape(bsz, -1, S5_STATE))
            new["s5_im"].append(si.reshape(bsz, -1, S5_STATE))
        else:
            o_idx = l // 2
            if prompt:
                src_c = (x2.reshape(bsz, seq, d), g_mix, _pick(p["w_in_c"], o_idx))
            else:
                src_c = _norm_proj(x2, g_mix, _pick(p["w_in_c"], o_idx)).reshape(bsz, seq, -1)
            o_c, s_c = _gla_call(
                "gla", src_c,
                [p["gla_w_gate"][o_idx], p["gla_b_gate"][o_idx].reshape(1, -1),
                 p["gla_gnorm"][o_idx].reshape(1, -1)],
                None if states is None else states["gla"][o_idx],
                layer=l, dv=d // N_HEADS, **{**gla_tiles, **(dict(chunk=256) if prompt else {})})
            mixed, mixed_w = [o_c], [_pick(p["w_out_c"], o_idx)]
            new["gla"].append(s_c)
        g_cross, w_q, w_o = _pick(p["norm_cross"], l), _pick(p["xa_w_q"], l), _pick(p["xa_w_o"], l)
        if prompt:
            x2 = _xattn_call(x2.reshape(bsz, seq, d), mixed, mixed_w, g_cross, w_q, w_o,
                             mem_k, mem_v, l, tq=attn_tq).reshape(m, d)
        else:
            x2 = _proj_res(x2, [a.reshape(m, -1) for a in mixed], mixed_w)
            q = _norm_proj(x2, g_cross, w_q)
            o_x = _attn_rows_call(q.reshape(bsz, seq, d), mem_k, mem_v, l, nb=8)
            x2 = _proj_res(x2, [o_x.reshape(m, d)], [w_o])
        g_final = p["norm_final"] if l == depth - 1 else None
        ffn_w = tuple(_pick(p[name], l) for name in
                      ("norm_ffn", "ffn_w_up", "ffn_conv_w", "ffn_conv_b", "ffn_w_down"))
        if prompt:
            y, cst = _ffn_call(x2.reshape(bsz, seq, d), *ffn_w, None, g_final, tm=1024, ts=1)
            x2 = y.reshape(m, d)
        else:
            xt = x2.reshape(bsz, seq, d).transpose(1, 0, 2).reshape(1, m, d)
            hist0 = states["conv"][l].transpose(1, 0, 2).reshape(1, 2 * bsz, -1)
            y, cst = _ffn_call(xt, *ffn_w, hist0, g_final, tm=m, ts=bsz)
            x2 = y.reshape(seq, bsz, d).transpose(1, 0, 2).reshape(m, d)
            cst = cst.reshape(2, bsz, -1).transpose(1, 0, 2)
        new["conv"].append(cst)
    return x2.reshape(bsz, seq, d), new


def kernel(x_prompt, x_sample, mem_prompt, cache_mem_k, cache_mem_v, state_hgrn, state_s5_re, state_s5_im, state_gla, state_ffn_conv, norm_mix, norm_cross, norm_mem, norm_ffn, norm_final, w_in_ab, hgrn_lb, hgrn_gnorm, s5_lam_re, s5_lam_im, s5_log_step, s5_b_re, s5_b_im, s5_c_re, s5_c_im, s5_d, s5_w_glu, s5_b_glu, w_out_ab, w_in_c, gla_w_gate_up, gla_b_gate, gla_gnorm, w_out_c, xa_w_q, xa_w_kv, xa_w_o, ffn_w_up, ffn_conv_w, ffn_conv_b, ffn_w_down):
    depth, d = norm_mix.shape

    gla_cols = w_in_c.shape[2]
    gate_rank = gla_w_gate_up.shape[1]
    pad_c = (-gla_cols) % 128
    w_in_c_p = jnp.pad(w_in_c, ((0, 0), (0, 0), (0, pad_c))).astype(BF16)
    gla_w_gate = jnp.pad(gla_w_gate_up, ((0, 0), (0, 128 - gate_rank), (0, 0))).astype(BF16)

    s5_mats = []
    for e in range(s5_lam_re.shape[0]):
        a_re, a_im, bb_re, bb_im = _s5_prep(s5_lam_re[e], s5_lam_im[e], s5_log_step[e],
                                            s5_b_re[e], s5_b_im[e])
        s5_mats.append(_s5_matrices(a_re, a_im, bb_re, bb_im, s5_c_re[e], s5_c_im[e], s5_d[e],
                                    s5_w_glu[e], s5_b_glu[e]))

    row = lambda a: a.reshape(a.shape[0], 1, a.shape[1])
    p = dict(norm_mix=row(norm_mix), norm_cross=row(norm_cross), norm_ffn=row(norm_ffn),
             norm_final=norm_final.reshape(1, d),
             w_in_ab=w_in_ab.astype(BF16), hgrn_lb=hgrn_lb, hgrn_gnorm=hgrn_gnorm, s5_mats=s5_mats,
             w_out_ab=w_out_ab.astype(BF16), w_in_c=w_in_c_p, gla_w_gate=gla_w_gate,
             gla_b_gate=gla_b_gate, gla_gnorm=gla_gnorm, w_out_c=w_out_c.astype(BF16),
             xa_w_q=xa_w_q.astype(BF16), xa_w_o=xa_w_o.astype(BF16),
             ffn_w_up=ffn_w_up.astype(BF16), ffn_conv_w=ffn_conv_w, ffn_conv_b=row(ffn_conv_b),
             ffn_w_down=ffn_w_down.astype(BF16))

    mem_k_p, mem_v_p = _mem_kv(mem_prompt, norm_mem.reshape(depth, 1, d), xa_w_kv.astype(BF16))
    y_prompt, st_p = _trunk(x_prompt, mem_k_p, mem_v_p, None, p, prompt=True)

    states = dict(hgrn=state_hgrn, s5_re=state_s5_re, s5_im=state_s5_im, gla=state_gla,
                  conv=state_ffn_conv)
    y_sample, st_s = _trunk(x_sample, cache_mem_k, cache_mem_v, states, p, prompt=False)

    stack = lambda xs: xs[0][None] if len(xs) == 1 else jnp.stack(xs)
    return (y_prompt, y_sample,
            stack(st_p["hgrn"]), stack(st_p["s5_re"]), stack(st_p["s5_im"]), stack(st_p["gla"]),
            mem_k_p, mem_v_p, stack(st_p["conv"]),
            stack(st_s["hgrn"]), stack(st_s["s5_re"]), stack(st_s["s5_im"]), stack(st_s["gla"]),
            stack(st_s["conv"]))
```

```python
import functools
import math

import jax
import jax.numpy as jnp
from jax import lax
from jax.experimental import pallas as pl
from jax.experimental.pallas import tpu as pltpu

F32 = jnp.float32
BF16 = jnp.bfloat16

EPS = 1e-6
S5_MAX_RE = -1e-4
GLA_GATE_TAU = 16.0
N_HEADS = 4
HEAD_DK = 128
S5_GROUP = 16
S5_STATE = 64
SUB_BLOCK = 16
GLA_SAFE_DECAY = 64.0
VMEM_LIMIT = 56 * 1024 * 1024

_NT = (((1,), (1,)), ((), ()))
_TN = (((0,), (0,)), ((), ()))


def _cparams(*sem):
    return pltpu.CompilerParams(dimension_semantics=sem, vmem_limit_bytes=VMEM_LIMIT)


def _rms(x, g):
    return x * lax.rsqrt(jnp.mean(x * x, axis=-1, keepdims=True) + EPS) * g


def _sigmoid(x):
    return 1.0 / (1.0 + jnp.exp(-x))


def _silu(x):
    return x * _sigmoid(x)


def _row_tile(rows, want):
    t = min(rows, want)
    assert rows % t == 0, (rows, t)
    return t


def _pick(a, layer, rows=None, row_block=0, cols=None, col_block=0):
    block = (None, rows or a.shape[1], cols or a.shape[2])
    return a, pl.BlockSpec(block, lambda *_: (layer, row_block, col_block))


def _whole(a):
    return a if isinstance(a, tuple) else (a, pl.BlockSpec(a.shape, lambda *_: (0,) * a.ndim))


def _norm_proj_kernel(x_ref, g_ref, w_ref, o_ref):
    h = _rms(x_ref[...], g_ref[...]).astype(BF16)
    n = o_ref.shape[1]
    for c0 in range(0, n, 512):
        cw = min(512, n - c0)
        o_ref[:, c0:c0 + cw] = jnp.dot(h, w_ref[:, c0:c0 + cw], preferred_element_type=F32)


def _norm_proj(x, g, w, *, tm=512):
    m, k = x.shape
    (g, g_spec), (w, w_spec) = _whole(g), _whole(w)
    n = w.shape[-1]
    tm = _row_tile(m, tm)
    return pl.pallas_call(
        _norm_proj_kernel,
        grid=(m // tm,),
        in_specs=[pl.BlockSpec((tm, k), lambda i: (i, 0)), g_spec, w_spec],
        out_specs=pl.BlockSpec((tm, n), lambda i: (i, 0)),
        out_shape=jax.ShapeDtypeStruct((m, n), F32),
        compiler_params=_cparams("arbitrary"),
        name="norm_proj",
    )(x, g, w)


def _mem_kv_kernel(x_ref, g_ref, w_ref, k_ref, v_ref):
    nb, n_mem, d = x_ref.shape
    h = _rms(x_ref[...].reshape(nb * n_mem, d), g_ref[...]).astype(BF16)
    hd = d // N_HEADS
    for o_ref, base in ((k_ref, 0), (v_ref, d)):
        for hh in range(N_HEADS):
            kv = jnp.dot(h, w_ref[:, base + hh * hd:base + (hh + 1) * hd], preferred_element_type=F32)
            for i in range(nb):
                o_ref[i, :, hh, :] = kv[i * n_mem:(i + 1) * n_mem]


def _mem_kv(mem, g, w, *, nb=2):
    bsz, n_mem, d = mem.shape
    depth = w.shape[0]
    assert bsz % nb == 0
    out_spec = pl.BlockSpec((None, nb, n_mem, N_HEADS, d // N_HEADS), lambda l, b: (l, b, 0, 0, 0))
    out_shape = jax.ShapeDtypeStruct((depth, bsz, n_mem, N_HEADS, d // N_HEADS), F32)
    return pl.pallas_call(
        _mem_kv_kernel,
        grid=(depth, bsz // nb),
        in_specs=[pl.BlockSpec((nb, n_mem, d), lambda l, b: (b, 0, 0)),
                  pl.BlockSpec((None, 1, d), lambda l, b: (l, 0, 0)),
                  pl.BlockSpec((None, d, 2 * d), lambda l, b: (l, 0, 0))],
        out_specs=[out_spec, out_spec],
        out_shape=[out_shape, out_shape],
        compiler_params=_cparams("arbitrary", "arbitrary"),
        name="mem_kv",
    )(mem, g, w)


def _proj_res_kernel(res_ref, *refs, n_in):
    acc = res_ref[...]
    for a_ref, w_ref in zip(refs[:n_in], refs[n_in:2 * n_in]):
        acc = acc + jnp.dot(a_ref[...].astype(BF16), w_ref[...], preferred_element_type=F32)
    refs[2 * n_in][...] = acc


def _proj_res(res, a_list, w_list, *, tm=512):
    m, n = res.shape
    tm = _row_tile(m, tm)
    n_in = len(a_list)
    w_list = [_whole(w) for w in w_list]
    in_specs = [pl.BlockSpec((tm, n), lambda i: (i, 0))]
    in_specs += [pl.BlockSpec((tm, a.shape[1]), lambda i: (i, 0)) for a in a_list]
    in_specs += [spec for _, spec in w_list]
    return pl.pallas_call(
        functools.partial(_proj_res_kernel, n_in=n_in),
        grid=(m // tm,),
        in_specs=in_specs,
        out_specs=pl.BlockSpec((tm, n), lambda i: (i, 0)),
        out_shape=jax.ShapeDtypeStruct((m, n), F32),
        compiler_params=_cparams("arbitrary"),
        name="proj_res",
    )(res, *a_list, *[w for w, _ in w_list])


def _cumsum_rows(x, c):
    hi = x.astype(BF16)
    rest = x - hi.astype(F32)
    mid = rest.astype(BF16)
    lo = (rest - mid.astype(F32)).astype(BF16)
    n = max(c, min(x.shape[0], 128))
    r = lax.broadcasted_iota(jnp.int32, (n, n), 0)
    col = lax.broadcasted_iota(jnp.int32, (n, n), 1)
    tri = jnp.where((r >= col) & (r // c == col // c), 1.0, 0.0).astype(BF16)
    tri3 = jnp.concatenate([tri, tri, tri], axis=1)
    out = [jnp.dot(tri3, jnp.concatenate([hi[s:s + n], mid[s:s + n], lo[s:s + n]], axis=0),
                   preferred_element_type=F32) for s in range(0, x.shape[0], n)]
    return out[0] if len(out) == 1 else jnp.concatenate(out, axis=0)


def _gla_head(qh, kh, bh, vh, st_ref, small_decay, live_rows):
    c, dk = qh.shape
    dv = vh.shape[1]
    sb = min(SUB_BLOCK, c)
    vb = vh.astype(BF16)
    st = st_ref[...]
    b_last = bh[c - 1:c, :]
    q_in = (qh * jnp.exp(bh)).astype(BF16)
    o_inter = jnp.dot(q_in, st.astype(BF16), preferred_element_type=F32)
    decay = jnp.transpose(jnp.broadcast_to(jnp.exp(b_last), (dk, dk)))
    decay = decay if dv == dk else jnp.concatenate([decay] * (dv // dk), axis=1)
    if small_decay:
        k_up = kh * jnp.exp(-bh)
        a = lax.dot_general(q_in, k_up.astype(BF16), _NT, preferred_element_type=F32)
        causal = (lax.broadcasted_iota(jnp.int32, (c, c), 0) >= lax.broadcasted_iota(jnp.int32, (c, c), 1))
        kd = (k_up * jnp.exp(b_last)).astype(BF16)
        st_ref[...] = st * decay + lax.dot_general(kd, vb, _TN, preferred_element_type=F32)
        return o_inter + jnp.dot(jnp.where(causal, a, 0.0).astype(BF16), vb, preferred_element_type=F32)
    kd = (kh * jnp.exp(b_last - bh)).astype(BF16)
    st_ref[...] = st * decay + lax.dot_general(kd, vb, _TN, preferred_element_type=F32)
    rows = lax.broadcasted_iota(jnp.int32, (sb, 1), 0)
    parts = []
    for s in range(c // sb):
        r0 = s * sb
        qs, ks, bs, vs = qh[r0:r0 + sb], kh[r0:r0 + sb], bh[r0:r0 + sb], vh[r0:r0 + sb]
        acc = o_inter[r0:r0 + sb]
        if s > 0:
            ref_b = bh[r0 - 1:r0, :]
            qf = (qs * jnp.exp(bs - ref_b)).astype(BF16)
            kf = (kh[0:r0] * jnp.exp(ref_b - bh[0:r0])).astype(BF16)
            a_off = lax.dot_general(qf, kf, _NT, preferred_element_type=F32)
            acc = acc + jnp.dot(a_off.astype(BF16), vb[0:r0], preferred_element_type=F32)
        for j in range(max(0, min(sb, live_rows - r0))):
            w = jnp.exp(jnp.minimum(bs - bs[j:j + 1], 0.0)) * qs * ks[j:j + 1]
            col = jnp.where(rows >= j, jnp.sum(w, axis=-1, keepdims=True), 0.0)
            acc = acc + col * vs[j:j + 1]
        parts.append(acc)
    return parts[0] if len(parts) == 1 else jnp.concatenate(parts, axis=0)


def _gla_kernel(*refs, mode, layer, dv, nb, rows_in, chunk, n_chunks, has_s0, in_widths):
    n_act = 4 if mode == "hgrn" else 5
    n_src = 3 if in_widths else n_act
    n_par = 2 if mode == "hgrn" else 3
    n_in = n_src + n_par + (1 if has_s0 else 0)
    n_out = 2 + (len(in_widths) - n_act if in_widths else 0)
    ins, (o_ref, sout_ref), scr = refs[:n_in], refs[n_in:n_in + 2], refs[n_in + n_out:]
    pars = ins[n_src:n_src + n_par]
    st_ref = scr[0]
    pad_refs = scr[1:]
    t_idx = pl.program_id(1)
    padded = rows_in < chunk
    assert (padded and not in_widths) or nb == 1

    @pl.when(t_idx == 0)
    def _():
        for i in range(nb):
            for h in range(N_HEADS):
                if has_s0:
                    st_ref[i * N_HEADS + h] = ins[-1][i, h]
                else:
                    st_ref[i * N_HEADS + h] = jnp.zeros(st_ref.shape[1:], F32)

    if in_widths:
        x_ref, gm_ref, win_ref = ins[:n_src]
        hx = _rms(x_ref[0], gm_ref[...]).astype(BF16)
        offs = [sum(in_widths[:i]) for i in range(len(in_widths))]
        ld = [jnp.dot(hx, win_ref[:, o:o + w], preferred_element_type=F32) for o, w in zip(offs, in_widths)]
        for extra_ref, extra in zip(refs[n_in + 2:n_in + n_out], ld[n_act:]):
            extra_ref[0] = extra
        ld = ld[:n_act]
    elif padded:
        for p_ref, a_ref in zip(pad_refs, ins[:n_act]):
            p_ref[...] = jnp.zeros(p_ref.shape, F32)
            for i in range(nb):
                p_ref[i * chunk:i * chunk + rows_in, :] = a_ref[i]
        ld = [a[...] for a in pad_refs]
    else:
        ld = [a[0] for a in ins[:n_act]]
    groups = nb if padded else n_chunks
    span = groups * chunk

    if mode == "hgrn":
        q_raw, f, v, gate = ld
        lb_ref, gn_ref = pars
        lbv = lb_ref[...]
        e = jnp.exp(lbv - jnp.max(lbv, axis=0, keepdims=True))
        lb = jnp.sum(e[0:layer + 1], axis=0, keepdims=True) / jnp.sum(e, axis=0, keepdims=True)
        forget = lb + (1.0 - lb) * _sigmoid(f)
        k = 1.0 - forget
        lg = jnp.log(forget)
        q = _silu(q_raw)
    else:
        q_raw, k, v, gate, gd = ld
        wg_ref, bg_ref, gn_ref = pars
        z = jnp.dot(gd.astype(BF16), wg_ref[...], preferred_element_type=F32) + bg_ref[...]
        lg = (jnp.minimum(z, 0.0) - jnp.log(1.0 + jnp.exp(-jnp.abs(z)))) / GLA_GATE_TAU
        q = q_raw * (HEAD_DK ** -0.5)
    if padded:
        live = lax.broadcasted_iota(jnp.int32, (span, 1), 0) % chunk < rows_in
        lg = jnp.where(live, lg, 0.0)
        k = jnp.where(live, k, 0.0)
    b = _cumsum_rows(lg, chunk)

    def piece(x, i, h, width):
        return x[i * chunk:(i + 1) * chunk, h * width:(h + 1) * width]

    def heads(small_decay):
        rows = []
        for i in range(groups):
            cols = []
            for h in range(N_HEADS):
                cols.append(_gla_head(piece(q, i, h, HEAD_DK), piece(k, i, h, HEAD_DK),
                                      piece(b, i, h, HEAD_DK), piece(v, i, h, dv),
                                      st_ref.at[(i if padded else 0) * N_HEADS + h], small_decay,
                                      rows_in if padded else chunk))
            rows.append(jnp.concatenate(cols, axis=1))
        return rows[0] if groups == 1 else jnp.concatenate(rows, axis=0)

    if chunk <= SUB_BLOCK:
        o_raw = heads(False)
    else:
        o_raw = lax.cond(jnp.min(b) >= -GLA_SAFE_DECAY,
                         functools.partial(heads, True), functools.partial(heads, False))
    o_all = jnp.concatenate(
        [_rms(o_raw[:, h * dv:(h + 1) * dv], gn_ref[...]) * _silu(gate[:, h * dv:(h + 1) * dv])
         for h in range(N_HEADS)], axis=1).astype(o_ref.dtype)
    if padded:
        for i in range(nb):
            o_ref[i] = o_all[i * chunk:i * chunk + rows_in]
    else:
        o_ref[0] = o_all

    @pl.when(t_idx == pl.num_programs(1) - 1)
    def _():
        for i in range(nb):
            for h in range(N_HEADS):
                sout_ref[i, h] = st_ref[i * N_HEADS + h]


def _gla_call(mode, src, params, s0, *, layer, dv, tb, chunk, nb, extra_widths=()):
    fused_in = isinstance(src, tuple)
    proj = src[0] if fused_in else src
    bsz, seq, _ = proj.shape
    rows_in = min(tb, seq)
    if rows_in < chunk:
        assert seq == rows_in and bsz % nb == 0
        nt, n_chunks = 1, 1
    else:
        assert seq % tb == 0 and tb % chunk == 0 and nb == 1
        nt, n_chunks = seq // tb, tb // chunk
    kw, vw = N_HEADS * HEAD_DK, N_HEADS * dv

    def act(width, col_block):
        return pl.BlockSpec((nb, rows_in, width), lambda b, t: (b, t, col_block))

    def whole(a):
        return pl.BlockSpec(a.shape, lambda b, t: (0,) * a.ndim)

    if mode == "hgrn":
        act_specs = [act(kw, 0), act(kw, 1), act(vw, 2), act(vw, 3)]
        act_widths = [kw, kw, vw, vw]
    else:
        act_specs = [act(kw, 0), act(kw, 1), act(vw, kw * 2 // vw), act(vw, kw * 2 // vw + 1),
                     act(128, (2 * kw + 2 * vw) // 128)]
        act_widths = [kw, kw, vw, vw, 128]
    if fused_in:
        x, g_mix, w_in = src[0], _whole(src[1]), _whole(src[2])
        in_specs = [pl.BlockSpec((nb, rows_in, x.shape[-1]), lambda b, t: (b, t, 0)), g_mix[1], w_in[1]]
        args = [x, g_mix[0], w_in[0]]
    else:
        in_specs, args = act_specs, [proj] * len(act_specs)
    in_specs = in_specs + [whole(p) for p in params]
    args = args + list(params)
    state_spec = pl.BlockSpec((nb, N_HEADS, HEAD_DK, dv), lambda b, t: (b, 0, 0, 0))
    if s0 is not None:
        in_specs.append(state_spec)
        args.append(s0)
    scratch = [pltpu.VMEM((nb * N_HEADS, HEAD_DK, dv), F32)]
    if rows_in < chunk:
        scratch += [pltpu.VMEM((nb * chunk, w), F32) for w in act_widths]
    return pl.pallas_call(
        functools.partial(_gla_kernel, mode=mode, layer=layer, dv=dv, nb=nb, rows_in=rows_in,
                          chunk=chunk, n_chunks=n_chunks, has_s0=s0 is not None,
                          in_widths=tuple(act_widths) + tuple(extra_widths) if fused_in else None),
        grid=(bsz // nb, nt),
        in_specs=in_specs,
        out_specs=[pl.BlockSpec((nb, rows_in, vw), lambda b, t: (b, t, 0)), state_spec]
        + [pl.BlockSpec((nb, rows_in, w), lambda b, t: (b, t, 0)) for w in extra_widths],
        out_shape=[jax.ShapeDtypeStruct((bsz, seq, vw), BF16),
                   jax.ShapeDtypeStruct((bsz, N_HEADS, HEAD_DK, dv), F32)]
        + [jax.ShapeDtypeStruct((bsz, seq, w), F32) for w in extra_widths],
        scratch_shapes=scratch,
        compiler_params=_cparams("arbitrary", "arbitrary"),
        name="gla_" + mode,
    )(*args)


def _s5_prep_kernel(lre_ref, lim_ref, ls_ref, lre_x_ref, lim_x_ref, ls_x_ref, bre_ref, bim_ref,
                    are_ref, aim_ref, bbre_ref, bbim_ref):
    def disc(lre, lim, ls):
        lr = jnp.minimum(lre, S5_MAX_RE)
        dt = jnp.exp(ls)
        mag = jnp.exp(lr * dt)
        a_re = mag * jnp.cos(lim * dt)
        a_im = mag * jnp.sin(lim * dt)
        den = lr * lr + lim * lim
        z_re = ((a_re - 1.0) * lr + a_im * lim) / den
        z_im = (a_im * lr - (a_re - 1.0) * lim) / den
        return a_re, a_im, z_re, z_im

    a_re, a_im, _, _ = disc(lre_ref[...], lim_ref[...], ls_ref[...])
    are_ref[...] = a_re
    aim_ref[...] = a_im
    _, _, z_re, z_im = disc(lre_x_ref[...], lim_x_ref[...], ls_x_ref[...])
    bbre_ref[...] = z_re * bre_ref[...] - z_im * bim_ref[...]
    bbim_ref[...] = z_re * bim_ref[...] + z_im * bre_ref[...]


def _s5_prep(lam_re, lam_im, log_step, b_re, b_im):
    g, n = lam_re.shape
    p = b_re.shape[-1]
    ls = jnp.broadcast_to(log_step[:, None], (g, n))
    rep = lambda a: jnp.repeat(a, p, axis=1)
    outs = pl.pallas_call(
        _s5_prep_kernel,
        out_shape=[jax.ShapeDtypeStruct((g, n), F32)] * 2 + [jax.ShapeDtypeStruct((g, n * p), F32)] * 2,
        name="s5_prep",
    )(lam_re, lam_im, ls, rep(lam_re), rep(lam_im), rep(ls),
      b_re.reshape(g, n * p), b_im.reshape(g, n * p))
    a_re, a_im, bb_re, bb_im = outs
    return a_re, a_im, bb_re.reshape(g, n, p), bb_im.reshape(g, n, p)


def _s5_kernel(*refs, bg, tc, nt, cw, has_x0):
    n_in = 9 if has_x0 else 7
    u_ref, bm_ref, cm_ref, a_ref, d_ref, wg_ref, bgl_ref = refs[:7]
    o_ref, sre_ref, sim_ref = refs[n_in:n_in + 3]
    utm_ref, xs_ref, st_ref = refs[n_in + 3:]
    t_idx = pl.program_id(1)
    half = xs_ref.shape[1] // 2
    uw = u_ref.shape[-1] // 2

    @pl.when(t_idx == 0)
    def _():
        for hf in range(2):
            if has_x0:
                st_ref[:, hf * 2 * half:hf * 2 * half + half] = refs[7][:, hf * half:(hf + 1) * half]
                st_ref[:, hf * 2 * half + half:(hf + 1) * 2 * half] = refs[8][:, hf * half:(hf + 1) * half]
            else:
                st_ref[...] = jnp.zeros(st_ref.shape, F32)

    for t in range(tc):
        utm_ref[t * bg:(t + 1) * bg, :] = u_ref[:, t, :]
    u = utm_ref[...]
    ys = []
    for hf in range(2):
        xs_ref[...] = jnp.dot(u[:, hf * uw:(hf + 1) * uw].astype(BF16), bm_ref[hf],
                              preferred_element_type=F32)
        base = hf * 2 * half
        for c0 in range(0, half, cw):
            ar = a_ref[0:1, base + c0:base + c0 + cw]
            ai = a_ref[0:1, base + half + c0:base + half + c0 + cw]
            xr = st_ref[:, base + c0:base + c0 + cw]
            xi = st_ref[:, base + half + c0:base + half + c0 + cw]
            for t in range(tc):
                rows = slice(t * bg, (t + 1) * bg)
                nr = ar * xr - ai * xi + xs_ref[rows, c0:c0 + cw]
                ni = ar * xi + ai * xr + xs_ref[rows, half + c0:half + c0 + cw]
                xs_ref[rows, c0:c0 + cw] = nr
                xs_ref[rows, half + c0:half + c0 + cw] = ni
                xr, xi = nr, ni
            st_ref[:, base + c0:base + c0 + cw] = xr
            st_ref[:, base + half + c0:base + half + c0 + cw] = xi
        ys.append(jnp.dot(xs_ref[...].astype(BF16), cm_ref[hf], preferred_element_type=F32))
    y = jnp.concatenate(ys, axis=1) + d_ref[...] * u
    y = 0.5 * y * (1.0 + jnp.tanh(math.sqrt(2.0 / math.pi) * (y + 0.044715 * (y * y * y))))
    gate = jnp.dot(y.astype(BF16), wg_ref[...], preferred_element_type=F32) + bgl_ref[...]
    utm_ref[...] = y * _sigmoid(gate)
    for t in range(tc):
        o_ref[:, t, :] = utm_ref[t * bg:(t + 1) * bg, :]

    @pl.when(t_idx == nt - 1)
    def _():
        for hf in range(2):
            sre_ref[:, hf * half:(hf + 1) * half] = st_ref[:, hf * 2 * half:hf * 2 * half + half]
            sim_ref[:, hf * half:(hf + 1) * half] = st_ref[:, hf * 2 * half + half:(hf + 1) * 2 * half]


def _s5_call(u_src, col_block, mats, x0, *, tc, cw):
    bg, seq, _ = u_src.shape
    assert seq % tc == 0
    nt = seq // tc
    bmat, cmat, a_flat, d_row, w_glu, b_glu = mats
    width = d_row.shape[1]
    nstate = a_flat.shape[1] // 2
    whole = lambda a: pl.BlockSpec(a.shape, lambda g, t: (0,) * a.ndim)
    in_specs = [pl.BlockSpec((bg, tc, width), lambda g, t: (0, t, col_block))]
    in_specs += [whole(m) for m in mats]
    args = [u_src] + list(mats)
    st_spec = pl.BlockSpec((bg, nstate), lambda g, t: (0, 0))
    if x0 is not None:
        in_specs += [st_spec, st_spec]
        args += list(x0)
    scratch = [pltpu.VMEM((bg * tc, width), F32), pltpu.VMEM((bg * tc, nstate), F32),
               pltpu.VMEM((bg, 2 * nstate), F32)]
    return pl.pallas_call(
        functools.partial(_s5_kernel, bg=bg, tc=tc, nt=nt, cw=cw, has_x0=x0 is not None),
        grid=(1, nt),
        in_specs=in_specs,
        out_specs=[pl.BlockSpec((bg, tc, width), lambda g, t: (0, t, 0)), st_spec, st_spec],
        out_shape=[jax.ShapeDtypeStruct((bg, seq, width), F32),
                   jax.ShapeDtypeStruct((bg, nstate), F32),
                   jax.ShapeDtypeStruct((bg, nstate), F32)],
        scratch_shapes=scratch,
        compiler_params=_cparams("arbitrary", "arbitrary"),
        name="s5",
    )(*args)


def _s5_matrices(a_re, a_im, bb_re, bb_im, c_re, c_im, d, w_glu, b_glu):
    g, n, p = bb_re.shape
    gh = g // 2
    eye = jnp.eye(gh, dtype=F32)

    def block_diag(t):
        return (eye[:, None, :, None] * t[:, :, None, :]).reshape(gh * t.shape[1], gh * t.shape[2])

    def in_mat(bb):
        return block_diag(bb.transpose(0, 2, 1))

    def out_mat(cc):
        return block_diag(cc.transpose(0, 2, 1))

    bmat = jnp.stack([jnp.concatenate([in_mat(bb_re[h * gh:(h + 1) * gh]),
                                       in_mat(bb_im[h * gh:(h + 1) * gh])], axis=1)
                      for h in range(2)]).astype(BF16)
    cmat = jnp.stack([jnp.concatenate([out_mat(c_re[h * gh:(h + 1) * gh]),
                                       out_mat(-c_im[h * gh:(h + 1) * gh])], axis=0)
                      for h in range(2)]).astype(BF16)
    a_flat = jnp.concatenate([jnp.concatenate([a_re[h * gh:(h + 1) * gh].reshape(1, gh * n),
                                               a_im[h * gh:(h + 1) * gh].reshape(1, gh * n)], axis=1)
                              for h in range(2)], axis=1)
    return (bmat, cmat, a_flat, d.reshape(1, g * p), w_glu.astype(BF16), b_glu.reshape(1, -1))


def _xattn_kernel(*refs, n_pre):
    x_ref = refs[0]
    g_ref, wq_ref, wo_ref, k_ref, v_ref, y_ref, kt_ref, vt_ref = refs[1 + 2 * n_pre:]

    @pl.when(pl.program_id(1) == 0)
    def _():
        kt_ref[...] = jnp.transpose(k_ref[...], (1, 0, 2)).astype(BF16)
        vt_ref[...] = jnp.transpose(v_ref[...], (1, 0, 2)).astype(BF16)

    x = x_ref[...]
    for a_ref, w_ref in zip(refs[1:1 + n_pre], refs[1 + n_pre:1 + 2 * n_pre]):
        x = x + jnp.dot(a_ref[...].astype(BF16), w_ref[...], preferred_element_type=F32)
    q = jnp.dot(_rms(x, g_ref[...]).astype(BF16), wq_ref[...], preferred_element_type=F32).astype(BF16)
    hd = q.shape[1] // N_HEADS
    outs = []
    for h in range(N_HEADS):
        s = lax.dot_general(q[:, h * hd:(h + 1) * hd], kt_ref[h], _NT,
                            preferred_element_type=F32) * (hd ** -0.5)
        p = jnp.exp(s - jnp.max(s, axis=-1, keepdims=True))
        p = p / jnp.sum(p, axis=-1, keepdims=True)
        outs.append(jnp.dot(p.astype(BF16), vt_ref[h], preferred_element_type=F32).astype(BF16))
    y_ref[...] = x + jnp.dot(jnp.concatenate(outs, axis=1), wo_ref[...], preferred_element_type=F32)


def _xattn_call(x, pre_a, pre_w, g, w_q, w_o, mem_k, mem_v, layer, *, tq):
    bsz, seq, d = x.shape
    n_mem, nh, hd = mem_k.shape[2:]
    assert seq % tq == 0
    kv_spec = pl.BlockSpec((None, None, n_mem, nh, hd), lambda b, t: (layer, b, 0, 0, 0))
    rows = lambda a: pl.BlockSpec((None, tq, a.shape[-1]), lambda b, t: (b, t, 0))
    params = [_whole(w) for w in (*pre_w, g, w_q, w_o)]
    return pl.pallas_call(
        functools.partial(_xattn_kernel, n_pre=len(pre_a)),
        grid=(bsz, seq // tq),
        in_specs=[rows(x)] + [rows(a) for a in pre_a] + [spec for _, spec in params] + [kv_spec, kv_spec],
        out_specs=rows(x),
        out_shape=jax.ShapeDtypeStruct((bsz, seq, d), F32),
        scratch_shapes=[pltpu.VMEM((nh, n_mem, hd), BF16), pltpu.VMEM((nh, n_mem, hd), BF16)],
        compiler_params=_cparams("arbitrary", "arbitrary"),
        name="xattn",
    )(x, *pre_a, *[w for w, _ in params], mem_k, mem_v)


KV_SLOTS = 3


def _attn_rows_kernel(q_ref, k_hbm, v_hbm, o_ref, pad_ref, kbuf, vbuf, sem, *, rows_in, layer):
    nb, rows, d = pad_ref.shape
    step, n_steps = pl.program_id(0), pl.num_programs(0)

    def kv_copies(at_step, slot):
        src = pl.ds(at_step * nb, nb)
        return (pltpu.make_async_copy(k_hbm.at[layer, src], kbuf.at[slot], sem.at[0, slot]),
                pltpu.make_async_copy(v_hbm.at[layer, src], vbuf.at[slot], sem.at[1, slot]))

    @pl.when(step == 0)
    def _():
        for ahead in range(KV_SLOTS - 1):
            @pl.when(ahead < n_steps)
            def _():
                for cp in kv_copies(ahead, ahead):
                    cp.start()

    slot = step % KV_SLOTS
    for cp in kv_copies(step, slot):
        cp.wait()

    @pl.when(step + KV_SLOTS - 1 < n_steps)
    def _():
        for cp in kv_copies(step + KV_SLOTS - 1, (step + KV_SLOTS - 1) % KV_SLOTS):
            cp.start()

    k_ref, v_ref = kbuf.at[slot], vbuf.at[slot]
    nblk = d // 128
    half_blk = nblk // 2
    lanes = k_ref.shape[1]
    lane_blk = lax.broadcasted_iota(jnp.int32, (1, lanes), 1) % nblk
    row_head = lax.broadcasted_iota(jnp.int32, (N_HEADS * rows, 1), 0) // rows
    live = lane_blk == row_head
    scale = (d // N_HEADS) ** -0.5
    for i in range(nb):
        pad_ref[i] = jnp.zeros((rows, d), F32)
        pad_ref[i, 0:rows_in, :] = q_ref[i]
        q = pad_ref[i]
        qx = jnp.concatenate([q[:, j * 128:(j + 1) * 128] for j in range(nblk)], axis=0)
        g = lax.dot_general(qx.astype(BF16), k_ref[i].astype(BF16), _NT, preferred_element_type=F32)
        s = jnp.concatenate(
            [g[2 * h * rows:(2 * h + 1) * rows]
             + pltpu.roll(g[(2 * h + 1) * rows:(2 * h + 2) * rows], lanes - half_blk, axis=1)
             for h in range(N_HEADS)], axis=0) * scale
        s = jnp.where(live, s, -1e30)
        e = jnp.exp(s - jnp.max(s, axis=-1, keepdims=True))
        p = e / jnp.sum(e, axis=-1, keepdims=True)
        px = jnp.concatenate(
            [blk for h in range(N_HEADS)
             for blk in (p[h * rows:(h + 1) * rows], pltpu.roll(p[h * rows:(h + 1) * rows], half_blk, axis=1))],
            axis=0)
        o = jnp.dot(px.astype(BF16), v_ref[i].astype(BF16), preferred_element_type=F32)
        o = jnp.concatenate([o[j * rows:(j + 1) * rows] for j in range(nblk)], axis=1)
        o_ref[i] = o[0:rows_in].astype(o_ref.dtype)


def _attn_rows_call(q, mem_k, mem_v, layer, *, nb):
    bsz, seq, d = q.shape
    depth, _, n_mem, nh, hd = mem_k.shape
    assert bsz % nb == 0 and hd == 256 and nh == N_HEADS and seq <= 16
    as_rows = lambda a: a.reshape(depth, bsz, n_mem, nh, 2, 128).transpose(0, 1, 2, 4, 3, 5).reshape(
        depth, bsz, n_mem * 2 * nh, 128)
    kv_rows = n_mem * 2 * nh
    kv_spec = pl.BlockSpec(memory_space=pl.ANY)
    return pl.pallas_call(
        functools.partial(_attn_rows_kernel, rows_in=seq, layer=layer),
        grid=(bsz // nb,),
        in_specs=[pl.BlockSpec((nb, seq, d), lambda b: (b, 0, 0)), kv_spec, kv_spec],
        out_specs=pl.BlockSpec((nb, seq, d), lambda b: (b, 0, 0)),
        out_shape=jax.ShapeDtypeStruct((bsz, seq, d), BF16),
        scratch_shapes=[pltpu.VMEM((nb, 8 * pl.cdiv(seq, 8), d), F32),
                        pltpu.VMEM((KV_SLOTS, nb, kv_rows, 128), F32),
                        pltpu.VMEM((KV_SLOTS, nb, kv_rows, 128), F32),
                        pltpu.SemaphoreType.DMA((2, KV_SLOTS))],
        compiler_params=_cparams("arbitrary"),
        name="mem_attn_rows",
    )(q, as_rows(mem_k), as_rows(mem_v))


FFN_COLS = 256


def _ffn_kernel(*refs, tm, ts, hs, f_dim, has_hist, has_final):
    n_in = 6 + int(has_hist) + int(has_final)
    x_ref, g_ref, wup_ref, cw_ref, cb_ref, wdn_ref = refs[:6]
    y_ref, state_ref = refs[n_in:n_in + 2]
    gated_ref, hist_ref = refs[n_in + 2:n_in + 4]
    t_idx = pl.program_id(1)

    @pl.when(t_idx == 0)
    def _():
        hist_ref[...] = jnp.zeros(hist_ref.shape, F32)
        if has_hist:
            hist_ref[hs - 2 * ts:hs, :] = refs[6][...]

    x = x_ref[...]
    h = _rms(x, g_ref[...]).astype(BF16)
    row = lax.broadcasted_iota(jnp.int32, (tm, 1), 0)
    for c in range(f_dim // FFN_COLS):
        conv = []
        for part in range(2):
            cols = slice(part * f_dim + c * FFN_COLS, part * f_dim + (c + 1) * FFN_COLS)
            u = jnp.dot(h, wup_ref[:, cols], preferred_element_type=F32)
            if ts == 1:
                prev2, prev1 = hist_ref[hs - 2:hs - 1, cols], hist_ref[hs - 1:hs, cols]
                m1 = jnp.where(row == 0, prev1, pltpu.roll(u, 1, axis=0))
                m2 = jnp.where(row == 0, prev2, jnp.where(row == 1, prev1, pltpu.roll(u, 2, axis=0)))
            else:
                ext = jnp.concatenate([hist_ref[hs - 2 * ts:hs, cols], u], axis=0)
                m2, m1 = ext[0:tm], ext[ts:ts + tm]
            conv.append(cb_ref[:, cols] + cw_ref[0:1, cols] * m2 + cw_ref[1:2, cols] * m1
                        + cw_ref[2:3, cols] * u)
            hist_ref[:, cols] = u[tm - hs:tm]
        gated_ref[:, c * FFN_COLS:(c + 1) * FFN_COLS] = (_silu(conv[0]) * conv[1]).astype(BF16)
    out = x + jnp.dot(gated_ref[...], wdn_ref[...], preferred_element_type=F32)
    if has_final:
        out = _rms(out, refs[n_in - 1][...])
    y_ref[...] = out

    @pl.when(t_idx == pl.num_programs(1) - 1)
    def _():
        state_ref[...] = hist_ref[hs - 2 * ts:hs, :]


def _ffn_call(x, g, w_up, conv_w, conv_b, w_down, hist0, g_final, *, tm, ts):
    ngrp, rows, d = x.shape
    params = [_whole(a) for a in (g, w_up, conv_w, conv_b, w_down)]
    f2 = params[1][0].shape[-1]
    f_dim = f2 // 2
    hs = max(8, 2 * ts)
    assert rows % tm == 0 and tm >= hs and f_dim % FFN_COLS == 0 and (ts == 1 or ts % 8 == 0)
    args = [x] + [a for a, _ in params]
    in_specs = [pl.BlockSpec((None, tm, d), lambda s, t: (s, t, 0))] + [spec for _, spec in params]
    st_spec = pl.BlockSpec((None, 2 * ts, f2), lambda s, t: (s, 0, 0))
    if hist0 is not None:
        in_specs.append(st_spec)
        args.append(hist0)
    if g_final is not None:
        in_specs.append(_whole(g_final)[1])
        args.append(g_final)
    return pl.pallas_call(
        functools.partial(_ffn_kernel, tm=tm, ts=ts, hs=hs, f_dim=f_dim,
                          has_hist=hist0 is not None, has_final=g_final is not None),
        grid=(ngrp, rows // tm),
        in_specs=in_specs,
        out_specs=[pl.BlockSpec((None, tm, d), lambda s, t: (s, t, 0)), st_spec],
        out_shape=[jax.ShapeDtypeStruct((ngrp, rows, d), F32),
                   jax.ShapeDtypeStruct((ngrp, 2 * ts, f2), F32)],
        scratch_shapes=[pltpu.VMEM((tm, f_dim), BF16), pltpu.VMEM((hs, f2), F32)],
        compiler_params=_cparams("arbitrary", "arbitrary"),
        name="conv_ffn",
    )(*args)


def _trunk(x, mem_k, mem_v, states, p, *, prompt):
    bsz, seq, d = x.shape
    depth = p["norm_mix"].shape[0]
    m = bsz * seq
    x2 = x.reshape(m, d)
    new = {"hgrn": [], "s5_re": [], "s5_im": [], "gla": [], "conv": []}
    if prompt:
        gla_tiles = dict(tb=1024, chunk=64, nb=1)
        attn_tq = 1024
    else:
        gla_tiles = dict(tb=seq, chunk=16, nb=8)
        attn_tq = seq
    for l in range(depth):
        g_mix = _pick(p["norm_mix"], l)
        if l % 2 == 0:
            e = l // 2
            kw = N_HEADS * HEAD_DK
            x3 = x2.reshape(bsz, seq, d)
            hgrn_par = [p["hgrn_lb"], p["hgrn_gnorm"][e].reshape(1, -1)]
            mats = p["s5_mats"][e]
            if prompt:
                o_a, s_a, u = _gla_call("hgrn", (x3, g_mix, _pick(p["w_in_ab"], e)), hgrn_par, None,
                                        layer=l, dv=kw // N_HEADS, extra_widths=(kw,), **gla_tiles)
                o_b, sr, si = _s5_call(u, 0, mats, None, tc=128, cw=512)
            else:
                proj = _norm_proj(x2, g_mix, _pick(p["w_in_ab"], e)).reshape(bsz, seq, -1)
                o_a, s_a = _gla_call("hgrn", proj, hgrn_par, states["hgrn"][e],
                                     layer=l, dv=kw // N_HEADS, **gla_tiles)
                x0 = (states["s5_re"][e].reshape(bsz, -1), states["s5_im"][e].reshape(bsz, -1))
                o_b, sr, si = _s5_call(proj, 4, mats, x0, tc=seq, cw=128)
            mixed = [o_a, o_b]
            mixed_w = [_pick(p["w_out_ab"], e, rows=kw, row_block=0), _pick(p["w_out_ab"], e, rows=kw, row_block=1)]
            new["hgrn"].append(s_a)
            new["s5_re"].append(sr.reshape(bsz, -1, S5_STATE))
            new["s5_im"].append(si.reshape(bsz, -1, S5_STATE))
        else:
            o_idx = l // 2
            if prompt:
                src_c = (x2.reshape(bsz, seq, d), g_mix, _pick(p["w_in_c"], o_idx))
            else:
                src_c = _norm_proj(x2, g_mix, _pick(p["w_in_c"], o_idx)).reshape(bsz, seq, -1)
            o_c, s_c = _gla_call(
                "gla", src_c,
                [p["gla_w_gate"][o_idx], p["gla_b_gate"][o_idx].reshape(1, -1),
                 p["gla_gnorm"][o_idx].reshape(1, -1)],
                None if states is None else states["gla"][o_idx],
                layer=l, dv=d // N_HEADS, **{**gla_tiles, **(dict(chunk=256) if prompt else {})})
            mixed, mixed_w = [o_c], [_pick(p["w_out_c"], o_idx)]
            new["gla"].append(s_c)
        g_cross, w_q, w_o = _pick(p["norm_cross"], l), _pick(p["xa_w_q"], l), _pick(p["xa_w_o"], l)
        if prompt:
            x2 = _xattn_call(x2.reshape(bsz, seq, d), mixed, mixed_w, g_cross, w_q, w_o,
                             mem_k, mem_v, l, tq=attn_tq).reshape(m, d)
        else:
            x2 = _proj_res(x2, [a.reshape(m, -1) for a in mixed], mixed_w)
            q = _norm_proj(x2, g_cross, w_q)
            o_x = _attn_rows_call(q.reshape(bsz, seq, d), mem_k, mem_v, l, nb=4)
            x2 = _proj_res(x2, [o_x.reshape(m, d)], [w_o])
        g_final = p["norm_final"] if l == depth - 1 else None
        ffn_w = tuple(_pick(p[name], l) for name in
                      ("norm_ffn", "ffn_w_up", "ffn_conv_w", "ffn_conv_b", "ffn_w_down"))
        if prompt:
            y, cst = _ffn_call(x2.reshape(bsz, seq, d), *ffn_w, None, g_final, tm=1024, ts=1)
            x2 = y.reshape(m, d)
        else:
            xt = x2.reshape(bsz, seq, d).transpose(1, 0, 2).reshape(1, m, d)
            hist0 = states["conv"][l].transpose(1, 0, 2).reshape(1, 2 * bsz, -1)
            y, cst = _ffn_call(xt, *ffn_w, hist0, g_final, tm=m, ts=bsz)
            x2 = y.reshape(seq, bsz, d).transpose(1, 0, 2).reshape(m, d)
            cst = cst.reshape(2, bsz, -1).transpose(1, 0, 2)
        new["conv"].append(cst)
    return x2.reshape(bsz, seq, d), new


def kernel(x_prompt, x_sample, mem_prompt, cache_mem_k, cache_mem_v, state_hgrn, state_s5_re, state_s5_im, state_gla, state_ffn_conv, norm_mix, norm_cross, norm_mem, norm_ffn, norm_final, w_in_ab, hgrn_lb, hgrn_gnorm, s5_lam_re, s5_lam_im, s5_log_step, s5_b_re, s5_b_im, s5_c_re, s5_c_im, s5_d, s5_w_glu, s5_b_glu, w_out_ab, w_in_c, gla_w_gate_up, gla_b_gate, gla_gnorm, w_out_c, xa_w_q, xa_w_kv, xa_w_o, ffn_w_up, ffn_conv_w, ffn_conv_b, ffn_w_down):
    depth, d = norm_mix.shape

    gla_cols = w_in_c.shape[2]
    gate_rank = gla_w_gate_up.shape[1]
    pad_c = (-gla_cols) % 128
    w_in_c_p = jnp.pad(w_in_c, ((0, 0), (0, 0), (0, pad_c))).astype(BF16)
    gla_w_gate = jnp.pad(gla_w_gate_up, ((0, 0), (0, 128 - gate_rank), (0, 0))).astype(BF16)

    s5_mats = []
    for e in range(s5_lam_re.shape[0]):
        a_re, a_im, bb_re, bb_im = _s5_prep(s5_lam_re[e], s5_lam_im[e], s5_log_step[e],
                                            s5_b_re[e], s5_b_im[e])
        s5_mats.append(_s5_matrices(a_re, a_im, bb_re, bb_im, s5_c_re[e], s5_c_im[e], s5_d[e],
                                    s5_w_glu[e], s5_b_glu[e]))

    row = lambda a: a.reshape(a.shape[0], 1, a.shape[1])
    p = dict(norm_mix=row(norm_mix), norm_cross=row(norm_cross), norm_ffn=row(norm_ffn),
             norm_final=norm_final.reshape(1, d),
             w_in_ab=w_in_ab.astype(BF16), hgrn_lb=hgrn_lb, hgrn_gnorm=hgrn_gnorm, s5_mats=s5_mats,
             w_out_ab=w_out_ab.astype(BF16), w_in_c=w_in_c_p, gla_w_gate=gla_w_gate,
             gla_b_gate=gla_b_gate, gla_gnorm=gla_gnorm, w_out_c=w_out_c.astype(BF16),
             xa_w_q=xa_w_q.astype(BF16), xa_w_o=xa_w_o.astype(BF16),
             ffn_w_up=ffn_w_up.astype(BF16), ffn_conv_w=ffn_conv_w, ffn_conv_b=row(ffn_conv_b),
             ffn_w_down=ffn_w_down.astype(BF16))

    mem_k_p, mem_v_p = _mem_kv(mem_prompt, norm_mem.reshape(depth, 1, d), xa_w_kv.astype(BF16))
    y_prompt, st_p = _trunk(x_prompt, mem_k_p, mem_v_p, None, p, prompt=True)

    states = dict(hgrn=state_hgrn, s5_re=state_s5_re, s5_im=state_s5_im, gla=state_gla,
                  conv=state_ffn_conv)
    y_sample, st_s = _trunk(x_sample, cache_mem_k, cache_mem_v, states, p, prompt=False)

    stack = lambda xs: xs[0][None] if len(xs) == 1 else jnp.stack(xs)
    return (y_prompt, y_sample,
            stack(st_p["hgrn"]), stack(st_p["s5_re"]), stack(st_p["s5_im"]), stack(st_p["gla"]),
            mem_k_p, mem_v_p, stack(st_p["conv"]),
            stack(st_s["hgrn"]), stack(st_s["s5_re"]), stack(st_s["s5_im"]), stack(st_s["gla"]),
            stack(st_s["conv"]))
```

```python
import functools
import math

import jax
import jax.numpy as jnp
from jax import lax
from jax.experimental import pallas as pl
from jax.experimental.pallas import tpu as pltpu

F32 = jnp.float32
BF16 = jnp.bfloat16

EPS = 1e-6
S5_MAX_RE = -1e-4
GLA_GATE_TAU = 16.0
N_HEADS = 4
HEAD_DK = 128
S5_GROUP = 16
S5_STATE = 64
SUB_BLOCK = 16
GLA_SAFE_DECAY = 64.0
VMEM_LIMIT = 56 * 1024 * 1024

_NT = (((1,), (1,)), ((), ()))
_TN = (((0,), (0,)), ((), ()))


def _cparams(*sem):
    return pltpu.CompilerParams(dimension_semantics=sem, vmem_limit_bytes=VMEM_LIMIT)


def _rms(x, g):
    return x * lax.rsqrt(jnp.mean(x * x, axis=-1, keepdims=True) + EPS) * g


def _sigmoid(x):
    return 1.0 / (1.0 + jnp.exp(-x))


def _silu(x):
    return x * _sigmoid(x)


def _row_tile(rows, want):
    t = min(rows, want)
    assert rows % t == 0, (rows, t)
    return t


def _pick(a, layer, rows=None, row_block=0, cols=None, col_block=0):
    block = (None, rows or a.shape[1], cols or a.shape[2])
    return a, pl.BlockSpec(block, lambda *_: (layer, row_block, col_block))


def _whole(a):
    return a if isinstance(a, tuple) else (a, pl.BlockSpec(a.shape, lambda *_: (0,) * a.ndim))


def _norm_proj_kernel(x_ref, g_ref, w_ref, o_ref):
    h = _rms(x_ref[...], g_ref[...]).astype(BF16)
    n = o_ref.shape[1]
    for c0 in range(0, n, 512):
        cw = min(512, n - c0)
        o_ref[:, c0:c0 + cw] = jnp.dot(h, w_ref[:, c0:c0 + cw], preferred_element_type=F32)


def _norm_proj(x, g, w, *, tm=512):
    m, k = x.shape
    (g, g_spec), (w, w_spec) = _whole(g), _whole(w)
    n = w.shape[-1]
    tm = _row_tile(m, tm)
    return pl.pallas_call(
        _norm_proj_kernel,
        grid=(m // tm,),
        in_specs=[pl.BlockSpec((tm, k), lambda i: (i, 0)), g_spec, w_spec],
        out_specs=pl.BlockSpec((tm, n), lambda i: (i, 0)),
        out_shape=jax.ShapeDtypeStruct((m, n), F32),
        compiler_params=_cparams("arbitrary"),
        name="norm_proj",
    )(x, g, w)


def _mem_kv_kernel(x_ref, g_ref, w_ref, k_ref, v_ref):
    nb, n_mem, d = x_ref.shape
    h = _rms(x_ref[...].reshape(nb * n_mem, d), g_ref[...]).astype(BF16)
    hd = d // N_HEADS
    for o_ref, base in ((k_ref, 0), (v_ref, d)):
        for hh in range(N_HEADS):
            kv = jnp.dot(h, w_ref[:, base + hh * hd:base + (hh + 1) * hd], preferred_element_type=F32)
            for i in range(nb):
                o_ref[i, :, hh, :] = kv[i * n_mem:(i + 1) * n_mem]


def _mem_kv(mem, g, w, *, nb=2):
    bsz, n_mem, d = mem.shape
    depth = w.shape[0]
    assert bsz % nb == 0
    out_spec = pl.BlockSpec((None, nb, n_mem, N_HEADS, d // N_HEADS), lambda l, b: (l, b, 0, 0, 0))
    out_shape = jax.ShapeDtypeStruct((depth, bsz, n_mem, N_HEADS, d // N_HEADS), F32)
    return pl.pallas_call(
        _mem_kv_kernel,
        grid=(depth, bsz // nb),
        in_specs=[pl.BlockSpec((nb, n_mem, d), lambda l, b: (b, 0, 0)),
                  pl.BlockSpec((None, 1, d), lambda l, b: (l, 0, 0)),
                  pl.BlockSpec((None, d, 2 * d), lambda l, b: (l, 0, 0))],
        out_specs=[out_spec, out_spec],
        out_shape=[out_shape, out_shape],
        compiler_params=_cparams("arbitrary", "arbitrary"),
        name="mem_kv",
    )(mem, g, w)


def _proj_res_kernel(res_ref, *refs, n_in):
    acc = res_ref[...]
    for a_ref, w_ref in zip(refs[:n_in], refs[n_in:2 * n_in]):
        acc = acc + jnp.dot(a_ref[...].astype(BF16), w_ref[...], preferred_element_type=F32)
    refs[2 * n_in][...] = acc


def _proj_res(res, a_list, w_list, *, tm=512):
    m, n = res.shape
    tm = _row_tile(m, tm)
    n_in = len(a_list)
    w_list = [_whole(w) for w in w_list]
    in_specs = [pl.BlockSpec((tm, n), lambda i: (i, 0))]
    in_specs += [pl.BlockSpec((tm, a.shape[1]), lambda i: (i, 0)) for a in a_list]
    in_specs += [spec for _, spec in w_list]
    return pl.pallas_call(
        functools.partial(_proj_res_kernel, n_in=n_in),
        grid=(m // tm,),
        in_specs=in_specs,
        out_specs=pl.BlockSpec((tm, n), lambda i: (i, 0)),
        out_shape=jax.ShapeDtypeStruct((m, n), F32),
        compiler_params=_cparams("arbitrary"),
        name="proj_res",
    )(res, *a_list, *[w for w, _ in w_list])


def _res_norm_proj_kernel(res_ref, *refs, n_in):
    x = res_ref[...]
    for a_ref, w_ref in zip(refs[:n_in], refs[n_in:2 * n_in]):
        x = x + jnp.dot(a_ref[...].astype(BF16), w_ref[...], preferred_element_type=F32)
    g_ref, wq_ref, x_out, q_out = refs[2 * n_in:]
    x_out[...] = x
    q_out[...] = jnp.dot(_rms(x, g_ref[...]).astype(BF16), wq_ref[...], preferred_element_type=F32)


def _res_norm_proj(res, a_list, w_list, g, w_q, *, tm=512):
    m, n = res.shape
    tm = _row_tile(m, tm)
    params = [_whole(w) for w in (*w_list, g, w_q)]
    nq = params[-1][0].shape[-1]
    rows = lambda width: pl.BlockSpec((tm, width), lambda i: (i, 0))
    return pl.pallas_call(
        functools.partial(_res_norm_proj_kernel, n_in=len(a_list)),
        grid=(m // tm,),
        in_specs=[rows(n)] + [rows(a.shape[1]) for a in a_list] + [spec for _, spec in params],
        out_specs=[rows(n), rows(nq)],
        out_shape=[jax.ShapeDtypeStruct((m, n), F32), jax.ShapeDtypeStruct((m, nq), F32)],
        compiler_params=_cparams("arbitrary"),
        name="res_norm_proj",
    )(res, *a_list, *[w for w, _ in params])


def _cumsum_rows(x, c):
    hi = x.astype(BF16)
    rest = x - hi.astype(F32)
    mid = rest.astype(BF16)
    lo = (rest - mid.astype(F32)).astype(BF16)
    n = max(c, min(x.shape[0], 128))
    r = lax.broadcasted_iota(jnp.int32, (n, n), 0)
    col = lax.broadcasted_iota(jnp.int32, (n, n), 1)
    tri = jnp.where((r >= col) & (r // c == col // c), 1.0, 0.0).astype(BF16)
    tri3 = jnp.concatenate([tri, tri, tri], axis=1)
    out = [jnp.dot(tri3, jnp.concatenate([hi[s:s + n], mid[s:s + n], lo[s:s + n]], axis=0),
                   preferred_element_type=F32) for s in range(0, x.shape[0], n)]
    return out[0] if len(out) == 1 else jnp.concatenate(out, axis=0)


def _gla_head(qh, kh, bh, vh, st_ref, small_decay, live_rows):
    c, dk = qh.shape
    dv = vh.shape[1]
    sb = min(SUB_BLOCK, c)
    vb = vh.astype(BF16)
    st = st_ref[...]
    b_last = bh[c - 1:c, :]
    q_in = (qh * jnp.exp(bh)).astype(BF16)
    o_inter = jnp.dot(q_in, st.astype(BF16), preferred_element_type=F32)
    decay = jnp.transpose(jnp.broadcast_to(jnp.exp(b_last), (dk, dk)))
    decay = decay if dv == dk else jnp.concatenate([decay] * (dv // dk), axis=1)
    if small_decay:
        k_up = kh * jnp.exp(-bh)
        a = lax.dot_general(q_in, k_up.astype(BF16), _NT, preferred_element_type=F32)
        causal = (lax.broadcasted_iota(jnp.int32, (c, c), 0) >= lax.broadcasted_iota(jnp.int32, (c, c), 1))
        kd = (k_up * jnp.exp(b_last)).astype(BF16)
        st_ref[...] = st * decay + lax.dot_general(kd, vb, _TN, preferred_element_type=F32)
        return o_inter + jnp.dot(jnp.where(causal, a, 0.0).astype(BF16), vb, preferred_element_type=F32)
    kd = (kh * jnp.exp(b_last - bh)).astype(BF16)
    st_ref[...] = st * decay + lax.dot_general(kd, vb, _TN, preferred_element_type=F32)
    rows = lax.broadcasted_iota(jnp.int32, (sb, 1), 0)
    parts = []
    for s in range(c // sb):
        r0 = s * sb
        qs, ks, bs, vs = qh[r0:r0 + sb], kh[r0:r0 + sb], bh[r0:r0 + sb], vh[r0:r0 + sb]
        acc = o_inter[r0:r0 + sb]
        if s > 0:
            ref_b = bh[r0 - 1:r0, :]
            qf = (qs * jnp.exp(bs - ref_b)).astype(BF16)
            kf = (kh[0:r0] * jnp.exp(ref_b - bh[0:r0])).astype(BF16)
            a_off = lax.dot_general(qf, kf, _NT, preferred_element_type=F32)
            acc = acc + jnp.dot(a_off.astype(BF16), vb[0:r0], preferred_element_type=F32)
        for j in range(max(0, min(sb, live_rows - r0))):
            w = jnp.exp(jnp.minimum(bs - bs[j:j + 1], 0.0)) * qs * ks[j:j + 1]
            col = jnp.where(rows >= j, jnp.sum(w, axis=-1, keepdims=True), 0.0)
            acc = acc + col * vs[j:j + 1]
        parts.append(acc)
    return parts[0] if len(parts) == 1 else jnp.concatenate(parts, axis=0)


def _gla_kernel(*refs, mode, layer, dv, nb, rows_in, chunk, n_chunks, has_s0, in_widths):
    n_act = 4 if mode == "hgrn" else 5
    n_src = 3 if in_widths else n_act
    n_par = 2 if mode == "hgrn" else 3
    n_in = n_src + n_par + (1 if has_s0 else 0)
    n_out = 2 + (len(in_widths) - n_act if in_widths else 0)
    ins, (o_ref, sout_ref), scr = refs[:n_in], refs[n_in:n_in + 2], refs[n_in + n_out:]
    pars = ins[n_src:n_src + n_par]
    st_ref = scr[0]
    pad_refs = scr[1:]
    t_idx = pl.program_id(1)
    padded = rows_in < chunk
    assert (padded and not in_widths) or nb == 1

    @pl.when(t_idx == 0)
    def _():
        for i in range(nb):
            for h in range(N_HEADS):
                if has_s0:
                    st_ref[i * N_HEADS + h] = ins[-1][i, h]
                else:
                    st_ref[i * N_HEADS + h] = jnp.zeros(st_ref.shape[1:], F32)

    if in_widths:
        x_ref, gm_ref, win_ref = ins[:n_src]
        hx = _rms(x_ref[0], gm_ref[...]).astype(BF16)
        offs = [sum(in_widths[:i]) for i in range(len(in_widths))]
        ld = [jnp.dot(hx, win_ref[:, o:o + w], preferred_element_type=F32) for o, w in zip(offs, in_widths)]
        for extra_ref, extra in zip(refs[n_in + 2:n_in + n_out], ld[n_act:]):
            extra_ref[0] = extra
        ld = ld[:n_act]
    elif padded:
        for p_ref, a_ref in zip(pad_refs, ins[:n_act]):
            p_ref[...] = jnp.zeros(p_ref.shape, F32)
            for i in range(nb):
                p_ref[i * chunk:i * chunk + rows_in, :] = a_ref[i]
        ld = [a[...] for a in pad_refs]
    else:
        ld = [a[0] for a in ins[:n_act]]
    groups = nb if padded else n_chunks
    span = groups * chunk

    if mode == "hgrn":
        q_raw, f, v, gate = ld
        lb_ref, gn_ref = pars
        lbv = lb_ref[...]
        e = jnp.exp(lbv - jnp.max(lbv, axis=0, keepdims=True))
        lb = jnp.sum(e[0:layer + 1], axis=0, keepdims=True) / jnp.sum(e, axis=0, keepdims=True)
        forget = lb + (1.0 - lb) * _sigmoid(f)
        k = 1.0 - forget
        lg = jnp.log(forget)
        q = _silu(q_raw)
    else:
        q_raw, k, v, gate, gd = ld
        wg_ref, bg_ref, gn_ref = pars
        z = jnp.dot(gd.astype(BF16), wg_ref[...], preferred_element_type=F32) + bg_ref[...]
        lg = (jnp.minimum(z, 0.0) - jnp.log(1.0 + jnp.exp(-jnp.abs(z)))) / GLA_GATE_TAU
        q = q_raw * (HEAD_DK ** -0.5)
    if padded:
        live = lax.broadcasted_iota(jnp.int32, (span, 1), 0) % chunk < rows_in
        lg = jnp.where(live, lg, 0.0)
        k = jnp.where(live, k, 0.0)
    b = _cumsum_rows(lg, chunk)

    def piece(x, i, h, width):
        return x[i * chunk:(i + 1) * chunk, h * width:(h + 1) * width]

    def heads(small_decay):
        rows = []
        for i in range(groups):
            cols = []
            for h in range(N_HEADS):
                cols.append(_gla_head(piece(q, i, h, HEAD_DK), piece(k, i, h, HEAD_DK),
                                      piece(b, i, h, HEAD_DK), piece(v, i, h, dv),
                                      st_ref.at[(i if padded else 0) * N_HEADS + h], small_decay,
                                      rows_in if padded else chunk))
            rows.append(jnp.concatenate(cols, axis=1))
        return rows[0] if groups == 1 else jnp.concatenate(rows, axis=0)

    if chunk <= SUB_BLOCK:
        o_raw = heads(False)
    else:
        o_raw = lax.cond(jnp.min(b) >= -GLA_SAFE_DECAY,
                         functools.partial(heads, True), functools.partial(heads, False))
    o_all = jnp.concatenate(
        [_rms(o_raw[:, h * dv:(h + 1) * dv], gn_ref[...]) * _silu(gate[:, h * dv:(h + 1) * dv])
         for h in range(N_HEADS)], axis=1).astype(o_ref.dtype)
    if padded:
        for i in range(nb):
            o_ref[i] = o_all[i * chunk:i * chunk + rows_in]
    else:
        o_ref[0] = o_all

    @pl.when(t_idx == pl.num_programs(1) - 1)
    def _():
        for i in range(nb):
            for h in range(N_HEADS):
                sout_ref[i, h] = st_ref[i * N_HEADS + h]


def _gla_call(mode, src, params, s0, *, layer, dv, tb, chunk, nb, extra_widths=()):
    fused_in = isinstance(src, tuple)
    proj = src[0] if fused_in else src
    bsz, seq, _ = proj.shape
    rows_in = min(tb, seq)
    if rows_in < chunk:
        assert seq == rows_in and bsz % nb == 0
        nt, n_chunks = 1, 1
    else:
        assert seq % tb == 0 and tb % chunk == 0 and nb == 1
        nt, n_chunks = seq // tb, tb // chunk
    kw, vw = N_HEADS * HEAD_DK, N_HEADS * dv

    def act(width, col_block):
        return pl.BlockSpec((nb, rows_in, width), lambda b, t: (b, t, col_block))

    def whole(a):
        return pl.BlockSpec(a.shape, lambda b, t: (0,) * a.ndim)

    if mode == "hgrn":
        act_specs = [act(kw, 0), act(kw, 1), act(vw, 2), act(vw, 3)]
        act_widths = [kw, kw, vw, vw]
    else:
        act_specs = [act(kw, 0), act(kw, 1), act(vw, kw * 2 // vw), act(vw, kw * 2 // vw + 1),
                     act(128, (2 * kw + 2 * vw) // 128)]
        act_widths = [kw, kw, vw, vw, 128]
    if fused_in:
        x, g_mix, w_in = src[0], _whole(src[1]), _whole(src[2])
        in_specs = [pl.BlockSpec((nb, rows_in, x.shape[-1]), lambda b, t: (b, t, 0)), g_mix[1], w_in[1]]
        args = [x, g_mix[0], w_in[0]]
    else:
        in_specs, args = act_specs, [proj] * len(act_specs)
    in_specs = in_specs + [whole(p) for p in params]
    args = args + list(params)
    state_spec = pl.BlockSpec((nb, N_HEADS, HEAD_DK, dv), lambda b, t: (b, 0, 0, 0))
    if s0 is not None:
        in_specs.append(state_spec)
        args.append(s0)
    scratch = [pltpu.VMEM((nb * N_HEADS, HEAD_DK, dv), F32)]
    if rows_in < chunk:
        scratch += [pltpu.VMEM((nb * chunk, w), F32) for w in act_widths]
    return pl.pallas_call(
        functools.partial(_gla_kernel, mode=mode, layer=layer, dv=dv, nb=nb, rows_in=rows_in,
                          chunk=chunk, n_chunks=n_chunks, has_s0=s0 is not None,
                          in_widths=tuple(act_widths) + tuple(extra_widths) if fused_in else None),
        grid=(bsz // nb, nt),
        in_specs=in_specs,
        out_specs=[pl.BlockSpec((nb, rows_in, vw), lambda b, t: (b, t, 0)), state_spec]
        + [pl.BlockSpec((nb, rows_in, w), lambda b, t: (b, t, 0)) for w in extra_widths],
        out_shape=[jax.ShapeDtypeStruct((bsz, seq, vw), BF16),
                   jax.ShapeDtypeStruct((bsz, N_HEADS, HEAD_DK, dv), F32)]
        + [jax.ShapeDtypeStruct((bsz, seq, w), F32) for w in extra_widths],
        scratch_shapes=scratch,
        compiler_params=_cparams("arbitrary", "arbitrary"),
        name="gla_" + mode,
    )(*args)


def _s5_prep_kernel(lre_ref, lim_ref, ls_ref, lre_x_ref, lim_x_ref, ls_x_ref, bre_ref, bim_ref,
                    are_ref, aim_ref, bbre_ref, bbim_ref):
    def disc(lre, lim, ls):
        lr = jnp.minimum(lre, S5_MAX_RE)
        dt = jnp.exp(ls)
        mag = jnp.exp(lr * dt)
        a_re = mag * jnp.cos(lim * dt)
        a_im = mag * jnp.sin(lim * dt)
        den = lr * lr + lim * lim
        z_re = ((a_re - 1.0) * lr + a_im * lim) / den
        z_im = (a_im * lr - (a_re - 1.0) * lim) / den
        return a_re, a_im, z_re, z_im

    a_re, a_im, _, _ = disc(lre_ref[...], lim_ref[...], ls_ref[...])
    are_ref[...] = a_re
    aim_ref[...] = a_im
    _, _, z_re, z_im = disc(lre_x_ref[...], lim_x_ref[...], ls_x_ref[...])
    bbre_ref[...] = z_re * bre_ref[...] - z_im * bim_ref[...]
    bbim_ref[...] = z_re * bim_ref[...] + z_im * bre_ref[...]


def _s5_prep(lam_re, lam_im, log_step, b_re, b_im):
    g, n = lam_re.shape
    p = b_re.shape[-1]
    ls = jnp.broadcast_to(log_step[:, None], (g, n))
    rep = lambda a: jnp.repeat(a, p, axis=1)
    outs = pl.pallas_call(
        _s5_prep_kernel,
        out_shape=[jax.ShapeDtypeStruct((g, n), F32)] * 2 + [jax.ShapeDtypeStruct((g, n * p), F32)] * 2,
        name="s5_prep",
    )(lam_re, lam_im, ls, rep(lam_re), rep(lam_im), rep(ls),
      b_re.reshape(g, n * p), b_im.reshape(g, n * p))
    a_re, a_im, bb_re, bb_im = outs
    return a_re, a_im, bb_re.reshape(g, n, p), bb_im.reshape(g, n, p)


def _s5_kernel(*refs, bg, tc, nt, cw, has_x0):
    n_in = 9 if has_x0 else 7
    u_ref, bm_ref, cm_ref, a_ref, d_ref, wg_ref, bgl_ref = refs[:7]
    o_ref, sre_ref, sim_ref = refs[n_in:n_in + 3]
    utm_ref, xs_ref, st_ref = refs[n_in + 3:]
    t_idx = pl.program_id(1)
    half = xs_ref.shape[1] // 2
    uw = u_ref.shape[-1] // 2

    @pl.when(t_idx == 0)
    def _():
        for hf in range(2):
            if has_x0:
                st_ref[:, hf * 2 * half:hf * 2 * half + half] = refs[7][:, hf * half:(hf + 1) * half]
                st_ref[:, hf * 2 * half + half:(hf + 1) * 2 * half] = refs[8][:, hf * half:(hf + 1) * half]
            else:
                st_ref[...] = jnp.zeros(st_ref.shape, F32)

    for t in range(tc):
        utm_ref[t * bg:(t + 1) * bg, :] = u_ref[:, t, :]
    u = utm_ref[...]
    ys = []
    for hf in range(2):
        xs_ref[...] = jnp.dot(u[:, hf * uw:(hf + 1) * uw].astype(BF16), bm_ref[hf],
                              preferred_element_type=F32)
        base = hf * 2 * half
        for c0 in range(0, half, cw):
            ar = a_ref[0:1, base + c0:base + c0 + cw]
            ai = a_ref[0:1, base + half + c0:base + half + c0 + cw]
            xr = st_ref[:, base + c0:base + c0 + cw]
            xi = st_ref[:, base + half + c0:base + half + c0 + cw]
            for t in range(tc):
                rows = slice(t * bg, (t + 1) * bg)
                nr = ar * xr - ai * xi + xs_ref[rows, c0:c0 + cw]
                ni = ar * xi + ai * xr + xs_ref[rows, half + c0:half + c0 + cw]
                xs_ref[rows, c0:c0 + cw] = nr
                xs_ref[rows, half + c0:half + c0 + cw] = ni
                xr, xi = nr, ni
            st_ref[:, base + c0:base + c0 + cw] = xr
            st_ref[:, base + half + c0:base + half + c0 + cw] = xi
        ys.append(jnp.dot(xs_ref[...].astype(BF16), cm_ref[hf], preferred_element_type=F32))
    y = jnp.concatenate(ys, axis=1) + d_ref[...] * u
    y = 0.5 * y * (1.0 + jnp.tanh(math.sqrt(2.0 / math.pi) * (y + 0.044715 * (y * y * y))))
    gate = jnp.dot(y.astype(BF16), wg_ref[...], preferred_element_type=F32) + bgl_ref[...]
    utm_ref[...] = y * _sigmoid(gate)
    for t in range(tc):
        o_ref[:, t, :] = utm_ref[t * bg:(t + 1) * bg, :]

    @pl.when(t_idx == nt - 1)
    def _():
        for hf in range(2):
            sre_ref[:, hf * half:(hf + 1) * half] = st_ref[:, hf * 2 * half:hf * 2 * half + half]
            sim_ref[:, hf * half:(hf + 1) * half] = st_ref[:, hf * 2 * half + half:(hf + 1) * 2 * half]


def _s5_call(u_src, col_block, mats, x0, *, tc, cw):
    bg, seq, _ = u_src.shape
    assert seq % tc == 0
    nt = seq // tc
    bmat, cmat, a_flat, d_row, w_glu, b_glu = mats
    width = d_row.shape[1]
    nstate = a_flat.shape[1] // 2
    whole = lambda a: pl.BlockSpec(a.shape, lambda g, t: (0,) * a.ndim)
    in_specs = [pl.BlockSpec((bg, tc, width), lambda g, t: (0, t, col_block))]
    in_specs += [whole(m) for m in mats]
    args = [u_src] + list(mats)
    st_spec = pl.BlockSpec((bg, nstate), lambda g, t: (0, 0))
    if x0 is not None:
        in_specs += [st_spec, st_spec]
        args += list(x0)
    scratch = [pltpu.VMEM((bg * tc, width), F32), pltpu.VMEM((bg * tc, nstate), F32),
               pltpu.VMEM((bg, 2 * nstate), F32)]
    return pl.pallas_call(
        functools.partial(_s5_kernel, bg=bg, tc=tc, nt=nt, cw=cw, has_x0=x0 is not None),
        grid=(1, nt),
        in_specs=in_specs,
        out_specs=[pl.BlockSpec((bg, tc, width), lambda g, t: (0, t, 0)), st_spec, st_spec],
        out_shape=[jax.ShapeDtypeStruct((bg, seq, width), F32),
                   jax.ShapeDtypeStruct((bg, nstate), F32),
                   jax.ShapeDtypeStruct((bg, nstate), F32)],
        scratch_shapes=scratch,
        compiler_params=_cparams("arbitrary", "arbitrary"),
        name="s5",
    )(*args)


def _s5_matrices(a_re, a_im, bb_re, bb_im, c_re, c_im, d, w_glu, b_glu):
    g, n, p = bb_re.shape
    gh = g // 2
    eye = jnp.eye(gh, dtype=F32)

    def block_diag(t):
        return (eye[:, None, :, None] * t[:, :, None, :]).reshape(gh * t.shape[1], gh * t.shape[2])

    def in_mat(bb):
        return block_diag(bb.transpose(0, 2, 1))

    def out_mat(cc):
        return block_diag(cc.transpose(0, 2, 1))

    bmat = jnp.stack([jnp.concatenate([in_mat(bb_re[h * gh:(h + 1) * gh]),
                                       in_mat(bb_im[h * gh:(h + 1) * gh])], axis=1)
                      for h in range(2)]).astype(BF16)
    cmat = jnp.stack([jnp.concatenate([out_mat(c_re[h * gh:(h + 1) * gh]),
                                       out_mat(-c_im[h * gh:(h + 1) * gh])], axis=0)
                      for h in range(2)]).astype(BF16)
    a_flat = jnp.concatenate([jnp.concatenate([a_re[h * gh:(h + 1) * gh].reshape(1, gh * n),
                                               a_im[h * gh:(h + 1) * gh].reshape(1, gh * n)], axis=1)
                              for h in range(2)], axis=1)
    return (bmat, cmat, a_flat, d.reshape(1, g * p), w_glu.astype(BF16), b_glu.reshape(1, -1))


def _xattn_kernel(*refs, n_pre):
    x_ref = refs[0]
    g_ref, wq_ref, wo_ref, k_ref, v_ref, y_ref, kt_ref, vt_ref = refs[1 + 2 * n_pre:]

    @pl.when(pl.program_id(1) == 0)
    def _():
        kt_ref[...] = jnp.transpose(k_ref[...], (1, 0, 2)).astype(BF16)
        vt_ref[...] = jnp.transpose(v_ref[...], (1, 0, 2)).astype(BF16)

    x = x_ref[...]
    for a_ref, w_ref in zip(refs[1:1 + n_pre], refs[1 + n_pre:1 + 2 * n_pre]):
        x = x + jnp.dot(a_ref[...].astype(BF16), w_ref[...], preferred_element_type=F32)
    q = jnp.dot(_rms(x, g_ref[...]).astype(BF16), wq_ref[...], preferred_element_type=F32).astype(BF16)
    hd = q.shape[1] // N_HEADS
    outs = []
    for h in range(N_HEADS):
        s = lax.dot_general(q[:, h * hd:(h + 1) * hd], kt_ref[h], _NT,
                            preferred_element_type=F32) * (hd ** -0.5)
        p = jnp.exp(s - jnp.max(s, axis=-1, keepdims=True))
        p = p / jnp.sum(p, axis=-1, keepdims=True)
        outs.append(jnp.dot(p.astype(BF16), vt_ref[h], preferred_element_type=F32).astype(BF16))
    y_ref[...] = x + jnp.dot(jnp.concatenate(outs, axis=1), wo_ref[...], preferred_element_type=F32)


def _xattn_call(x, pre_a, pre_w, g, w_q, w_o, mem_k, mem_v, layer, *, tq):
    bsz, seq, d = x.shape
    n_mem, nh, hd = mem_k.shape[2:]
    assert seq % tq == 0
    kv_spec = pl.BlockSpec((None, None, n_mem, nh, hd), lambda b, t: (layer, b, 0, 0, 0))
    rows = lambda a: pl.BlockSpec((None, tq, a.shape[-1]), lambda b, t: (b, t, 0))
    params = [_whole(w) for w in (*pre_w, g, w_q, w_o)]
    return pl.pallas_call(
        functools.partial(_xattn_kernel, n_pre=len(pre_a)),
        grid=(bsz, seq // tq),
        in_specs=[rows(x)] + [rows(a) for a in pre_a] + [spec for _, spec in params] + [kv_spec, kv_spec],
        out_specs=rows(x),
        out_shape=jax.ShapeDtypeStruct((bsz, seq, d), F32),
        scratch_shapes=[pltpu.VMEM((nh, n_mem, hd), BF16), pltpu.VMEM((nh, n_mem, hd), BF16)],
        compiler_params=_cparams("arbitrary", "arbitrary"),
        name="xattn",
    )(x, *pre_a, *[w for w, _ in params], mem_k, mem_v)


def _attn_rows_kernel(q_ref, k_ref, v_ref, o_ref, pad_ref, *, rows_in):
    nb, rows, d = pad_ref.shape
    nblk = d // 128
    half_blk = nblk // 2
    lanes = k_ref.shape[1]
    lane_blk = lax.broadcasted_iota(jnp.int32, (1, lanes), 1) % nblk
    row_head = lax.broadcasted_iota(jnp.int32, (N_HEADS * rows, 1), 0) // rows
    live = lane_blk == row_head
    scale = (d // N_HEADS) ** -0.5
    for i in range(nb):
        pad_ref[i] = jnp.zeros((rows, d), F32)
        pad_ref[i, 0:rows_in, :] = q_ref[i]
        q = pad_ref[i]
        qx = jnp.concatenate([q[:, j * 128:(j + 1) * 128] for j in range(nblk)], axis=0)
        g = lax.dot_general(qx.astype(BF16), k_ref[i].astype(BF16), _NT, preferred_element_type=F32)
        s = jnp.concatenate(
            [g[2 * h * rows:(2 * h + 1) * rows]
             + pltpu.roll(g[(2 * h + 1) * rows:(2 * h + 2) * rows], lanes - half_blk, axis=1)
             for h in range(N_HEADS)], axis=0) * scale
        s = jnp.where(live, s, -1e30)
        e = jnp.exp(s - jnp.max(s, axis=-1, keepdims=True))
        p = e / jnp.sum(e, axis=-1, keepdims=True)
        px = jnp.concatenate(
            [blk for h in range(N_HEADS)
             for blk in (p[h * rows:(h + 1) * rows], pltpu.roll(p[h * rows:(h + 1) * rows], half_blk, axis=1))],
            axis=0)
        o = jnp.dot(px.astype(BF16), v_ref[i].astype(BF16), preferred_element_type=F32)
        o = jnp.concatenate([o[j * rows:(j + 1) * rows] for j in range(nblk)], axis=1)
        o_ref[i] = o[0:rows_in].astype(o_ref.dtype)


def _attn_rows_call(q, mem_k, mem_v, layer, *, nb):
    bsz, seq, d = q.shape
    depth, _, n_mem, nh, hd = mem_k.shape
    assert bsz % nb == 0 and hd == 256 and nh == N_HEADS and seq <= 16
    as_rows = lambda a: a.reshape(depth, bsz, n_mem, nh, 2, 128).transpose(0, 1, 2, 4, 3, 5).reshape(
        depth, bsz, n_mem * 2 * nh, 128)
    kv_spec = pl.BlockSpec((None, nb, n_mem * 2 * nh, 128), lambda b: (layer, b, 0, 0))
    return pl.pallas_call(
        functools.partial(_attn_rows_kernel, rows_in=seq),
        grid=(bsz // nb,),
        in_specs=[pl.BlockSpec((nb, seq, d), lambda b: (b, 0, 0)), kv_spec, kv_spec],
        out_specs=pl.BlockSpec((nb, seq, d), lambda b: (b, 0, 0)),
        out_shape=jax.ShapeDtypeStruct((bsz, seq, d), BF16),
        scratch_shapes=[pltpu.VMEM((nb, 8 * pl.cdiv(seq, 8), d), F32)],
        compiler_params=_cparams("arbitrary"),
        name="mem_attn_rows",
    )(q, as_rows(mem_k), as_rows(mem_v))


FFN_COLS = 256


def _ffn_kernel(*refs, tm, ts, hs, f_dim, has_hist, has_final):
    n_in = 6 + int(has_hist) + int(has_final)
    x_ref, g_ref, wup_ref, cw_ref, cb_ref, wdn_ref = refs[:6]
    y_ref, state_ref = refs[n_in:n_in + 2]
    gated_ref, hist_ref = refs[n_in + 2:n_in + 4]
    t_idx = pl.program_id(1)

    @pl.when(t_idx == 0)
    def _():
        hist_ref[...] = jnp.zeros(hist_ref.shape, F32)
        if has_hist:
            hist_ref[hs - 2 * ts:hs, :] = refs[6][...]

    x = x_ref[...]
    h = _rms(x, g_ref[...]).astype(BF16)
    row = lax.broadcasted_iota(jnp.int32, (tm, 1), 0)
    for c in range(f_dim // FFN_COLS):
        conv = []
        for part in range(2):
            cols = slice(part * f_dim + c * FFN_COLS, part * f_dim + (c + 1) * FFN_COLS)
            u = jnp.dot(h, wup_ref[:, cols], preferred_element_type=F32)
            if ts == 1:
                prev2, prev1 = hist_ref[hs - 2:hs - 1, cols], hist_ref[hs - 1:hs, cols]
                m1 = jnp.where(row == 0, prev1, pltpu.roll(u, 1, axis=0))
                m2 = jnp.where(row == 0, prev2, jnp.where(row == 1, prev1, pltpu.roll(u, 2, axis=0)))
            else:
                ext = jnp.concatenate([hist_ref[hs - 2 * ts:hs, cols], u], axis=0)
                m2, m1 = ext[0:tm], ext[ts:ts + tm]
            conv.append(cb_ref[:, cols] + cw_ref[0:1, cols] * m2 + cw_ref[1:2, cols] * m1
                        + cw_ref[2:3, cols] * u)
            hist_ref[:, cols] = u[tm - hs:tm]
        gated_ref[:, c * FFN_COLS:(c + 1) * FFN_COLS] = (_silu(conv[0]) * conv[1]).astype(BF16)
    out = x + jnp.dot(gated_ref[...], wdn_ref[...], preferred_element_type=F32)
    if has_final:
        out = _rms(out, refs[n_in - 1][...])
    y_ref[...] = out

    @pl.when(t_idx == pl.num_programs(1) - 1)
    def _():
        state_ref[...] = hist_ref[hs - 2 * ts:hs, :]


def _ffn_call(x, g, w_up, conv_w, conv_b, w_down, hist0, g_final, *, tm, ts):
    ngrp, rows, d = x.shape
    params = [_whole(a) for a in (g, w_up, conv_w, conv_b, w_down)]
    f2 = params[1][0].shape[-1]
    f_dim = f2 // 2
    hs = max(8, 2 * ts)
    assert rows % tm == 0 and tm >= hs and f_dim % FFN_COLS == 0 and (ts == 1 or ts % 8 == 0)
    args = [x] + [a for a, _ in params]
    in_specs = [pl.BlockSpec((None, tm, d), lambda s, t: (s, t, 0))] + [spec for _, spec in params]
    st_spec = pl.BlockSpec((None, 2 * ts, f2), lambda s, t: (s, 0, 0))
    if hist0 is not None:
        in_specs.append(st_spec)
        args.append(hist0)
    if g_final is not None:
        in_specs.append(_whole(g_final)[1])
        args.append(g_final)
    return pl.pallas_call(
        functools.partial(_ffn_kernel, tm=tm, ts=ts, hs=hs, f_dim=f_dim,
                          has_hist=hist0 is not None, has_final=g_final is not None),
        grid=(ngrp, rows // tm),
        in_specs=in_specs,
        out_specs=[pl.BlockSpec((None, tm, d), lambda s, t: (s, t, 0)), st_spec],
        out_shape=[jax.ShapeDtypeStruct((ngrp, rows, d), F32),
                   jax.ShapeDtypeStruct((ngrp, 2 * ts, f2), F32)],
        scratch_shapes=[pltpu.VMEM((tm, f_dim), BF16), pltpu.VMEM((hs, f2), F32)],
        compiler_params=_cparams("arbitrary", "arbitrary"),
        name="conv_ffn",
    )(*args)


def _trunk(x, mem_k, mem_v, states, p, *, prompt):
    bsz, seq, d = x.shape
    depth = p["norm_mix"].shape[0]
    m = bsz * seq
    x2 = x.reshape(m, d)
    new = {"hgrn": [], "s5_re": [], "s5_im": [], "gla": [], "conv": []}
    if prompt:
        gla_tiles = dict(tb=1024, chunk=64, nb=1)
        attn_tq = 1024
    else:
        gla_tiles = dict(tb=seq, chunk=16, nb=8)
        attn_tq = seq
    for l in range(depth):
        g_mix = _pick(p["norm_mix"], l)
        if l % 2 == 0:
            e = l // 2
            kw = N_HEADS * HEAD_DK
            x3 = x2.reshape(bsz, seq, d)
            hgrn_par = [p["hgrn_lb"], p["hgrn_gnorm"][e].reshape(1, -1)]
            mats = p["s5_mats"][e]
            if prompt:
                o_a, s_a, u = _gla_call("hgrn", (x3, g_mix, _pick(p["w_in_ab"], e)), hgrn_par, None,
                                        layer=l, dv=kw // N_HEADS, extra_widths=(kw,), **gla_tiles)
                o_b, sr, si = _s5_call(u, 0, mats, None, tc=128, cw=512)
            else:
                proj = _norm_proj(x2, g_mix, _pick(p["w_in_ab"], e)).reshape(bsz, seq, -1)
                o_a, s_a = _gla_call("hgrn", proj, hgrn_par, states["hgrn"][e],
                                     layer=l, dv=kw // N_HEADS, **gla_tiles)
                x0 = (states["s5_re"][e].reshape(bsz, -1), states["s5_im"][e].reshape(bsz, -1))
                o_b, sr, si = _s5_call(proj, 4, mats, x0, tc=seq, cw=128)
            mixed = [o_a, o_b]
            mixed_w = [_pick(p["w_out_ab"], e, rows=kw, row_block=0), _pick(p["w_out_ab"], e, rows=kw, row_block=1)]
            new["hgrn"].append(s_a)
            new["s5_re"].append(sr.reshape(bsz, -1, S5_STATE))
            new["s5_im"].append(si.reshape(bsz, -1, S5_STATE))
        else:
            o_idx = l // 2
            if prompt:
                src_c = (x2.reshape(bsz, seq, d), g_mix, _pick(p["w_in_c"], o_idx))
            else:
                src_c = _norm_proj(x2, g_mix, _pick(p["w_in_c"], o_idx)).reshape(bsz, seq, -1)
            o_c, s_c = _gla_call(
                "gla", src_c,
                [p["gla_w_gate"][o_idx], p["gla_b_gate"][o_idx].reshape(1, -1),
                 p["gla_gnorm"][o_idx].reshape(1, -1)],
                None if states is None else states["gla"][o_idx],
                layer=l, dv=d // N_HEADS, **{**gla_tiles, **(dict(chunk=256) if prompt else {})})
            mixed, mixed_w = [o_c], [_pick(p["w_out_c"], o_idx)]
            new["gla"].append(s_c)
        g_cross, w_q, w_o = _pick(p["norm_cross"], l), _pick(p["xa_w_q"], l), _pick(p["xa_w_o"], l)
        if prompt:
            x2 = _xattn_call(x2.reshape(bsz, seq, d), mixed, mixed_w, g_cross, w_q, w_o,
                             mem_k, mem_v, l, tq=attn_tq).reshape(m, d)
        else:
            x2, q = _res_norm_proj(x2, [a.reshape(m, -1) for a in mixed], mixed_w, g_cross, w_q)
            o_x = _attn_rows_call(q.reshape(bsz, seq, d), mem_k, mem_v, l, nb=8)
            x2 = _proj_res(x2, [o_x.reshape(m, d)], [w_o])
        g_final = p["norm_final"] if l == depth - 1 else None
        ffn_w = tuple(_pick(p[name], l) for name in
                      ("norm_ffn", "ffn_w_up", "ffn_conv_w", "ffn_conv_b", "ffn_w_down"))
        if prompt:
            y, cst = _ffn_call(x2.reshape(bsz, seq, d), *ffn_w, None, g_final, tm=1024, ts=1)
            x2 = y.reshape(m, d)
        else:
            xt = x2.reshape(bsz, seq, d).transpose(1, 0, 2).reshape(1, m, d)
            hist0 = states["conv"][l].transpose(1, 0, 2).reshape(1, 2 * bsz, -1)
            y, cst = _ffn_call(xt, *ffn_w, hist0, g_final, tm=m, ts=bsz)
            x2 = y.reshape(seq, bsz, d).transpose(1, 0, 2).reshape(m, d)
            cst = cst.reshape(2, bsz, -1).transpose(1, 0, 2)
        new["conv"].append(cst)
    return x2.reshape(bsz, seq, d), new


def kernel(x_prompt, x_sample, mem_prompt, cache_mem_k, cache_mem_v, state_hgrn, state_s5_re, state_s5_im, state_gla, state_ffn_conv, norm_mix, norm_cross, norm_mem, norm_ffn, norm_final, w_in_ab, hgrn_lb, hgrn_gnorm, s5_lam_re, s5_lam_im, s5_log_step, s5_b_re, s5_b_im, s5_c_re, s5_c_im, s5_d, s5_w_glu, s5_b_glu, w_out_ab, w_in_c, gla_w_gate_up, gla_b_gate, gla_gnorm, w_out_c, xa_w_q, xa_w_kv, xa_w_o, ffn_w_up, ffn_conv_w, ffn_conv_b, ffn_w_down):
    depth, d = norm_mix.shape

    gla_cols = w_in_c.shape[2]
    gate_rank = gla_w_gate_up.shape[1]
    pad_c = (-gla_cols) % 128
    w_in_c_p = jnp.pad(w_in_c, ((0, 0), (0, 0), (0, pad_c))).astype(BF16)
    gla_w_gate = jnp.pad(gla_w_gate_up, ((0, 0), (0, 128 - gate_rank), (0, 0))).astype(BF16)

    s5_mats = []
    for e in range(s5_lam_re.shape[0]):
        a_re, a_im, bb_re, bb_im = _s5_prep(s5_lam_re[e], s5_lam_im[e], s5_log_step[e],
                                            s5_b_re[e], s5_b_im[e])
        s5_mats.append(_s5_matrices(a_re, a_im, bb_re, bb_im, s5_c_re[e], s5_c_im[e], s5_d[e],
                                    s5_w_glu[e], s5_b_glu[e]))

    row = lambda a: a.reshape(a.shape[0], 1, a.shape[1])
    p = dict(norm_mix=row(norm_mix), norm_cross=row(norm_cross), norm_ffn=row(norm_ffn),
             norm_final=norm_final.reshape(1, d),
             w_in_ab=w_in_ab.astype(BF16), hgrn_lb=hgrn_lb, hgrn_gnorm=hgrn_gnorm, s5_mats=s5_mats,
             w_out_ab=w_out_ab.astype(BF16), w_in_c=w_in_c_p, gla_w_gate=gla_w_gate,
             gla_b_gate=gla_b_gate, gla_gnorm=gla_gnorm, w_out_c=w_out_c.astype(BF16),
             xa_w_q=xa_w_q.astype(BF16), xa_w_o=xa_w_o.astype(BF16),
             ffn_w_up=ffn_w_up.astype(BF16), ffn_conv_w=ffn_conv_w, ffn_conv_b=row(ffn_conv_b),
             ffn_w_down=ffn_w_down.astype(BF16))

    mem_k_p, mem_v_p = _mem_kv(mem_prompt, norm_mem.reshape(depth, 1, d), xa_w_kv.astype(BF16))
    y_prompt, st_p = _trunk(x_prompt, mem_k_p, mem_v_p, None, p, prompt=True)

    states = dict(hgrn=state_hgrn, s5_re=state_s5_re, s5_im=state_s5_im, gla=state_gla,
                  conv=state_ffn_conv)
    y_sample, st_s = _trunk(x_sample, cache_mem_k, cache_mem_v, states, p, prompt=False)

    stack = lambda xs: xs[0][None] if len(xs) == 1 else jnp.stack(xs)
    return (y_prompt, y_sample,
            stack(st_p["hgrn"]), stack(st_p["s5_re"]), stack(st_p["s5_im"]), stack(st_p["gla"]),
            mem_k_p, mem_v_p, stack(st_p["conv"]),
            stack(st_s["hgrn"]), stack(st_s["s5_re"]), stack(st_s["s5_im"]), stack(st_s["gla"]),
            stack(st_s["conv"]))
```
